```python
import math
import jax, jax.numpy as jnp
from jax import lax
import numpy as np

D_MODEL = 1024
BATCH = 8
SEQ = 2048
DEPTH = 2

N_HEADS = 4
HEAD_DIM = 64
V_DIM = 2 * HEAD_DIM
QK_WIDTH = N_HEADS * 2 * HEAD_DIM
ATTN_WIDTH = N_HEADS * V_DIM
SSM_WIDTH = D_MODEL // 2
SSM_GROUP = 16
SSM_GROUPS = SSM_WIDTH // SSM_GROUP
SSM_STATE = 64
STEP_MIN = 1e-3
STEP_MAX = 1e-1
D_FF = -(-8 * D_MODEL // (3 * 256)) * 256
N_BUCKETS = 32
MAX_DISTANCE = 128
Q_BLOCK = 128
EPS = 1e-6
NEG = -1e30
IN_WIDTH = 2 * QK_WIDTH + ATTN_WIDTH + SSM_WIDTH + 2 * D_MODEL

kernel_name = "hybrid_diffattn_s5_gated_block"


def rmsnorm(x, g):
    xf = x.astype(jnp.float32)
    y = xf * lax.rsqrt(jnp.mean(xf * xf, axis=-1, keepdims=True) + EPS)
    return (y * g.astype(jnp.float32)).astype(x.dtype)


def t5_bucket(rel):
    n = jnp.maximum(rel, 0)
    max_exact = N_BUCKETS // 2
    is_small = n < max_exact
    nf = jnp.maximum(n, 1).astype(jnp.float32)
    large = max_exact + (jnp.log(nf / max_exact) / math.log(MAX_DISTANCE / max_exact)
                         * (N_BUCKETS - max_exact)).astype(jnp.int32)
    large = jnp.minimum(large, N_BUCKETS - 1)
    return jnp.where(is_small, n, large)


def diff_attention(q, k, v, lam, bias_table):
    B, H, _, S, Dh = q.shape
    nblk = S // Q_BLOCK
    scale = Dh ** -0.5
    kpos = jnp.arange(S)
    qb = q.reshape(B, H, 2, nblk, Q_BLOCK, Dh).transpose(3, 0, 1, 2, 4, 5)

    def block(args):
        qi, i = args
        qpos = i * Q_BLOCK + jnp.arange(Q_BLOCK)
        rel = qpos[:, None] - kpos[None, :]
        bias = bias_table[t5_bucket(rel)].astype(jnp.float32).transpose(2, 0, 1)
        s = jnp.einsum('bhmqd,bhmkd->bhmqk', qi, k,
                       preferred_element_type=jnp.float32) * scale + bias[None, :, None]
        s = jnp.where(rel[None, None, None] >= 0, s, NEG)
        p = jax.nn.softmax(s, axis=-1)
        a = p[:, :, 0] - lam * p[:, :, 1]
        return jnp.einsum('bhqk,bhkv->bhqv', a.astype(v.dtype), v)

    out = lax.map(block, (qb, jnp.arange(nblk)))
    return out.transpose(1, 2, 0, 3, 4).reshape(B, H, S, -1)


def s5_ssm(u, lam_re, lam_im, b_re, b_im, c_re, c_im, d_skip, log_step):
    Bsz, S, W = u.shape
    uf = u.astype(jnp.float32).reshape(Bsz, S, SSM_GROUPS, SSM_GROUP)
    step = jnp.exp(log_step.astype(jnp.float32))[:, None]
    lr = lam_re.astype(jnp.float32)
    li = lam_im.astype(jnp.float32)
    decay = jnp.exp(lr * step)
    ab_re = decay * jnp.cos(li * step)
    ab_im = decay * jnp.sin(li * step)
    nr = ab_re - 1.0
    ni = ab_im
    den = lr * lr + li * li
    f_re = (nr * lr + ni * li) / den
    f_im = (ni * lr - nr * li) / den
    br = b_re.astype(jnp.float32)
    bi = b_im.astype(jnp.float32)
    bb_re = f_re[..., None] * br - f_im[..., None] * bi
    bb_im = f_re[..., None] * bi + f_im[..., None] * br
    bu_re = jnp.einsum('gpc,bsgc->sbgp', bb_re, uf)
    bu_im = jnp.einsum('gpc,bsgc->sbgp', bb_im, uf)
    a_re = jnp.broadcast_to(ab_re[None, None], (S, 1, SSM_GROUPS, SSM_STATE))
    a_im = jnp.broadcast_to(ab_im[None, None], (S, 1, SSM_GROUPS, SSM_STATE))

    def combine(e1, e2):
        a1r, a1i, b1r, b1i = e1
        a2r, a2i, b2r, b2i = e2
        return (a2r * a1r - a2i * a1i,
                a2r * a1i + a2i * a1r,
                a2r * b1r - a2i * b1i + b2r,
                a2r * b1i + a2i * b1r + b2i)

    _, _, xr, xi = lax.associative_scan(combine, (a_re, a_im, bu_re, bu_im), axis=0)
    y = (jnp.einsum('gcp,sbgp->bsgc', c_re.astype(jnp.float32), xr)
         - jnp.einsum('gcp,sbgp->bsgc', c_im.astype(jnp.float32), xi))
    y = y.reshape(Bsz, S, W) + d_skip.astype(jnp.float32) * uf.reshape(Bsz, S, W)
    return y.astype(u.dtype)


def setup_inputs(seed: int = 0) -> dict:
    key = jax.random.key(seed)
    ks = iter(jax.random.split(key, 32))
    f32 = jnp.float32

    def nrm(shape, scale):
        return jax.random.normal(next(ks), shape, f32) * scale

    n = jnp.arange(SSM_STATE, dtype=f32)
    lam_re = -0.5 + nrm((DEPTH, SSM_GROUPS, SSM_STATE), 1e-3)
    lam_im = math.pi * n[None, None, :] + nrm((DEPTH, SSM_GROUPS, SSM_STATE), 1e-3)
    log_step = jax.random.uniform(next(ks), (DEPTH, SSM_GROUPS), f32,
                                  math.log(STEP_MIN), math.log(STEP_MAX))
    return {
        "x": nrm((BATCH, SEQ, D_MODEL), 1.0),
        "rel_bias": nrm((N_BUCKETS, N_HEADS), 0.5),
        "norm_mix": 1.0 + nrm((DEPTH, D_MODEL), 0.02),
        "w_in": nrm((DEPTH, D_MODEL, IN_WIDTH), D_MODEL ** -0.5),
        "q_gain": 1.0 + nrm((DEPTH, HEAD_DIM), 0.02),
        "k_gain": 1.0 + nrm((DEPTH, HEAD_DIM), 0.02),
        "lambda_q1": nrm((DEPTH, HEAD_DIM), 0.1),
        "lambda_k1": nrm((DEPTH, HEAD_DIM), 0.1),
        "lambda_q2": nrm((DEPTH, HEAD_DIM), 0.1),
        "lambda_k2": nrm((DEPTH, HEAD_DIM), 0.1),
        "subln": 1.0 + nrm((DEPTH, V_DIM), 0.02),
        "w_a": nrm((DEPTH, ATTN_WIDTH, D_MODEL), ATTN_WIDTH ** -0.5),
        "lam_re": lam_re,
        "lam_im": lam_im,
        "b_re": nrm((DEPTH, SSM_GROUPS, SSM_STATE, SSM_GROUP), (2 * SSM_GROUP) ** -0.5),
        "b_im": nrm((DEPTH, SSM_GROUPS, SSM_STATE, SSM_GROUP), (2 * SSM_GROUP) ** -0.5),
        "c_re": nrm((DEPTH, SSM_GROUPS, SSM_GROUP, SSM_STATE), (2 * SSM_STATE) ** -0.5),
        "c_im": nrm((DEPTH, SSM_GROUPS, SSM_GROUP, SSM_STATE), (2 * SSM_STATE) ** -0.5),
        "d_skip": nrm((DEPTH, SSM_WIDTH), 1.0),
        "log_step": log_step,
        "w_glu": nrm((DEPTH, SSM_WIDTH, SSM_WIDTH), SSM_WIDTH ** -0.5),
        "w_b": nrm((DEPTH, SSM_WIDTH, D_MODEL), SSM_WIDTH ** -0.5),
        "w_o": nrm((DEPTH, D_MODEL, D_MODEL), D_MODEL ** -0.5),
        "norm_ffn": 1.0 + nrm((DEPTH, D_MODEL), 0.02),
        "w1": nrm((DEPTH, D_MODEL, D_FF), D_MODEL ** -0.5),
        "w3": nrm((DEPTH, D_MODEL, D_FF), D_MODEL ** -0.5),
        "w2": nrm((DEPTH, D_FF, D_MODEL), D_FF ** -0.5),
    }


def reference(x, rel_bias, norm_mix, w_in, q_gain, k_gain, lambda_q1, lambda_k1,
              lambda_q2, lambda_k2, subln, w_a, lam_re, lam_im, b_re, b_im, c_re, c_im,
              d_skip, log_step, w_glu, w_b, w_o, norm_ffn, w1, w3, w2):
    B, S, _ = x.shape
    splits = [QK_WIDTH, 2 * QK_WIDTH, 2 * QK_WIDTH + ATTN_WIDTH,
              2 * QK_WIDTH + ATTN_WIDTH + SSM_WIDTH,
              2 * QK_WIDTH + ATTN_WIDTH + SSM_WIDTH + D_MODEL]
    for l in range(DEPTH):
        lam_init = 0.8 - 0.6 * math.exp(-0.3 * l)
        h = rmsnorm(x, norm_mix[l])
        z = h @ w_in[l]
        q, k, v, u, g_a, g_b = jnp.split(z, splits, axis=-1)
        q = rmsnorm(q.reshape(B, S, N_HEADS, 2, HEAD_DIM).transpose(0, 2, 3, 1, 4), q_gain[l])
        k = rmsnorm(k.reshape(B, S, N_HEADS, 2, HEAD_DIM).transpose(0, 2, 3, 1, 4), k_gain[l])
        v = v.reshape(B, S, N_HEADS, V_DIM).transpose(0, 2, 1, 3)
        lam = (jnp.exp(jnp.sum(lambda_q1[l].astype(jnp.float32) * lambda_k1[l].astype(jnp.float32)))
               - jnp.exp(jnp.sum(lambda_q2[l].astype(jnp.float32) * lambda_k2[l].astype(jnp.float32)))
               + lam_init)
        o = diff_attention(q, k, v, lam, rel_bias)
        o = rmsnorm(o, subln[l]) * (1.0 - lam_init)
        o = o.transpose(0, 2, 1, 3).reshape(B, S, ATTN_WIDTH)
        y_a = o @ w_a[l]
        s = s5_ssm(u, lam_re[l], lam_im[l], b_re[l], b_im[l], c_re[l], c_im[l],
                   d_skip[l], log_step[l])
        s = jax.nn.gelu(s)
        s = s * jax.nn.sigmoid(s @ w_glu[l])
        y_b = s @ w_b[l]
        mixed = jax.nn.sigmoid(g_a) * y_a + jax.nn.sigmoid(g_b) * y_b
        x = x + mixed @ w_o[l]
        h = rmsnorm(x, norm_ffn[l])
        x = x + (jax.nn.silu(h @ w1[l]) * (h @ w3[l])) @ w2[l]
    return x
```

```python
import functools
import math

import jax
import jax.numpy as jnp
from jax import lax
from jax.experimental import pallas as pl
from jax.experimental.pallas import tpu as pltpu

D_MODEL = 1024
N_HEADS = 4
HEAD_DIM = 64
V_DIM = 2 * HEAD_DIM
QK_WIDTH = N_HEADS * 2 * HEAD_DIM
ATTN_WIDTH = N_HEADS * V_DIM
SSM_WIDTH = D_MODEL // 2
SSM_GROUP = 16
SSM_GROUPS = SSM_WIDTH // SSM_GROUP
SSM_STATE = 64
STATE_LANES = SSM_GROUPS * SSM_STATE
D_FF = 2816
N_BUCKETS = 32
MAX_DISTANCE = 128
EPS = 1e-6
NEG = -1e30

MXU_TILE = 256
SUBLANES = 8

TM_PROJ = 512
TQ = 256
TK = 256
SCAN_STEPS = 64
SCAN_STRIP = 512
FF_CHUNK = 256
VMEM_LIMIT = 56 * 1024 * 1024

BF16 = jnp.bfloat16
F32 = jnp.float32


def _dot(a, b):
    return jnp.dot(a, b, preferred_element_type=F32)


def _dot_nt(a, b):
    return lax.dot_general(a, b, (((1,), (1,)), ((), ())), preferred_element_type=F32)


def _rms(x, g):
    return x * lax.rsqrt(jnp.mean(x * x, axis=-1, keepdims=True) + EPS) * g


def _resident(shape):
    zeros = (0,) * len(shape)
    return pl.BlockSpec(shape, lambda *_: zeros, pipeline_mode=pl.Buffered(1))


def _bias_tiles_kernel(brel_ref, out_ref):
    h = pl.program_id(0)
    r = lax.broadcasted_iota(jnp.int32, (TQ, TK), 0)
    c = lax.broadcasted_iota(jnp.int32, (TQ, TK), 1)
    for t, off in enumerate((0, TK)):
        rel = r - c + off

        def body(n, tile):
            return jnp.where(rel == n, brel_ref[h, n], tile)

        tile = lax.fori_loop(0, MAX_DISTANCE, body, jnp.zeros((TQ, TK), F32))
        if off == 0:
            tile = jnp.where(rel < 0, NEG, tile)
        out_ref[t] = tile
    out_ref[2] = jnp.zeros((TQ, TK), F32)


def _bias_tiles(brel):
    return pl.pallas_call(
        _bias_tiles_kernel,
        grid=(N_HEADS,),
        in_specs=[pl.BlockSpec(memory_space=pltpu.SMEM)],
        out_specs=pl.BlockSpec((None, 3, TQ, TK), lambda h: (h, 0, 0, 0)),
        out_shape=jax.ShapeDtypeStruct((N_HEADS, 3, TQ, TK), F32),
        name="bias_tiles",
    )(brel)


def _inproj_kernel(x_ref, g_ref, w_ref, seg_ref, qg_ref, kg_ref,
                   q_ref, k_ref, v_ref, u_ref):
    h = _rms(x_ref[...], g_ref[...]).astype(BF16)

    def qk_norm(z, gain):
        ss = _dot((z * z).astype(BF16), seg_ref[...])
        return z * lax.rsqrt(ss * (1.0 / HEAD_DIM) + EPS) * gain

    zq = _dot(h, w_ref[:, 0:QK_WIDTH])
    q_ref[...] = qk_norm(zq, qg_ref[...]).astype(BF16)
    zk = _dot(h, w_ref[:, QK_WIDTH:2 * QK_WIDTH])
    k_ref[...] = qk_norm(zk, kg_ref[...]).astype(BF16)
    v_ref[...] = _dot(h, w_ref[:, 2 * QK_WIDTH:2 * QK_WIDTH + ATTN_WIDTH]).astype(BF16)
    u_ref[...] = _dot(h, w_ref[:, 2 * QK_WIDTH + ATTN_WIDTH:]).astype(BF16)


def _inproj(x, g, w, seg, qg, kg):
    B, S, D = x.shape
    W = w.shape[1]
    tok = lambda width: pl.BlockSpec((None, TM_PROJ, width), lambda b, i: (b, i, 0))
    return pl.pallas_call(
        _inproj_kernel,
        grid=(B, S // TM_PROJ),
        in_specs=[tok(D), _resident((1, D)), _resident((D, W)),
                  _resident((QK_WIDTH, QK_WIDTH)),
                  _resident((1, QK_WIDTH)), _resident((1, QK_WIDTH))],
        out_specs=[tok(QK_WIDTH), tok(QK_WIDTH), tok(ATTN_WIDTH),
                   pl.BlockSpec((TM_PROJ, SSM_WIDTH), lambda b, i: (i, b))],
        out_shape=[jax.ShapeDtypeStruct((B, S, QK_WIDTH), BF16),
                   jax.ShapeDtypeStruct((B, S, QK_WIDTH), BF16),
                   jax.ShapeDtypeStruct((B, S, ATTN_WIDTH), BF16),
                   jax.ShapeDtypeStruct((S, B * SSM_WIDTH), BF16)],
        compiler_params=pltpu.CompilerParams(
            dimension_semantics=("parallel", "parallel"), vmem_limit_bytes=VMEM_LIMIT),
        name="inproj",
    )(x, g, w, seg, qg, kg)


def _attn_kernel(lamv_ref, subln_ref, q_ref, k_ref, v_ref, bias_ref, o_ref,
                 m_s, l_s, acc_s, *, lam_init):
    qi = pl.program_id(2)
    q = q_ref[...]
    lane = lax.broadcasted_iota(jnp.int32, q.shape, 1)
    zero = jnp.zeros_like(q)
    qm = (jnp.where(lane < HEAD_DIM, q, zero), jnp.where(lane >= HEAD_DIM, q, zero))

    m_s[...] = jnp.full(m_s.shape, NEG, F32)
    l_s[...] = jnp.zeros(l_s.shape, F32)
    acc_s[...] = jnp.zeros(acc_s.shape, F32)

    def block(ki, _):
        ks = pl.multiple_of(ki * TK, TK)
        kb = k_ref[pl.ds(ks, TK), :]
        vb = v_ref[pl.ds(ks, TK), :]
        bias = bias_ref[jnp.minimum(qi - ki, 2)]
        for mp in range(2):
            s = _dot_nt(qm[mp], kb) + bias
            m_old = m_s[mp]
            m_new = jnp.maximum(m_old, jnp.max(s, axis=-1, keepdims=True))
            alpha = jnp.exp(m_old - m_new)
            p = jnp.exp(s - m_new)
            l_s[mp] = alpha * l_s[mp] + jnp.sum(p, axis=-1, keepdims=True)
            acc_s[mp] = alpha * acc_s[mp] + _dot(p.astype(BF16), vb)
            m_s[mp] = m_new
        return 0

    lax.fori_loop(0, qi + 1, block, 0)

    lv = lamv_ref[...]
    lam = (jnp.exp(jnp.sum(lv[0:1] * lv[1:2], axis=-1, keepdims=True))
           - jnp.exp(jnp.sum(lv[2:3] * lv[3:4], axis=-1, keepdims=True)) + lam_init)
    o = acc_s[0] / l_s[0] - lam * (acc_s[1] / l_s[1])
    o_ref[...] = (_rms(o, subln_ref[...]) * (1.0 - lam_init)).astype(BF16)


def _attention(lamv, subln, q, k, v, bias_tiles, lam_init):
    B, S, _ = q.shape
    return pl.pallas_call(
        functools.partial(_attn_kernel, lam_init=lam_init),
        grid=(B, N_HEADS, S // TQ),
        in_specs=[_resident((4, HEAD_DIM)), _resident((1, V_DIM)),
                  pl.BlockSpec((None, TQ, V_DIM), lambda b, h, i: (b, i, h)),
                  pl.BlockSpec((None, S, V_DIM), lambda b, h, i: (b, 0, h)),
                  pl.BlockSpec((None, S, V_DIM), lambda b, h, i: (b, 0, h)),
                  pl.BlockSpec((None, 3, TQ, TK), lambda b, h, i: (h, 0, 0, 0))],
        out_specs=pl.BlockSpec((None, TQ, V_DIM), lambda b, h, i: (b, i, h)),
        out_shape=jax.ShapeDtypeStruct((B, S, ATTN_WIDTH), BF16),
        scratch_shapes=[pltpu.VMEM((2, TQ, 1), F32), pltpu.VMEM((2, TQ, 1), F32),
                        pltpu.VMEM((2, TQ, V_DIM), F32)],
        compiler_params=pltpu.CompilerParams(
            dimension_semantics=("parallel", "parallel", "parallel"),
            vmem_limit_bytes=VMEM_LIMIT),
        name="diff_attention",
    )(lamv, subln, q, k, v, bias_tiles)


def _ssm_kernel(u_ref, wbr_ref, wbi_ref, ar_ref, ai_ref, cr_ref, ci_ref, dskip_ref,
                wglu_ref, out_ref, bur, bui, xr_s, xi_s):
    @pl.when(pl.program_id(0) == 0)
    def _():
        xr_s[...] = jnp.zeros(xr_s.shape, F32)
        xi_s[...] = jnp.zeros(xi_s.shape, F32)

    u = u_ref[...]
    for j in range(STATE_LANES // MXU_TILE):
        kt = (j * MXU_TILE // SSM_STATE * SSM_GROUP) // MXU_TILE
        rows = slice(kt * MXU_TILE, (kt + 1) * MXU_TILE)
        cols = slice(j * MXU_TILE, (j + 1) * MXU_TILE)
        bur[:, cols] = _dot(u[:, rows], wbr_ref[rows, cols])
        bui[:, cols] = _dot(u[:, rows], wbi_ref[rows, cols])

    for s in range(STATE_LANES // SCAN_STRIP):
        sl = slice(s * SCAN_STRIP, (s + 1) * SCAN_STRIP)
        a_r = ar_ref[:, sl]
        a_i = ai_ref[:, sl]

        def step(t, carry):
            xr, xi = carry
            r0 = pl.multiple_of(t * SUBLANES, SUBLANES)
            nxr = a_r * xr - a_i * xi + bur[pl.ds(r0, SUBLANES), sl]
            nxi = a_r * xi + a_i * xr + bui[pl.ds(r0, SUBLANES), sl]
            bur[pl.ds(r0, SUBLANES), sl] = nxr
            bui[pl.ds(r0, SUBLANES), sl] = nxi
            return nxr, nxi

        xr, xi = lax.fori_loop(0, SCAN_STEPS, step, (xr_s[:, sl], xi_s[:, sl]), unroll=4)
        xr_s[:, sl] = xr
        xi_s[:, sl] = xi

    halves = []
    k_per_half = STATE_LANES // 2
    n_per_half = SSM_WIDTH // 2
    for m in range(2):
        rows = slice(m * k_per_half, (m + 1) * k_per_half)
        cols = slice(m * n_per_half, (m + 1) * n_per_half)
        halves.append(_dot(bur[:, rows].astype(BF16), cr_ref[rows, cols])
                      + _dot(bui[:, rows].astype(BF16), ci_ref[rows, cols]))
    y = jnp.concatenate(halves, axis=-1) + dskip_ref[...] * u.astype(F32)
    s = jax.nn.gelu(y)
    s = s * jax.nn.sigmoid(_dot(s.astype(BF16), wglu_ref[...]))
    out_ref[...] = s.astype(BF16)


def _ssm(u2, wbr, wbi, ar, ai, cr, ci, dskip, wglu):
    rows = SCAN_STEPS * SUBLANES
    n_tok = u2.shape[0]
    return pl.pallas_call(
        _ssm_kernel,
        grid=(n_tok // rows,),
        in_specs=[pl.BlockSpec((rows, SSM_WIDTH), lambda c: (c, 0)),
                  _resident(wbr.shape), _resident(wbi.shape),
                  _resident(ar.shape), _resident(ai.shape),
                  _resident(cr.shape), _resident(ci.shape),
                  _resident(dskip.shape), _resident(wglu.shape)],
        out_specs=pl.BlockSpec((rows, SSM_WIDTH), lambda c: (c, 0)),
        out_shape=jax.ShapeDtypeStruct((n_tok, SSM_WIDTH), BF16),
        scratch_shapes=[pltpu.VMEM((rows, STATE_LANES), F32),
                        pltpu.VMEM((rows, STATE_LANES), F32),
                        pltpu.VMEM((SUBLANES, STATE_LANES), F32),
                        pltpu.VMEM((SUBLANES, STATE_LANES), F32)],
        compiler_params=pltpu.CompilerParams(
            dimension_semantics=("arbitrary",), vmem_limit_bytes=VMEM_LIMIT),
        name="s5_branch",
    )(u2, wbr, wbi, ar, ai, cr, ci, dskip, wglu)


def _merge_ffn_kernel(x_ref, o_ref, s_ref, g1_ref, wg_ref, wa_ref, wb_ref, wo_ref,
                      g2_ref, w1_ref, w3_ref, w2_ref, out_ref, acc_s):
    x = x_ref[...]
    h = _rms(x, g1_ref[...]).astype(BF16)
    mixed = jax.nn.sigmoid(_dot(h, wg_ref[:, 0:D_MODEL])) * _dot(o_ref[...], wa_ref[...])
    mixed += jax.nn.sigmoid(_dot(h, wg_ref[:, D_MODEL:])) * _dot(s_ref[...], wb_ref[...])
    x1 = x + _dot(mixed.astype(BF16), wo_ref[...])

    h2 = _rms(x1, g2_ref[...]).astype(BF16)
    acc_s[...] = x1

    def chunk(c, _):
        t = jax.nn.silu(_dot(h2, w1_ref[c])) * _dot(h2, w3_ref[c])
        acc_s[...] += _dot(t.astype(BF16), w2_ref[c])
        return 0

    lax.fori_loop(0, D_FF // FF_CHUNK, chunk, 0)
    out_ref[...] = acc_s[...]


def _merge_ffn(x, o, s2, g1, wg, wa, wb, wo, g2, w1, w3, w2):
    B, S, D = x.shape
    tok = lambda width: pl.BlockSpec((None, TM_PROJ, width), lambda b, i: (b, i, 0))
    return pl.pallas_call(
        _merge_ffn_kernel,
        grid=(B, S // TM_PROJ),
        in_specs=[tok(D), tok(ATTN_WIDTH),
                  pl.BlockSpec((TM_PROJ, SSM_WIDTH), lambda b, i: (i, b)),
                  _resident(g1.shape), _resident(wg.shape), _resident(wa.shape),
                  _resident(wb.shape), _resident(wo.shape), _resident(g2.shape),
                  _resident(w1.shape), _resident(w3.shape), _resident(w2.shape)],
        out_specs=tok(D),
        out_shape=jax.ShapeDtypeStruct((B, S, D), F32),
        scratch_shapes=[pltpu.VMEM((TM_PROJ, D), F32)],
        compiler_params=pltpu.CompilerParams(
            dimension_semantics=("parallel", "parallel"), vmem_limit_bytes=VMEM_LIMIT),
        name="merge_ffn",
    )(x, o, s2, g1, wg, wa, wb, wo, g2, w1, w3, w2)


def _t5_bucket(n):
    max_exact = N_BUCKETS // 2
    is_small = n < max_exact
    nf = jnp.maximum(n, 1).astype(F32)
    large = max_exact + (jnp.log(nf / max_exact) / math.log(MAX_DISTANCE / max_exact)
                         * (N_BUCKETS - max_exact)).astype(jnp.int32)
    large = jnp.minimum(large, N_BUCKETS - 1)
    return jnp.where(is_small, n, large)


def _ssm_params(lam_re, lam_im, b_re, b_im, c_re, c_im, log_step):
    step = jnp.exp(log_step.astype(F32))[:, None]
    lr = lam_re.astype(F32)
    li = lam_im.astype(F32)
    decay = jnp.exp(lr * step)
    ab_re = decay * jnp.cos(li * step)
    ab_im = decay * jnp.sin(li * step)
    nr = ab_re - 1.0
    ni = ab_im
    den = lr * lr + li * li
    f_re = (nr * lr + ni * li) / den
    f_im = (ni * lr - nr * li) / den
    br = b_re.astype(F32)
    bi = b_im.astype(F32)
    bb_re = f_re[..., None] * br - f_im[..., None] * bi
    bb_im = f_re[..., None] * bi + f_im[..., None] * br
    eye = jnp.eye(SSM_GROUPS, dtype=F32)

    def in_blockdiag(w):
        return jnp.einsum('gpc,gh->gchp', w, eye).reshape(SSM_WIDTH, STATE_LANES).astype(BF16)

    def out_blockdiag(w):
        return jnp.einsum('gcp,gh->gphc', w, eye).reshape(STATE_LANES, SSM_WIDTH).astype(BF16)

    bcast = lambda a: jnp.broadcast_to(a.reshape(1, STATE_LANES), (SUBLANES, STATE_LANES))
    return (in_blockdiag(bb_re), in_blockdiag(bb_im), bcast(ab_re), bcast(ab_im),
            out_blockdiag(c_re.astype(F32)), out_blockdiag(-c_im.astype(F32)))


def kernel(x, rel_bias, norm_mix, w_in, q_gain, k_gain, lambda_q1, lambda_k1, lambda_q2, lambda_k2, subln, w_a, lam_re, lam_im, b_re, b_im, c_re, c_im, d_skip, log_step, w_glu, w_b, w_o, norm_ffn, w1, w3, w2):
    B, S, D = x.shape
    depth = w_in.shape[0]
    assert B == SUBLANES and D == D_MODEL

    table = rel_bias.astype(F32)
    brel = (table[_t5_bucket(jnp.arange(MAX_DISTANCE))] - table[N_BUCKETS - 1][None]).T
    bias_tiles = _bias_tiles(brel)

    seg = (jnp.arange(QK_WIDTH)[:, None] // HEAD_DIM
           == jnp.arange(QK_WIDTH)[None, :] // HEAD_DIM).astype(BF16)
    n_split = 2 * QK_WIDTH + ATTN_WIDTH + SSM_WIDTH
    ff_chunks = D_FF // FF_CHUNK

    for l in range(depth):
        lam_init = 0.8 - 0.6 * math.exp(-0.3 * l)
        w_l = w_in[l].astype(BF16)
        qg = jnp.tile(q_gain[l].astype(F32), QK_WIDTH // HEAD_DIM)[None] * HEAD_DIM ** -0.5
        kg = jnp.tile(k_gain[l].astype(F32), QK_WIDTH // HEAD_DIM)[None]
        q, k, v, u = _inproj(x, norm_mix[l][None], w_l[:, :n_split], seg, qg, kg)

        lamv = jnp.stack([lambda_q1[l], lambda_k1[l], lambda_q2[l], lambda_k2[l]]).astype(F32)
        o = _attention(lamv, subln[l][None].astype(F32), q, k, v, bias_tiles, lam_init)

        wbr, wbi, ar, ai, cr, ci = _ssm_params(lam_re[l], lam_im[l], b_re[l], b_im[l],
                                               c_re[l], c_im[l], log_step[l])
        s2 = _ssm(u.reshape(S * B, SSM_WIDTH), wbr, wbi, ar, ai, cr, ci,
                  d_skip[l][None].astype(F32), w_glu[l].astype(BF16))

        x = _merge_ffn(
            x, o, s2.reshape(S, B * SSM_WIDTH), norm_mix[l][None], w_l[:, n_split:],
            w_a[l].astype(BF16), w_b[l].astype(BF16), w_o[l].astype(BF16),
            norm_ffn[l][None],
            w1[l].astype(BF16).reshape(D, ff_chunks, FF_CHUNK).transpose(1, 0, 2),
            w3[l].astype(BF16).reshape(D, ff_chunks, FF_CHUNK).transpose(1, 0, 2),
            w2[l].astype(BF16).reshape(ff_chunks, FF_CHUNK, D))
    return x
```

```python
import functools
import math

import jax
import jax.numpy as jnp
from jax import lax
from jax.experimental import pallas as pl
from jax.experimental.pallas import tpu as pltpu

D_MODEL = 1024
N_HEADS = 4
HEAD_DIM = 64
V_DIM = 2 * HEAD_DIM
QK_WIDTH = N_HEADS * 2 * HEAD_DIM
ATTN_WIDTH = N_HEADS * V_DIM
SSM_WIDTH = D_MODEL // 2
SSM_GROUP = 16
SSM_GROUPS = SSM_WIDTH // SSM_GROUP
SSM_STATE = 64
STATE_LANES = SSM_GROUPS * SSM_STATE
D_FF = 2816
N_BUCKETS = 32
MAX_DISTANCE = 128
EPS = 1e-6
NEG = -1e30

MXU_TILE = 256
SUBLANES = 8

TM_PROJ = 512
TQ = 512
LOG2E = math.log2(math.e)
SCAN_STEPS = 64
SCAN_STRIP = 512
FF_CHUNK = 256
VMEM_LIMIT = 56 * 1024 * 1024

BF16 = jnp.bfloat16
F32 = jnp.float32


def _dot(a, b):
    return jnp.dot(a, b, preferred_element_type=F32)


def _dot_nt(a, b):
    return lax.dot_general(a, b, (((1,), (1,)), ((), ())), preferred_element_type=F32)


def _rms(x, g):
    return x * lax.rsqrt(jnp.mean(x * x, axis=-1, keepdims=True) + EPS) * g


def _resident(shape):
    zeros = (0,) * len(shape)
    return pl.BlockSpec(shape, lambda *_: zeros, pipeline_mode=pl.Buffered(1))


def _bias_tiles_kernel(brel_ref, out_ref):
    h = pl.program_id(0)
    sub = MAX_DISTANCE
    r = lax.broadcasted_iota(jnp.int32, (sub, sub), 0)
    c = lax.broadcasted_iota(jnp.int32, (sub, sub), 1)

    def expand(d):
        rel = r - c + d * sub

        def body(n, tile):
            return jnp.where(rel == n, brel_ref[h, n], tile)

        tile = lax.fori_loop(0, MAX_DISTANCE, body, jnp.zeros((sub, sub), F32))
        return jnp.where(rel < 0, NEG, tile)

    by_diag = {0: expand(0), 1: expand(1)}
    for t, off in enumerate((TQ, 0)):
        for a in range(TQ // sub):
            for b in range(TQ // sub):
                d = a - b + off // sub
                fill = NEG if d < 0 else 0.0
                out_ref[t, a * sub:(a + 1) * sub, b * sub:(b + 1) * sub] = by_diag.get(
                    d, jnp.full((sub, sub), fill, F32))


def _bias_tiles(brel):
    return pl.pallas_call(
        _bias_tiles_kernel,
        grid=(N_HEADS,),
        in_specs=[pl.BlockSpec(memory_space=pltpu.SMEM)],
        out_specs=pl.BlockSpec((None, 2, TQ, TQ), lambda h: (h, 0, 0, 0)),
        out_shape=jax.ShapeDtypeStruct((N_HEADS, 2, TQ, TQ), F32),
        name="bias_tiles",
    )(brel)


def _inproj_kernel(x_ref, g_ref, w_ref, seg_ref, qg_ref, kg_ref,
                   q_ref, k_ref, v_ref, u_ref):
    h = _rms(x_ref[...], g_ref[...]).astype(BF16)

    def qk_norm(z, gain):
        ss = _dot((z * z).astype(BF16), seg_ref[...])
        return z * lax.rsqrt(ss * (1.0 / HEAD_DIM) + EPS) * gain

    zq = _dot(h, w_ref[:, 0:QK_WIDTH])
    q_ref[...] = qk_norm(zq, qg_ref[...]).astype(BF16)
    zk = _dot(h, w_ref[:, QK_WIDTH:2 * QK_WIDTH])
    k_ref[...] = qk_norm(zk, kg_ref[...]).astype(BF16)
    v_ref[...] = _dot(h, w_ref[:, 2 * QK_WIDTH:2 * QK_WIDTH + ATTN_WIDTH]).astype(BF16)
    u_ref[...] = _dot(h, w_ref[:, 2 * QK_WIDTH + ATTN_WIDTH:]).astype(BF16)


def _inproj(x, g, w, seg, qg, kg):
    B, S, D = x.shape
    W = w.shape[1]
    tok = lambda width: pl.BlockSpec((None, TM_PROJ, width), lambda b, i: (b, i, 0))
    return pl.pallas_call(
        _inproj_kernel,
        grid=(B, S // TM_PROJ),
        in_specs=[tok(D), _resident((1, D)), _resident((D, W)),
                  _resident((QK_WIDTH, QK_WIDTH)),
                  _resident((1, QK_WIDTH)), _resident((1, QK_WIDTH))],
        out_specs=[tok(QK_WIDTH), tok(QK_WIDTH), tok(ATTN_WIDTH),
                   pl.BlockSpec((TM_PROJ, SSM_WIDTH), lambda b, i: (i, b))],
        out_shape=[jax.ShapeDtypeStruct((B, S, QK_WIDTH), BF16),
                   jax.ShapeDtypeStruct((B, S, QK_WIDTH), BF16),
                   jax.ShapeDtypeStruct((B, S, ATTN_WIDTH), BF16),
                   jax.ShapeDtypeStruct((S, B * SSM_WIDTH), BF16)],
        compiler_params=pltpu.CompilerParams(
            dimension_semantics=("parallel", "parallel"), vmem_limit_bytes=VMEM_LIMIT),
        name="inproj",
    )(x, g, w, seg, qg, kg)


def _attn_kernel(lamv_ref, subln_ref, q_ref, k_ref, v_ref, bias_ref, o_ref,
                 m_s, l_s, acc_s, *, lam_init):
    qi = pl.program_id(2)
    q = q_ref[...]
    lane = lax.broadcasted_iota(jnp.int32, q.shape, 1)
    zero = jnp.zeros_like(q)
    qm = (jnp.where(lane < HEAD_DIM, q, zero), jnp.where(lane >= HEAD_DIM, q, zero))

    m_s[...] = jnp.full(m_s.shape, NEG, F32)
    l_s[...] = jnp.zeros(l_s.shape, F32)
    acc_s[...] = jnp.zeros(acc_s.shape, F32)

    def process(k0, nk, bias):
        kb = k_ref[pl.ds(k0, nk), :]
        vb = v_ref[pl.ds(k0, nk), :]
        for mp in range(2):
            s = _dot_nt(qm[mp], kb)
            if bias is not None:
                s = s + bias
            m_old = m_s[mp]
            m_new = jnp.maximum(m_old, jnp.max(s, axis=-1, keepdims=True))
            alpha = jnp.exp2(m_old - m_new)
            p = jnp.exp2(s - jnp.concatenate([m_new] * (nk // V_DIM), axis=1))
            l_s[mp] = alpha * l_s[mp] + jnp.sum(p, axis=-1, keepdims=True)
            acc_s[mp] = alpha * acc_s[mp] + _dot(p.astype(BF16), vb)
            m_s[mp] = m_new

    n_far = jnp.maximum(qi - 1, 0)

    def far_block(j, _):
        process(pl.multiple_of(j * TQ, TQ), TQ, None)
        return 0

    lax.fori_loop(0, n_far, far_block, 0)

    @pl.when(qi > 0)
    def _():
        process(pl.multiple_of((qi - 1) * TQ, TQ), TQ, bias_ref[0])

    process(pl.multiple_of(qi * TQ, TQ), TQ, bias_ref[1])

    lv = lamv_ref[...]
    lam = (jnp.exp(jnp.sum(lv[0:1] * lv[1:2], axis=-1, keepdims=True))
           - jnp.exp(jnp.sum(lv[2:3] * lv[3:4], axis=-1, keepdims=True)) + lam_init)
    o = acc_s[0] / l_s[0] - lam * (acc_s[1] / l_s[1])
    o_ref[...] = (_rms(o, subln_ref[...]) * (1.0 - lam_init)).astype(BF16)


def _attention(lamv, subln, q, k, v, bias_tiles, lam_init):
    B, S, _ = q.shape
    return pl.pallas_call(
        functools.partial(_attn_kernel, lam_init=lam_init),
        grid=(B, N_HEADS, S // TQ),
        in_specs=[_resident((4, HEAD_DIM)), _resident((1, V_DIM)),
                  pl.BlockSpec((None, TQ, V_DIM), lambda b, h, i: (b, i, h)),
                  pl.BlockSpec((None, S, V_DIM), lambda b, h, i: (b, 0, h)),
                  pl.BlockSpec((None, S, V_DIM), lambda b, h, i: (b, 0, h)),
                  pl.BlockSpec((None, 2, TQ, TQ), lambda b, h, i: (h, 0, 0, 0))],
        out_specs=pl.BlockSpec((None, TQ, V_DIM), lambda b, h, i: (b, i, h)),
        out_shape=jax.ShapeDtypeStruct((B, S, ATTN_WIDTH), BF16),
        scratch_shapes=[pltpu.VMEM((2, TQ, V_DIM), F32), pltpu.VMEM((2, TQ, V_DIM), F32),
                        pltpu.VMEM((2, TQ, V_DIM), F32)],
        compiler_params=pltpu.CompilerParams(
            dimension_semantics=("parallel", "parallel", "parallel"),
            vmem_limit_bytes=VMEM_LIMIT),
        name="diff_attention",
    )(lamv, subln, q, k, v, bias_tiles)


def _ssm_kernel(u_ref, wbr_ref, wbi_ref, ar_ref, ai_ref, cr_ref, ci_ref, dskip_ref,
                wglu_ref, out_ref, bur, bui, xr_s, xi_s):
    @pl.when(pl.program_id(0) == 0)
    def _():
        xr_s[...] = jnp.zeros(xr_s.shape, F32)
        xi_s[...] = jnp.zeros(xi_s.shape, F32)

    u = u_ref[...]
    for j in range(STATE_LANES // MXU_TILE):
        kt = (j * MXU_TILE // SSM_STATE * SSM_GROUP) // MXU_TILE
        rows = slice(kt * MXU_TILE, (kt + 1) * MXU_TILE)
        cols = slice(j * MXU_TILE, (j + 1) * MXU_TILE)
        bur[:, cols] = _dot(u[:, rows], wbr_ref[rows, cols])
        bui[:, cols] = _dot(u[:, rows], wbi_ref[rows, cols])

    for s in range(STATE_LANES // SCAN_STRIP):
        sl = slice(s * SCAN_STRIP, (s + 1) * SCAN_STRIP)
        a_r = ar_ref[:, sl]
        a_i = ai_ref[:, sl]

        def step(t, carry):
            xr, xi = carry
            r0 = pl.multiple_of(t * SUBLANES, SUBLANES)
            nxr = a_r * xr - a_i * xi + bur[pl.ds(r0, SUBLANES), sl]
            nxi = a_r * xi + a_i * xr + bui[pl.ds(r0, SUBLANES), sl]
            bur[pl.ds(r0, SUBLANES), sl] = nxr
            bui[pl.ds(r0, SUBLANES), sl] = nxi
            return nxr, nxi

        xr, xi = lax.fori_loop(0, SCAN_STEPS, step, (xr_s[:, sl], xi_s[:, sl]), unroll=4)
        xr_s[:, sl] = xr
        xi_s[:, sl] = xi

    halves = []
    k_per_half = STATE_LANES // 2
    n_per_half = SSM_WIDTH // 2
    for m in range(2):
        rows = slice(m * k_per_half, (m + 1) * k_per_half)
        cols = slice(m * n_per_half, (m + 1) * n_per_half)
        halves.append(_dot(bur[:, rows].astype(BF16), cr_ref[rows, cols])
                      + _dot(bui[:, rows].astype(BF16), ci_ref[rows, cols]))
    y = jnp.concatenate(halves, axis=-1) + dskip_ref[...] * u.astype(F32)
    s = jax.nn.gelu(y)
    s = s * jax.nn.sigmoid(_dot(s.astype(BF16), wglu_ref[...]))
    out_ref[...] = s.astype(BF16)


def _ssm(u2, wbr, wbi, ar, ai, cr, ci, dskip, wglu):
    rows = SCAN_STEPS * SUBLANES
    n_tok = u2.shape[0]
    return pl.pallas_call(
        _ssm_kernel,
        grid=(n_tok // rows,),
        in_specs=[pl.BlockSpec((rows, SSM_WIDTH), lambda c: (c, 0)),
                  _resident(wbr.shape), _resident(wbi.shape),
                  _resident(ar.shape), _resident(ai.shape),
                  _resident(cr.shape), _resident(ci.shape),
                  _resident(dskip.shape), _resident(wglu.shape)],
        out_specs=pl.BlockSpec((rows, SSM_WIDTH), lambda c: (c, 0)),
        out_shape=jax.ShapeDtypeStruct((n_tok, SSM_WIDTH), BF16),
        scratch_shapes=[pltpu.VMEM((rows, STATE_LANES), F32),
                        pltpu.VMEM((rows, STATE_LANES), F32),
                        pltpu.VMEM((SUBLANES, STATE_LANES), F32),
                        pltpu.VMEM((SUBLANES, STATE_LANES), F32)],
        compiler_params=pltpu.CompilerParams(
            dimension_semantics=("arbitrary",), vmem_limit_bytes=VMEM_LIMIT),
        name="s5_branch",
    )(u2, wbr, wbi, ar, ai, cr, ci, dskip, wglu)


def _merge_ffn_kernel(x_ref, o_ref, s_ref, g1_ref, wg_ref, wa_ref, wb_ref, wo_ref,
                      g2_ref, w1_ref, w3_ref, w2_ref, out_ref, acc_s):
    x = x_ref[...]
    h = _rms(x, g1_ref[...]).astype(BF16)
    mixed = jax.nn.sigmoid(_dot(h, wg_ref[:, 0:D_MODEL])) * _dot(o_ref[...], wa_ref[...])
    mixed += jax.nn.sigmoid(_dot(h, wg_ref[:, D_MODEL:])) * _dot(s_ref[...], wb_ref[...])
    x1 = x + _dot(mixed.astype(BF16), wo_ref[...])

    h2 = _rms(x1, g2_ref[...]).astype(BF16)
    acc_s[...] = x1

    for c in range(D_FF // FF_CHUNK):
        cols = slice(c * FF_CHUNK, (c + 1) * FF_CHUNK)
        t = jax.nn.silu(_dot(h2, w1_ref[:, cols])) * _dot(h2, w3_ref[:, cols])
        acc_s[...] += _dot(t.astype(BF16), w2_ref[cols, :])
    out_ref[...] = acc_s[...]


def _merge_ffn(x, o, s2, g1, wg, wa, wb, wo, g2, w1, w3, w2):
    B, S, D = x.shape
    tok = lambda width: pl.BlockSpec((None, TM_PROJ, width), lambda b, i: (b, i, 0))
    return pl.pallas_call(
        _merge_ffn_kernel,
        grid=(B, S // TM_PROJ),
        in_specs=[tok(D), tok(ATTN_WIDTH),
                  pl.BlockSpec((TM_PROJ, SSM_WIDTH), lambda b, i: (i, b)),
                  _resident(g1.shape), _resident(wg.shape), _resident(wa.shape),
                  _resident(wb.shape), _resident(wo.shape), _resident(g2.shape),
                  _resident(w1.shape), _resident(w3.shape), _resident(w2.shape)],
        out_specs=tok(D),
        out_shape=jax.ShapeDtypeStruct((B, S, D), F32),
        scratch_shapes=[pltpu.VMEM((TM_PROJ, D), F32)],
        compiler_params=pltpu.CompilerParams(
            dimension_semantics=("parallel", "parallel"), vmem_limit_bytes=VMEM_LIMIT),
        name="merge_ffn",
    )(x, o, s2, g1, wg, wa, wb, wo, g2, w1, w3, w2)


def _t5_bucket(n):
    max_exact = N_BUCKETS // 2
    is_small = n < max_exact
    nf = jnp.maximum(n, 1).astype(F32)
    large = max_exact + (jnp.log(nf / max_exact) / math.log(MAX_DISTANCE / max_exact)
                         * (N_BUCKETS - max_exact)).astype(jnp.int32)
    large = jnp.minimum(large, N_BUCKETS - 1)
    return jnp.where(is_small, n, large)


def _ssm_params(lam_re, lam_im, b_re, b_im, c_re, c_im, log_step):
    step = jnp.exp(log_step.astype(F32))[:, None]
    lr = lam_re.astype(F32)
    li = lam_im.astype(F32)
    decay = jnp.exp(lr * step)
    ab_re = decay * jnp.cos(li * step)
    ab_im = decay * jnp.sin(li * step)
    nr = ab_re - 1.0
    ni = ab_im
    den = lr * lr + li * li
    f_re = (nr * lr + ni * li) / den
    f_im = (ni * lr - nr * li) / den
    br = b_re.astype(F32)
    bi = b_im.astype(F32)
    bb_re = f_re[..., None] * br - f_im[..., None] * bi
    bb_im = f_re[..., None] * bi + f_im[..., None] * br
    eye = jnp.eye(SSM_GROUPS, dtype=F32)

    def in_blockdiag(w):
        return jnp.einsum('gpc,gh->gchp', w, eye).reshape(SSM_WIDTH, STATE_LANES).astype(BF16)

    def out_blockdiag(w):
        return jnp.einsum('gcp,gh->gphc', w, eye).reshape(STATE_LANES, SSM_WIDTH).astype(BF16)

    bcast = lambda a: jnp.broadcast_to(a.reshape(1, STATE_LANES), (SUBLANES, STATE_LANES))
    return (in_blockdiag(bb_re), in_blockdiag(bb_im), bcast(ab_re), bcast(ab_im),
            out_blockdiag(c_re.astype(F32)), out_blockdiag(-c_im.astype(F32)))


def kernel(x, rel_bias, norm_mix, w_in, q_gain, k_gain, lambda_q1, lambda_k1, lambda_q2, lambda_k2, subln, w_a, lam_re, lam_im, b_re, b_im, c_re, c_im, d_skip, log_step, w_glu, w_b, w_o, norm_ffn, w1, w3, w2):
    B, S, D = x.shape
    depth = w_in.shape[0]
    assert B == SUBLANES and D == D_MODEL

    table = rel_bias.astype(F32)
    brel = (table[_t5_bucket(jnp.arange(MAX_DISTANCE))] - table[N_BUCKETS - 1][None]).T * LOG2E
    bias_tiles = _bias_tiles(brel)

    seg = (jnp.arange(QK_WIDTH)[:, None] // HEAD_DIM
           == jnp.arange(QK_WIDTH)[None, :] // HEAD_DIM).astype(BF16)
    n_split = 2 * QK_WIDTH + ATTN_WIDTH + SSM_WIDTH

    for l in range(depth):
        lam_init = 0.8 - 0.6 * math.exp(-0.3 * l)
        w_l = w_in[l].astype(BF16)
        qg = jnp.tile(q_gain[l].astype(F32), QK_WIDTH // HEAD_DIM)[None] * (HEAD_DIM ** -0.5 * LOG2E)
        kg = jnp.tile(k_gain[l].astype(F32), QK_WIDTH // HEAD_DIM)[None]
        q, k, v, u = _inproj(x, norm_mix[l][None], w_l[:, :n_split], seg, qg, kg)

        lamv = jnp.stack([lambda_q1[l], lambda_k1[l], lambda_q2[l], lambda_k2[l]]).astype(F32)
        o = _attention(lamv, subln[l][None].astype(F32), q, k, v, bias_tiles, lam_init)

        wbr, wbi, ar, ai, cr, ci = _ssm_params(lam_re[l], lam_im[l], b_re[l], b_im[l],
                                               c_re[l], c_im[l], log_step[l])
        s2 = _ssm(u.reshape(S * B, SSM_WIDTH), wbr, wbi, ar, ai, cr, ci,
                  d_skip[l][None].astype(F32), w_glu[l].astype(BF16))

        x = _merge_ffn(
            x, o, s2.reshape(S, B * SSM_WIDTH), norm_mix[l][None], w_l[:, n_split:],
            w_a[l].astype(BF16), w_b[l].astype(BF16), w_o[l].astype(BF16),
            norm_ffn[l][None],
            w1[l].astype(BF16), w3[l].astype(BF16), w2[l].astype(BF16))
    return x
```

```python
import functools
import math

import jax
import jax.numpy as jnp
from jax import lax
from jax.experimental import pallas as pl
from jax.experimental.pallas import tpu as pltpu

D_MODEL = 1024
N_HEADS = 4
HEAD_DIM = 64
V_DIM = 2 * HEAD_DIM
QK_WIDTH = N_HEADS * 2 * HEAD_DIM
ATTN_WIDTH = N_HEADS * V_DIM
SSM_WIDTH = D_MODEL // 2
SSM_GROUP = 16
SSM_GROUPS = SSM_WIDTH // SSM_GROUP
SSM_STATE = 64
STATE_LANES = SSM_GROUPS * SSM_STATE
D_FF = 2816
N_BUCKETS = 32
MAX_DISTANCE = 128
EPS = 1e-6
NEG = -1e30

MXU_TILE = 256
SUBLANES = 8

TM_PROJ = 512
TQ = 512
LOG2E = math.log2(math.e)
SCAN_STEPS = 64
SCAN_STRIP = 512
FF_CHUNK = 256
VMEM_LIMIT = 56 * 1024 * 1024

BF16 = jnp.bfloat16
F32 = jnp.float32


def _dot(a, b):
    return jnp.dot(a, b, preferred_element_type=F32)


def _dot_nt(a, b):
    return lax.dot_general(a, b, (((1,), (1,)), ((), ())), preferred_element_type=F32)


def _rms(x, g):
    return x * lax.rsqrt(jnp.mean(x * x, axis=-1, keepdims=True) + EPS) * g


def _resident(shape):
    zeros = (0,) * len(shape)
    return pl.BlockSpec(shape, lambda *_: zeros, pipeline_mode=pl.Buffered(1))


def _bias_tiles_kernel(brel_ref, out_ref):
    h = pl.program_id(0)
    sub = MAX_DISTANCE
    r = lax.broadcasted_iota(jnp.int32, (sub, sub), 0)
    c = lax.broadcasted_iota(jnp.int32, (sub, sub), 1)

    def expand(d):
        rel = r - c + d * sub

        def body(n, tile):
            return jnp.where(rel == n, brel_ref[h, n], tile)

        tile = lax.fori_loop(0, MAX_DISTANCE, body, jnp.zeros((sub, sub), F32))
        return jnp.where(rel < 0, NEG, tile)

    by_diag = {0: expand(0), 1: expand(1)}
    for t, off in enumerate((TQ, 0)):
        for a in range(TQ // sub):
            for b in range(TQ // sub):
                d = a - b + off // sub
                fill = NEG if d < 0 else 0.0
                out_ref[t, a * sub:(a + 1) * sub, b * sub:(b + 1) * sub] = by_diag.get(
                    d, jnp.full((sub, sub), fill, F32))


def _bias_tiles(brel):
    return pl.pallas_call(
        _bias_tiles_kernel,
        grid=(N_HEADS,),
        in_specs=[pl.BlockSpec(memory_space=pltpu.SMEM)],
        out_specs=pl.BlockSpec((None, 2, TQ, TQ), lambda h: (h, 0, 0, 0)),
        out_shape=jax.ShapeDtypeStruct((N_HEADS, 2, TQ, TQ), F32),
        name="bias_tiles",
    )(brel)


def _inproj_kernel(x_ref, g_ref, w_ref, seg_ref, qg_ref, kg_ref,
                   q_ref, k_ref, v_ref, u_ref):
    h = _rms(x_ref[...], g_ref[...]).astype(BF16)

    def qk_norm(z, gain):
        ss = _dot((z * z).astype(BF16), seg_ref[...])
        return z * lax.rsqrt(ss * (1.0 / HEAD_DIM) + EPS) * gain

    zq = _dot(h, w_ref[:, 0:QK_WIDTH])
    q_ref[...] = qk_norm(zq, qg_ref[...]).astype(BF16)
    zk = _dot(h, w_ref[:, QK_WIDTH:2 * QK_WIDTH])
    k_ref[...] = qk_norm(zk, kg_ref[...]).astype(BF16)
    v_ref[...] = _dot(h, w_ref[:, 2 * QK_WIDTH:2 * QK_WIDTH + ATTN_WIDTH]).astype(BF16)
    u_ref[...] = _dot(h, w_ref[:, 2 * QK_WIDTH + ATTN_WIDTH:]).astype(BF16)


def _inproj(x, g, w, seg, qg, kg):
    B, S, D = x.shape
    W = w.shape[1]
    tok = lambda width: pl.BlockSpec((None, TM_PROJ, width), lambda b, i: (b, i, 0))
    return pl.pallas_call(
        _inproj_kernel,
        grid=(B, S // TM_PROJ),
        in_specs=[tok(D), _resident((1, D)), _resident((D, W)),
                  _resident((QK_WIDTH, QK_WIDTH)),
                  _resident((1, QK_WIDTH)), _resident((1, QK_WIDTH))],
        out_specs=[tok(QK_WIDTH), tok(QK_WIDTH), tok(ATTN_WIDTH),
                   pl.BlockSpec((TM_PROJ, SSM_WIDTH), lambda b, i: (i, b))],
        out_shape=[jax.ShapeDtypeStruct((B, S, QK_WIDTH), BF16),
                   jax.ShapeDtypeStruct((B, S, QK_WIDTH), BF16),
                   jax.ShapeDtypeStruct((B, S, ATTN_WIDTH), BF16),
                   jax.ShapeDtypeStruct((S, B * SSM_WIDTH), BF16)],
        compiler_params=pltpu.CompilerParams(
            dimension_semantics=("parallel", "parallel"), vmem_limit_bytes=VMEM_LIMIT),
        name="inproj",
    )(x, g, w, seg, qg, kg)


def _attn_kernel(lamv_ref, subln_ref, q_ref, k_ref, v_ref, bias_ref, o_ref,
                 *, lam_init):
    qi = pl.program_id(2)
    q = q_ref[...]
    lane = lax.broadcasted_iota(jnp.int32, q.shape, 1)
    zero = jnp.zeros_like(q)
    qm = (jnp.where(lane < HEAD_DIM, q, zero), jnp.where(lane >= HEAD_DIM, q, zero))
    ones = jnp.ones((TQ, V_DIM), BF16)
    lv = lamv_ref[...]
    lam = (jnp.exp(jnp.sum(lv[0:1] * lv[1:2], axis=-1, keepdims=True))
           - jnp.exp(jnp.sum(lv[2:3] * lv[3:4], axis=-1, keepdims=True)) + lam_init)

    def attend(n_tiles):
        m = [jnp.full((TQ, V_DIM), NEG, F32)] * 2
        acc = [jnp.zeros((TQ, 2 * V_DIM), F32)] * 2
        for j in range(n_tiles):
            rows = slice(j * TQ, (j + 1) * TQ)
            kb = k_ref[rows, :]
            v1 = jnp.concatenate([v_ref[rows, :], ones], axis=1)
            bias = {n_tiles - 1: 1, n_tiles - 2: 0}.get(j)
            for mp in range(2):
                s = _dot_nt(qm[mp], kb)
                if bias is not None:
                    s = s + bias_ref[bias]
                m_new = jnp.maximum(m[mp], jnp.max(s, axis=-1, keepdims=True))
                alpha = jnp.exp2(m[mp] - m_new)
                p = jnp.exp2(s - jnp.concatenate([m_new] * (TQ // V_DIM), axis=1))
                acc[mp] = (jnp.concatenate([alpha, alpha], axis=1) * acc[mp]
                           + _dot(p.astype(BF16), v1))
                m[mp] = m_new
        o = (acc[0][:, :V_DIM] / acc[0][:, V_DIM:]
             - lam * (acc[1][:, :V_DIM] / acc[1][:, V_DIM:]))
        o_ref[...] = (_rms(o, subln_ref[...]) * (1.0 - lam_init)).astype(BF16)

    for n_tiles in range(1, k_ref.shape[0] // TQ + 1):
        pl.when(qi == n_tiles - 1)(functools.partial(attend, n_tiles))


def _attention(lamv, subln, q, k, v, bias_tiles, lam_init):
    B, S, _ = q.shape
    return pl.pallas_call(
        functools.partial(_attn_kernel, lam_init=lam_init),
        grid=(B, N_HEADS, S // TQ),
        in_specs=[_resident((4, HEAD_DIM)), _resident((1, V_DIM)),
                  pl.BlockSpec((None, TQ, V_DIM), lambda b, h, i: (b, i, h)),
                  pl.BlockSpec((None, S, V_DIM), lambda b, h, i: (b, 0, h)),
                  pl.BlockSpec((None, S, V_DIM), lambda b, h, i: (b, 0, h)),
                  pl.BlockSpec((None, 2, TQ, TQ), lambda b, h, i: (h, 0, 0, 0))],
        out_specs=pl.BlockSpec((None, TQ, V_DIM), lambda b, h, i: (b, i, h)),
        out_shape=jax.ShapeDtypeStruct((B, S, ATTN_WIDTH), BF16),
        compiler_params=pltpu.CompilerParams(
            dimension_semantics=("parallel", "parallel", "parallel"),
            vmem_limit_bytes=VMEM_LIMIT),
        name="diff_attention",
    )(lamv, subln, q, k, v, bias_tiles)


def _ssm_kernel(u_ref, wbr_ref, wbi_ref, ar_ref, ai_ref, cr_ref, ci_ref, dskip_ref,
                wglu_ref, out_ref, bur, bui, xr_s, xi_s):
    @pl.when(pl.program_id(0) == 0)
    def _():
        xr_s[...] = jnp.zeros(xr_s.shape, F32)
        xi_s[...] = jnp.zeros(xi_s.shape, F32)

    u = u_ref[...]
    for j in range(STATE_LANES // MXU_TILE):
        kt = (j * MXU_TILE // SSM_STATE * SSM_GROUP) // MXU_TILE
        rows = slice(kt * MXU_TILE, (kt + 1) * MXU_TILE)
        cols = slice(j * MXU_TILE, (j + 1) * MXU_TILE)
        bur[:, cols] = _dot(u[:, rows], wbr_ref[rows, cols])
        bui[:, cols] = _dot(u[:, rows], wbi_ref[rows, cols])

    for s in range(STATE_LANES // SCAN_STRIP):
        sl = slice(s * SCAN_STRIP, (s + 1) * SCAN_STRIP)
        a_r = ar_ref[:, sl]
        a_i = ai_ref[:, sl]

        def step(t, carry):
            xr, xi = carry
            r0 = pl.multiple_of(t * SUBLANES, SUBLANES)
            nxr = a_r * xr - a_i * xi + bur[pl.ds(r0, SUBLANES), sl]
            nxi = a_r * xi + a_i * xr + bui[pl.ds(r0, SUBLANES), sl]
            bur[pl.ds(r0, SUBLANES), sl] = nxr
            bui[pl.ds(r0, SUBLANES), sl] = nxi
            return nxr, nxi

        xr, xi = lax.fori_loop(0, SCAN_STEPS, step, (xr_s[:, sl], xi_s[:, sl]), unroll=4)
        xr_s[:, sl] = xr
        xi_s[:, sl] = xi

    halves = []
    k_per_half = STATE_LANES // 2
    n_per_half = SSM_WIDTH // 2
    for m in range(2):
        rows = slice(m * k_per_half, (m + 1) * k_per_half)
        cols = slice(m * n_per_half, (m + 1) * n_per_half)
        halves.append(_dot(bur[:, rows].astype(BF16), cr_ref[rows, cols])
                      + _dot(bui[:, rows].astype(BF16), ci_ref[rows, cols]))
    y = jnp.concatenate(halves, axis=-1) + dskip_ref[...] * u.astype(F32)
    s = jax.nn.gelu(y)
    s = s * jax.nn.sigmoid(_dot(s.astype(BF16), wglu_ref[...]))
    out_ref[...] = s.astype(BF16)


def _ssm(u2, wbr, wbi, ar, ai, cr, ci, dskip, wglu):
    rows = SCAN_STEPS * SUBLANES
    n_tok = u2.shape[0]
    return pl.pallas_call(
        _ssm_kernel,
        grid=(n_tok // rows,),
        in_specs=[pl.BlockSpec((rows, SSM_WIDTH), lambda c: (c, 0)),
                  _resident(wbr.shape), _resident(wbi.shape),
                  _resident(ar.shape), _resident(ai.shape),
                  _resident(cr.shape), _resident(ci.shape),
                  _resident(dskip.shape), _resident(wglu.shape)],
        out_specs=pl.BlockSpec((rows, SSM_WIDTH), lambda c: (c, 0)),
        out_shape=jax.ShapeDtypeStruct((n_tok, SSM_WIDTH), BF16),
        scratch_shapes=[pltpu.VMEM((rows, STATE_LANES), F32),
                        pltpu.VMEM((rows, STATE_LANES), F32),
                        pltpu.VMEM((SUBLANES, STATE_LANES), F32),
                        pltpu.VMEM((SUBLANES, STATE_LANES), F32)],
        compiler_params=pltpu.CompilerParams(
            dimension_semantics=("arbitrary",), vmem_limit_bytes=VMEM_LIMIT),
        name="s5_branch",
    )(u2, wbr, wbi, ar, ai, cr, ci, dskip, wglu)


def _merge_ffn_kernel(x_ref, o_ref, s_ref, g1_ref, wg_ref, wa_ref, wb_ref, wo_ref,
                      g2_ref, w1_ref, w3_ref, w2_ref, out_ref, acc_s):
    x = x_ref[...]
    h = _rms(x, g1_ref[...]).astype(BF16)
    mixed = jax.nn.sigmoid(_dot(h, wg_ref[:, 0:D_MODEL])) * _dot(o_ref[...], wa_ref[...])
    mixed += jax.nn.sigmoid(_dot(h, wg_ref[:, D_MODEL:])) * _dot(s_ref[...], wb_ref[...])
    x1 = x + _dot(mixed.astype(BF16), wo_ref[...])

    h2 = _rms(x1, g2_ref[...]).astype(BF16)
    acc_s[...] = x1

    for c in range(D_FF // FF_CHUNK):
        cols = slice(c * FF_CHUNK, (c + 1) * FF_CHUNK)
        t = jax.nn.silu(_dot(h2, w1_ref[:, cols])) * _dot(h2, w3_ref[:, cols])
        acc_s[...] += _dot(t.astype(BF16), w2_ref[cols, :])
    out_ref[...] = acc_s[...]


def _merge_ffn(x, o, s2, g1, wg, wa, wb, wo, g2, w1, w3, w2):
    B, S, D = x.shape
    tok = lambda width: pl.BlockSpec((None, TM_PROJ, width), lambda b, i: (b, i, 0))
    return pl.pallas_call(
        _merge_ffn_kernel,
        grid=(B, S // TM_PROJ),
        in_specs=[tok(D), tok(ATTN_WIDTH),
                  pl.BlockSpec((TM_PROJ, SSM_WIDTH), lambda b, i: (i, b)),
                  _resident(g1.shape), _resident(wg.shape), _resident(wa.shape),
                  _resident(wb.shape), _resident(wo.shape), _resident(g2.shape),
                  _resident(w1.shape), _resident(w3.shape), _resident(w2.shape)],
        out_specs=tok(D),
        out_shape=jax.ShapeDtypeStruct((B, S, D), F32),
        scratch_shapes=[pltpu.VMEM((TM_PROJ, D), F32)],
        compiler_params=pltpu.CompilerParams(
            dimension_semantics=("parallel", "parallel"), vmem_limit_bytes=VMEM_LIMIT),
        name="merge_ffn",
    )(x, o, s2, g1, wg, wa, wb, wo, g2, w1, w3, w2)


def _t5_bucket(n):
    max_exact = N_BUCKETS // 2
    is_small = n < max_exact
    nf = jnp.maximum(n, 1).astype(F32)
    large = max_exact + (jnp.log(nf / max_exact) / math.log(MAX_DISTANCE / max_exact)
                         * (N_BUCKETS - max_exact)).astype(jnp.int32)
    large = jnp.minimum(large, N_BUCKETS - 1)
    return jnp.where(is_small, n, large)


def _ssm_params(lam_re, lam_im, b_re, b_im, c_re, c_im, log_step):
    step = jnp.exp(log_step.astype(F32))[:, None]
    lr = lam_re.astype(F32)
    li = lam_im.astype(F32)
    decay = jnp.exp(lr * step)
    ab_re = decay * jnp.cos(li * step)
    ab_im = decay * jnp.sin(li * step)
    nr = ab_re - 1.0
    ni = ab_im
    den = lr * lr + li * li
    f_re = (nr * lr + ni * li) / den
    f_im = (ni * lr - nr * li) / den
    br = b_re.astype(F32)
    bi = b_im.astype(F32)
    bb_re = f_re[..., None] * br - f_im[..., None] * bi
    bb_im = f_re[..., None] * bi + f_im[..., None] * br

    def blockdiag(w):
        g, n, m = w.shape
        stacked = w.transpose(0, 2, 1).reshape(g * m, n)
        same = (jnp.arange(g * m)[:, None] // m) == (jnp.arange(g * n)[None, :] // n)
        return jnp.where(same, jnp.tile(stacked, (1, g)), 0.0).astype(BF16)

    in_blockdiag = blockdiag
    out_blockdiag = blockdiag

    bcast = lambda a: jnp.broadcast_to(a.reshape(1, STATE_LANES), (SUBLANES, STATE_LANES))
    return (in_blockdiag(bb_re), in_blockdiag(bb_im), bcast(ab_re), bcast(ab_im),
            out_blockdiag(c_re.astype(F32)), out_blockdiag(-c_im.astype(F32)))


def kernel(x, rel_bias, norm_mix, w_in, q_gain, k_gain, lambda_q1, lambda_k1, lambda_q2, lambda_k2, subln, w_a, lam_re, lam_im, b_re, b_im, c_re, c_im, d_skip, log_step, w_glu, w_b, w_o, norm_ffn, w1, w3, w2):
    B, S, D = x.shape
    depth = w_in.shape[0]
    assert B == SUBLANES and D == D_MODEL

    table = rel_bias.astype(F32)
    brel = (table[_t5_bucket(jnp.arange(MAX_DISTANCE))] - table[N_BUCKETS - 1][None]).T * LOG2E
    bias_tiles = _bias_tiles(brel)

    seg = (jnp.arange(QK_WIDTH)[:, None] // HEAD_DIM
           == jnp.arange(QK_WIDTH)[None, :] // HEAD_DIM).astype(BF16)
    n_split = 2 * QK_WIDTH + ATTN_WIDTH + SSM_WIDTH

    for l in range(depth):
        lam_init = 0.8 - 0.6 * math.exp(-0.3 * l)
        w_l = w_in[l].astype(BF16)
        qg = jnp.tile(q_gain[l].astype(F32), QK_WIDTH // HEAD_DIM)[None] * (HEAD_DIM ** -0.5 * LOG2E)
        kg = jnp.tile(k_gain[l].astype(F32), QK_WIDTH // HEAD_DIM)[None]
        q, k, v, u = _inproj(x, norm_mix[l][None], w_l[:, :n_split], seg, qg, kg)

        lamv = jnp.stack([lambda_q1[l], lambda_k1[l], lambda_q2[l], lambda_k2[l]]).astype(F32)
        o = _attention(lamv, subln[l][None].astype(F32), q, k, v, bias_tiles, lam_init)

        wbr, wbi, ar, ai, cr, ci = _ssm_params(lam_re[l], lam_im[l], b_re[l], b_im[l],
                                               c_re[l], c_im[l], log_step[l])
        s2 = _ssm(u.reshape(S * B, SSM_WIDTH), wbr, wbi, ar, ai, cr, ci,
                  d_skip[l][None].astype(F32), w_glu[l].astype(BF16))

        x = _merge_ffn(
            x, o, s2.reshape(S, B * SSM_WIDTH), norm_mix[l][None], w_l[:, n_split:],
            w_a[l].astype(BF16), w_b[l].astype(BF16), w_o[l].astype(BF16),
            norm_ffn[l][None],
            w1[l].astype(BF16), w3[l].astype(BF16), w2[l].astype(BF16))
    return x
```

```python
import functools
import math

import jax
import jax.numpy as jnp
from jax import lax
from jax.experimental import pallas as pl
from jax.experimental.pallas import tpu as pltpu

D_MODEL = 1024
N_HEADS = 4
HEAD_DIM = 64
V_DIM = 2 * HEAD_DIM
QK_WIDTH = N_HEADS * 2 * HEAD_DIM
ATTN_WIDTH = N_HEADS * V_DIM
SSM_WIDTH = D_MODEL // 2
SSM_GROUP = 16
SSM_GROUPS = SSM_WIDTH // SSM_GROUP
SSM_STATE = 64
STATE_LANES = SSM_GROUPS * SSM_STATE
D_FF = 2816
N_BUCKETS = 32
MAX_DISTANCE = 128
EPS = 1e-6
NEG = -1e30

MXU_TILE = 256
SUBLANES = 8

TM_PROJ = 512
TQ = 512
LOG2E = math.log2(math.e)
SCAN_STEPS = 64
SCAN_STRIP = 512
FF_CHUNK = 256
VMEM_LIMIT = 56 * 1024 * 1024

BF16 = jnp.bfloat16
F32 = jnp.float32


def _dot(a, b):
    return jnp.dot(a, b, preferred_element_type=F32)


def _dot_nt(a, b):
    return lax.dot_general(a, b, (((1,), (1,)), ((), ())), preferred_element_type=F32)


def _rms(x, g):
    return x * lax.rsqrt(jnp.mean(x * x, axis=-1, keepdims=True) + EPS) * g


def _resident(shape):
    zeros = (0,) * len(shape)
    return pl.BlockSpec(shape, lambda *_: zeros, pipeline_mode=pl.Buffered(1))


def _bias_tiles_kernel(brel_ref, out_ref):
    h = pl.program_id(0)
    sub = MAX_DISTANCE
    r = lax.broadcasted_iota(jnp.int32, (sub, sub), 0)
    c = lax.broadcasted_iota(jnp.int32, (sub, sub), 1)

    def expand(d):
        rel = r - c + d * sub

        def body(n, tile):
            return jnp.where(rel == n, brel_ref[h, n], tile)

        tile = lax.fori_loop(0, MAX_DISTANCE, body, jnp.zeros((sub, sub), F32))
        return jnp.where(rel < 0, NEG, tile)

    by_diag = {0: expand(0), 1: expand(1)}
    for t, off in enumerate((TQ, 0)):
        for a in range(TQ // sub):
            for b in range(TQ // sub):
                d = a - b + off // sub
                fill = NEG if d < 0 else 0.0
                out_ref[t, a * sub:(a + 1) * sub, b * sub:(b + 1) * sub] = by_diag.get(
                    d, jnp.full((sub, sub), fill, F32))


def _bias_tiles(brel):
    return pl.pallas_call(
        _bias_tiles_kernel,
        grid=(N_HEADS,),
        in_specs=[pl.BlockSpec(memory_space=pltpu.SMEM)],
        out_specs=pl.BlockSpec((None, 2, TQ, TQ), lambda h: (h, 0, 0, 0)),
        out_shape=jax.ShapeDtypeStruct((N_HEADS, 2, TQ, TQ), F32),
        name="bias_tiles",
    )(brel)


def _inproj_kernel(x_ref, g_ref, w_ref, seg_ref, qg_ref, kg_ref,
                   q_ref, k_ref, v_ref, u_ref):
    h = _rms(x_ref[...], g_ref[...]).astype(BF16)

    def qk_norm(z, gain):
        ss = _dot((z * z).astype(BF16), seg_ref[...])
        return z * lax.rsqrt(ss * (1.0 / HEAD_DIM) + EPS) * gain

    zq = _dot(h, w_ref[:, 0:QK_WIDTH])
    q_ref[...] = qk_norm(zq, qg_ref[...]).astype(BF16)
    zk = _dot(h, w_ref[:, QK_WIDTH:2 * QK_WIDTH])
    k_ref[...] = qk_norm(zk, kg_ref[...]).astype(BF16)
    v_ref[...] = _dot(h, w_ref[:, 2 * QK_WIDTH:2 * QK_WIDTH + ATTN_WIDTH]).astype(BF16)
    u_ref[...] = _dot(h, w_ref[:, 2 * QK_WIDTH + ATTN_WIDTH:]).astype(BF16)


def _inproj(x, g, w, seg, qg, kg):
    B, S, D = x.shape
    W = w.shape[1]
    tok = lambda width: pl.BlockSpec((None, TM_PROJ, width), lambda b, i: (b, i, 0))
    return pl.pallas_call(
        _inproj_kernel,
        grid=(B, S // TM_PROJ),
        in_specs=[tok(D), _resident((1, D)), _resident((D, W)),
                  _resident((QK_WIDTH, QK_WIDTH)),
                  _resident((1, QK_WIDTH)), _resident((1, QK_WIDTH))],
        out_specs=[tok(QK_WIDTH), tok(QK_WIDTH), tok(ATTN_WIDTH), tok(SSM_WIDTH)],
        out_shape=[jax.ShapeDtypeStruct((B, S, QK_WIDTH), BF16),
                   jax.ShapeDtypeStruct((B, S, QK_WIDTH), BF16),
                   jax.ShapeDtypeStruct((B, S, ATTN_WIDTH), BF16),
                   jax.ShapeDtypeStruct((B, S, SSM_WIDTH), BF16)],
        compiler_params=pltpu.CompilerParams(
            dimension_semantics=("parallel", "parallel"), vmem_limit_bytes=VMEM_LIMIT),
        name="inproj",
    )(x, g, w, seg, qg, kg)


def _attn_kernel(lamv_ref, subln_ref, q_ref, k_ref, v_ref, bias_ref, o_ref,
                 *, lam_init):
    qi = pl.program_id(2)
    q = q_ref[...]
    lane = lax.broadcasted_iota(jnp.int32, q.shape, 1)
    zero = jnp.zeros_like(q)
    qm = (jnp.where(lane < HEAD_DIM, q, zero), jnp.where(lane >= HEAD_DIM, q, zero))
    ones = jnp.ones((TQ, V_DIM), BF16)
    lv = lamv_ref[...]
    lam = (jnp.exp(jnp.sum(lv[0:1] * lv[1:2], axis=-1, keepdims=True))
           - jnp.exp(jnp.sum(lv[2:3] * lv[3:4], axis=-1, keepdims=True)) + lam_init)

    def attend(n_tiles):
        m = [jnp.full((TQ, V_DIM), NEG, F32)] * 2
        acc = [jnp.zeros((TQ, 2 * V_DIM), F32)] * 2
        for j in range(n_tiles):
            rows = slice(j * TQ, (j + 1) * TQ)
            kb = k_ref[rows, :]
            v1 = jnp.concatenate([v_ref[rows, :], ones], axis=1)
            bias = {n_tiles - 1: 1, n_tiles - 2: 0}.get(j)
            for mp in range(2):
                s = _dot_nt(qm[mp], kb)
                if bias is not None:
                    s = s + bias_ref[bias]
                m_new = jnp.maximum(m[mp], jnp.max(s, axis=-1, keepdims=True))
                alpha = jnp.exp2(m[mp] - m_new)
                p = jnp.exp2(s - jnp.concatenate([m_new] * (TQ // V_DIM), axis=1))
                acc[mp] = (jnp.concatenate([alpha, alpha], axis=1) * acc[mp]
                           + _dot(p.astype(BF16), v1))
                m[mp] = m_new
        o = (acc[0][:, :V_DIM] / acc[0][:, V_DIM:]
             - lam * (acc[1][:, :V_DIM] / acc[1][:, V_DIM:]))
        o_ref[...] = (_rms(o, subln_ref[...]) * (1.0 - lam_init)).astype(BF16)

    for n_tiles in range(1, k_ref.shape[0] // TQ + 1):
        pl.when(qi == n_tiles - 1)(functools.partial(attend, n_tiles))


def _attention(lamv, subln, q, k, v, bias_tiles, lam_init):
    B, S, _ = q.shape
    return pl.pallas_call(
        functools.partial(_attn_kernel, lam_init=lam_init),
        grid=(B, N_HEADS, S // TQ),
        in_specs=[_resident((4, HEAD_DIM)), _resident((1, V_DIM)),
                  pl.BlockSpec((None, TQ, V_DIM), lambda b, h, i: (b, i, h)),
                  pl.BlockSpec((None, S, V_DIM), lambda b, h, i: (b, 0, h)),
                  pl.BlockSpec((None, S, V_DIM), lambda b, h, i: (b, 0, h)),
                  pl.BlockSpec((None, 2, TQ, TQ), lambda b, h, i: (h, 0, 0, 0))],
        out_specs=pl.BlockSpec((None, TQ, V_DIM), lambda b, h, i: (b, i, h)),
        out_shape=jax.ShapeDtypeStruct((B, S, ATTN_WIDTH), BF16),
        compiler_params=pltpu.CompilerParams(
            dimension_semantics=("parallel", "parallel", "parallel"),
            vmem_limit_bytes=VMEM_LIMIT),
        name="diff_attention",
    )(lamv, subln, q, k, v, bias_tiles)


def _ssm_kernel(u_ref, perm_ref, permt_ref, wbr_ref, wbi_ref, ar_ref, ai_ref, cr_ref, ci_ref, dskip_ref,
                wglu_ref, out_ref, bur, bui, xr_s, xi_s):
    @pl.when(pl.program_id(0) == 0)
    def _():
        xr_s[...] = jnp.zeros(xr_s.shape, F32)
        xi_s[...] = jnp.zeros(xi_s.shape, F32)

    u = _dot(perm_ref[...], u_ref[...].reshape(SCAN_STEPS * SUBLANES, SSM_WIDTH))
    ub = u.astype(BF16)
    for j in range(STATE_LANES // MXU_TILE):
        kt = (j * MXU_TILE // SSM_STATE * SSM_GROUP) // MXU_TILE
        rows = slice(kt * MXU_TILE, (kt + 1) * MXU_TILE)
        cols = slice(j * MXU_TILE, (j + 1) * MXU_TILE)
        bur[:, cols] = _dot(ub[:, rows], wbr_ref[rows, cols])
        bui[:, cols] = _dot(ub[:, rows], wbi_ref[rows, cols])

    for s in range(STATE_LANES // SCAN_STRIP):
        sl = slice(s * SCAN_STRIP, (s + 1) * SCAN_STRIP)
        a_r = ar_ref[:, sl]
        a_i = ai_ref[:, sl]

        def step(t, carry):
            xr, xi = carry
            r0 = pl.multiple_of(t * SUBLANES, SUBLANES)
            nxr = a_r * xr - a_i * xi + bur[pl.ds(r0, SUBLANES), sl]
            nxi = a_r * xi + a_i * xr + bui[pl.ds(r0, SUBLANES), sl]
            bur[pl.ds(r0, SUBLANES), sl] = nxr
            bui[pl.ds(r0, SUBLANES), sl] = nxi
            return nxr, nxi

        xr, xi = lax.fori_loop(0, SCAN_STEPS, step, (xr_s[:, sl], xi_s[:, sl]), unroll=4)
        xr_s[:, sl] = xr
        xi_s[:, sl] = xi

    halves = []
    k_per_half = STATE_LANES // 2
    n_per_half = SSM_WIDTH // 2
    for m in range(2):
        rows = slice(m * k_per_half, (m + 1) * k_per_half)
        cols = slice(m * n_per_half, (m + 1) * n_per_half)
        halves.append(_dot(bur[:, rows].astype(BF16), cr_ref[rows, cols])
                      + _dot(bui[:, rows].astype(BF16), ci_ref[rows, cols]))
    y = jnp.concatenate(halves, axis=-1) + dskip_ref[...] * u
    s = jax.nn.gelu(y)
    s = s * jax.nn.sigmoid(_dot(s.astype(BF16), wglu_ref[...]))
    out_ref[...] = _dot(permt_ref[...], s.astype(BF16)).astype(BF16).reshape(out_ref.shape)


def _ssm(u, wbr, wbi, ar, ai, cr, ci, dskip, wglu):
    rows = SCAN_STEPS * SUBLANES
    B, S, _ = u.shape
    r = jnp.arange(rows)
    perm = (jnp.arange(rows)[None, :] == ((r % B) * SCAN_STEPS + r // B)[:, None]).astype(BF16)
    blk = pl.BlockSpec((B, SCAN_STEPS, SSM_WIDTH), lambda c: (0, c, 0))
    return pl.pallas_call(
        _ssm_kernel,
        grid=(S // SCAN_STEPS,),
        in_specs=[blk, _resident(perm.shape), _resident(perm.shape),
                  _resident(wbr.shape), _resident(wbi.shape),
                  _resident(ar.shape), _resident(ai.shape),
                  _resident(cr.shape), _resident(ci.shape),
                  _resident(dskip.shape), _resident(wglu.shape)],
        out_specs=blk,
        out_shape=jax.ShapeDtypeStruct((B, S, SSM_WIDTH), BF16),
        scratch_shapes=[pltpu.VMEM((rows, STATE_LANES), F32),
                        pltpu.VMEM((rows, STATE_LANES), F32),
                        pltpu.VMEM((SUBLANES, STATE_LANES), F32),
                        pltpu.VMEM((SUBLANES, STATE_LANES), F32)],
        compiler_params=pltpu.CompilerParams(
            dimension_semantics=("arbitrary",), vmem_limit_bytes=VMEM_LIMIT),
        name="s5_branch",
    )(u, perm, perm.T, wbr, wbi, ar, ai, cr, ci, dskip, wglu)


def _merge_ffn_kernel(x_ref, o_ref, s_ref, g1_ref, wg_ref, wa_ref, wb_ref, wo_ref,
                      g2_ref, w1_ref, w3_ref, w2_ref, out_ref, acc_s):
    x = x_ref[...]
    h = _rms(x, g1_ref[...]).astype(BF16)
    mixed = jax.nn.sigmoid(_dot(h, wg_ref[:, 0:D_MODEL])) * _dot(o_ref[...], wa_ref[...])
    mixed += jax.nn.sigmoid(_dot(h, wg_ref[:, D_MODEL:])) * _dot(s_ref[...], wb_ref[...])
    x1 = x + _dot(mixed.astype(BF16), wo_ref[...])

    h2 = _rms(x1, g2_ref[...]).astype(BF16)
    acc_s[...] = x1

    for c in range(D_FF // FF_CHUNK):
        cols = slice(c * FF_CHUNK, (c + 1) * FF_CHUNK)
        t = jax.nn.silu(_dot(h2, w1_ref[:, cols])) * _dot(h2, w3_ref[:, cols])
        acc_s[...] += _dot(t.astype(BF16), w2_ref[cols, :])
    out_ref[...] = acc_s[...]


def _merge_ffn(x, o, s2, g1, wg, wa, wb, wo, g2, w1, w3, w2):
    B, S, D = x.shape
    tok = lambda width: pl.BlockSpec((None, TM_PROJ, width), lambda b, i: (b, i, 0))
    return pl.pallas_call(
        _merge_ffn_kernel,
        grid=(B, S // TM_PROJ),
        in_specs=[tok(D), tok(ATTN_WIDTH),
                  tok(SSM_WIDTH),
                  _resident(g1.shape), _resident(wg.shape), _resident(wa.shape),
                  _resident(wb.shape), _resident(wo.shape), _resident(g2.shape),
                  _resident(w1.shape), _resident(w3.shape), _resident(w2.shape)],
        out_specs=tok(D),
        out_shape=jax.ShapeDtypeStruct((B, S, D), F32),
        scratch_shapes=[pltpu.VMEM((TM_PROJ, D), F32)],
        compiler_params=pltpu.CompilerParams(
            dimension_semantics=("parallel", "parallel"), vmem_limit_bytes=VMEM_LIMIT),
        name="merge_ffn",
    )(x, o, s2, g1, wg, wa, wb, wo, g2, w1, w3, w2)


def _t5_bucket(n):
    max_exact = N_BUCKETS // 2
    is_small = n < max_exact
    nf = jnp.maximum(n, 1).astype(F32)
    large = max_exact + (jnp.log(nf / max_exact) / math.log(MAX_DISTANCE / max_exact)
                         * (N_BUCKETS - max_exact)).astype(jnp.int32)
    large = jnp.minimum(large, N_BUCKETS - 1)
    return jnp.where(is_small, n, large)


def _ssm_params(lam_re, lam_im, b_re, b_im, c_re, c_im, log_step):
    step = jnp.exp(log_step.astype(F32))[:, None]
    lr = lam_re.astype(F32)
    li = lam_im.astype(F32)
    decay = jnp.exp(lr * step)
    ab_re = decay * jnp.cos(li * step)
    ab_im = decay * jnp.sin(li * step)
    nr = ab_re - 1.0
    ni = ab_im
    den = lr * lr + li * li
    f_re = (nr * lr + ni * li) / den
    f_im = (ni * lr - nr * li) / den
    br = b_re.astype(F32)
    bi = b_im.astype(F32)
    bb_re = f_re[..., None] * br - f_im[..., None] * bi
    bb_im = f_re[..., None] * bi + f_im[..., None] * br

    def blockdiag(w):
        g, n, m = w.shape
        stacked = w.transpose(0, 2, 1).reshape(g * m, n)
        same = (jnp.arange(g * m)[:, None] // m) == (jnp.arange(g * n)[None, :] // n)
        return jnp.where(same, jnp.tile(stacked, (1, g)), 0.0).astype(BF16)

    in_blockdiag = blockdiag
    out_blockdiag = blockdiag

    bcast = lambda a: jnp.broadcast_to(a.reshape(1, STATE_LANES), (SUBLANES, STATE_LANES))
    return (in_blockdiag(bb_re), in_blockdiag(bb_im), bcast(ab_re), bcast(ab_im),
            out_blockdiag(c_re.astype(F32)), out_blockdiag(-c_im.astype(F32)))


def kernel(x, rel_bias, norm_mix, w_in, q_gain, k_gain, lambda_q1, lambda_k1, lambda_q2, lambda_k2, subln, w_a, lam_re, lam_im, b_re, b_im, c_re, c_im, d_skip, log_step, w_glu, w_b, w_o, norm_ffn, w1, w3, w2):
    B, S, D = x.shape
    depth = w_in.shape[0]
    assert B == SUBLANES and D == D_MODEL

    table = rel_bias.astype(F32)
    brel = (table[_t5_bucket(jnp.arange(MAX_DISTANCE))] - table[N_BUCKETS - 1][None]).T * LOG2E
    bias_tiles = _bias_tiles(brel)

    seg = (jnp.arange(QK_WIDTH)[:, None] // HEAD_DIM
           == jnp.arange(QK_WIDTH)[None, :] // HEAD_DIM).astype(BF16)
    n_split = 2 * QK_WIDTH + ATTN_WIDTH + SSM_WIDTH

    for l in range(depth):
        lam_init = 0.8 - 0.6 * math.exp(-0.3 * l)
        w_l = w_in[l].astype(BF16)
        qg = jnp.tile(q_gain[l].astype(F32), QK_WIDTH // HEAD_DIM)[None] * (HEAD_DIM ** -0.5 * LOG2E)
        kg = jnp.tile(k_gain[l].astype(F32), QK_WIDTH // HEAD_DIM)[None]
        q, k, v, u = _inproj(x, norm_mix[l][None], w_l[:, :n_split], seg, qg, kg)

        lamv = jnp.stack([lambda_q1[l], lambda_k1[l], lambda_q2[l], lambda_k2[l]]).astype(F32)
        o = _attention(lamv, subln[l][None].astype(F32), q, k, v, bias_tiles, lam_init)

        wbr, wbi, ar, ai, cr, ci = _ssm_params(lam_re[l], lam_im[l], b_re[l], b_im[l],
                                               c_re[l], c_im[l], log_step[l])
        s2 = _ssm(u, wbr, wbi, ar, ai, cr, ci,
                  d_skip[l][None].astype(F32), w_glu[l].astype(BF16))

        x = _merge_ffn(
            x, o, s2, norm_mix[l][None], w_l[:, n_split:],
            w_a[l].astype(BF16), w_b[l].astype(BF16), w_o[l].astype(BF16),
            norm_ffn[l][None],
            w1[l].astype(BF16), w3[l].astype(BF16), w2[l].astype(BF16))
    return x
```

```python
import functools
import math

import jax
import jax.numpy as jnp
from jax import lax
from jax.experimental import pallas as pl
from jax.experimental.pallas import tpu as pltpu

D_MODEL = 1024
N_HEADS = 4
HEAD_DIM = 64
V_DIM = 2 * HEAD_DIM
QK_WIDTH = N_HEADS * 2 * HEAD_DIM
ATTN_WIDTH = N_HEADS * V_DIM
SSM_WIDTH = D_MODEL // 2
SSM_GROUP = 16
SSM_GROUPS = SSM_WIDTH // SSM_GROUP
SSM_STATE = 64
STATE_LANES = SSM_GROUPS * SSM_STATE
D_FF = 2816
N_BUCKETS = 32
MAX_DISTANCE = 128
EPS = 1e-6
NEG = -1e30

MXU_TILE = 256
SUBLANES = 8

TM_PROJ = 512
TQ = 512
LOG2E = math.log2(math.e)
SCAN_STEPS = 64
SCAN_CHUNKS = 2
SCAN_STRIP = 512
FF_CHUNK = 256
VMEM_LIMIT = 56 * 1024 * 1024

BF16 = jnp.bfloat16
F32 = jnp.float32


def _dot(a, b):
    return jnp.dot(a, b, preferred_element_type=F32)


def _dot_nt(a, b):
    return lax.dot_general(a, b, (((1,), (1,)), ((), ())), preferred_element_type=F32)


def _rms(x, g):
    return x * lax.rsqrt(jnp.mean(x * x, axis=-1, keepdims=True) + EPS) * g


def _resident(shape):
    zeros = (0,) * len(shape)
    return pl.BlockSpec(shape, lambda *_: zeros, pipeline_mode=pl.Buffered(1))


def _bias_tiles_kernel(brel_ref, out_ref):
    h = pl.program_id(0)
    sub = MAX_DISTANCE
    r = lax.broadcasted_iota(jnp.int32, (sub, sub), 0)
    c = lax.broadcasted_iota(jnp.int32, (sub, sub), 1)

    def expand(d):
        rel = r - c + d * sub

        def body(n, tile):
            return jnp.where(rel == n, brel_ref[h, n], tile)

        tile = lax.fori_loop(0, MAX_DISTANCE, body, jnp.zeros((sub, sub), F32))
        return jnp.where(rel < 0, NEG, tile)

    by_diag = {0: expand(0), 1: expand(1)}
    for t, off in enumerate((TQ, 0)):
        for a in range(TQ // sub):
            for b in range(TQ // sub):
                d = a - b + off // sub
                fill = NEG if d < 0 else 0.0
                out_ref[t, a * sub:(a + 1) * sub, b * sub:(b + 1) * sub] = by_diag.get(
                    d, jnp.full((sub, sub), fill, F32))


def _bias_tiles(brel):
    return pl.pallas_call(
        _bias_tiles_kernel,
        grid=(N_HEADS,),
        in_specs=[pl.BlockSpec(memory_space=pltpu.SMEM)],
        out_specs=pl.BlockSpec((None, 2, TQ, TQ), lambda h: (h, 0, 0, 0)),
        out_shape=jax.ShapeDtypeStruct((N_HEADS, 2, TQ, TQ), F32),
        name="bias_tiles",
    )(brel)


def _inproj_kernel(x_ref, g_ref, w_ref, seg_ref, qg_ref, kg_ref,
                   q_ref, k_ref, v_ref, u_ref):
    h = _rms(x_ref[...], g_ref[...]).astype(BF16)

    def qk_norm(z, gain):
        ss = _dot((z * z).astype(BF16), seg_ref[...])
        return z * lax.rsqrt(ss * (1.0 / HEAD_DIM) + EPS) * gain

    zq = _dot(h, w_ref[:, 0:QK_WIDTH])
    q_ref[...] = qk_norm(zq, qg_ref[...]).astype(BF16)
    zk = _dot(h, w_ref[:, QK_WIDTH:2 * QK_WIDTH])
    k_ref[...] = qk_norm(zk, kg_ref[...]).astype(BF16)
    v_ref[...] = _dot(h, w_ref[:, 2 * QK_WIDTH:2 * QK_WIDTH + ATTN_WIDTH]).astype(BF16)
    u_ref[...] = _dot(h, w_ref[:, 2 * QK_WIDTH + ATTN_WIDTH:]).astype(BF16)


def _inproj(x, g, w, seg, qg, kg):
    B, S, D = x.shape
    W = w.shape[1]
    tok = lambda width: pl.BlockSpec((None, TM_PROJ, width), lambda b, i: (b, i, 0))
    return pl.pallas_call(
        _inproj_kernel,
        grid=(B, S // TM_PROJ),
        in_specs=[tok(D), _resident((1, D)), _resident((D, W)),
                  _resident((QK_WIDTH, QK_WIDTH)),
                  _resident((1, QK_WIDTH)), _resident((1, QK_WIDTH))],
        out_specs=[tok(QK_WIDTH), tok(QK_WIDTH), tok(ATTN_WIDTH), tok(SSM_WIDTH)],
        out_shape=[jax.ShapeDtypeStruct((B, S, QK_WIDTH), BF16),
                   jax.ShapeDtypeStruct((B, S, QK_WIDTH), BF16),
                   jax.ShapeDtypeStruct((B, S, ATTN_WIDTH), BF16),
                   jax.ShapeDtypeStruct((B, S, SSM_WIDTH), BF16)],
        compiler_params=pltpu.CompilerParams(
            dimension_semantics=("parallel", "parallel"), vmem_limit_bytes=VMEM_LIMIT),
        name="inproj",
    )(x, g, w, seg, qg, kg)


def _attn_kernel(lamv_ref, subln_ref, q_ref, k_ref, v_ref, bias_ref, o_ref,
                 *, lam_init):
    qi = pl.program_id(2)
    q = q_ref[...]
    lane = lax.broadcasted_iota(jnp.int32, q.shape, 1)
    zero = jnp.zeros_like(q)
    qm = (jnp.where(lane < HEAD_DIM, q, zero), jnp.where(lane >= HEAD_DIM, q, zero))
    ones = jnp.ones((TQ, V_DIM), BF16)
    lv = lamv_ref[...]
    lam = (jnp.exp(jnp.sum(lv[0:1] * lv[1:2], axis=-1, keepdims=True))
           - jnp.exp(jnp.sum(lv[2:3] * lv[3:4], axis=-1, keepdims=True)) + lam_init)

    def attend(n_tiles):
        m = [jnp.full((TQ, V_DIM), NEG, F32)] * 2
        acc = [jnp.zeros((TQ, 2 * V_DIM), F32)] * 2
        for j in range(n_tiles):
            rows = slice(j * TQ, (j + 1) * TQ)
            kb = k_ref[rows, :]
            v1 = jnp.concatenate([v_ref[rows, :], ones], axis=1)
            bias = {n_tiles - 1: 1, n_tiles - 2: 0}.get(j)
            for mp in range(2):
                s = _dot_nt(qm[mp], kb)
                if bias is not None:
                    s = s + bias_ref[bias]
                m_new = jnp.maximum(m[mp], jnp.max(s, axis=-1, keepdims=True))
                alpha = jnp.exp2(m[mp] - m_new)
                p = jnp.exp2(s - jnp.concatenate([m_new] * (TQ // V_DIM), axis=1))
                acc[mp] = (jnp.concatenate([alpha, alpha], axis=1) * acc[mp]
                           + _dot(p.astype(BF16), v1))
                m[mp] = m_new
        o = (acc[0][:, :V_DIM] / acc[0][:, V_DIM:]
             - lam * (acc[1][:, :V_DIM] / acc[1][:, V_DIM:]))
        o_ref[...] = (_rms(o, subln_ref[...]) * (1.0 - lam_init)).astype(BF16)

    for n_tiles in range(1, k_ref.shape[0] // TQ + 1):
        pl.when(qi == n_tiles - 1)(functools.partial(attend, n_tiles))


def _attention(lamv, subln, q, k, v, bias_tiles, lam_init):
    B, S, _ = q.shape
    return pl.pallas_call(
        functools.partial(_attn_kernel, lam_init=lam_init),
        grid=(B, N_HEADS, S // TQ),
        in_specs=[_resident((4, HEAD_DIM)), _resident((1, V_DIM)),
                  pl.BlockSpec((None, TQ, V_DIM), lambda b, h, i: (b, i, h)),
                  pl.BlockSpec((None, S, V_DIM), lambda b, h, i: (b, 0, h)),
                  pl.BlockSpec((None, S, V_DIM), lambda b, h, i: (b, 0, h)),
                  pl.BlockSpec((None, 2, TQ, TQ), lambda b, h, i: (h, 0, 0, 0))],
        out_specs=pl.BlockSpec((None, TQ, V_DIM), lambda b, h, i: (b, i, h)),
        out_shape=jax.ShapeDtypeStruct((B, S, ATTN_WIDTH), BF16),
        compiler_params=pltpu.CompilerParams(
            dimension_semantics=("parallel", "parallel", "parallel"),
            vmem_limit_bytes=VMEM_LIMIT),
        name="diff_attention",
    )(lamv, subln, q, k, v, bias_tiles)


def _ssm_kernel(u_ref, perm_ref, permt_ref, wbr_ref, wbi_ref, ar_ref, ai_ref, cr_ref, ci_ref,
                dskip_ref, wglu_ref, out_ref, bur, bui, xr_s, xi_s):
    @pl.when(pl.program_id(0) == 0)
    def _():
        xr_s[...] = jnp.zeros(xr_s.shape, F32)
        xi_s[...] = jnp.zeros(xi_s.shape, F32)


    def b_project(c):
        u_bt = u_ref[:, c * SCAN_STEPS:(c + 1) * SCAN_STEPS, :]
        u = _dot(perm_ref[...], u_bt.reshape(SCAN_STEPS * SUBLANES, SSM_WIDTH))
        ub = u.astype(BF16)
        for j in range(STATE_LANES // MXU_TILE):
            kt = (j * MXU_TILE // SSM_STATE * SSM_GROUP) // MXU_TILE
            rows = slice(kt * MXU_TILE, (kt + 1) * MXU_TILE)
            cols = slice(j * MXU_TILE, (j + 1) * MXU_TILE)
            bur[c, :, cols] = _dot(ub[:, rows], wbr_ref[rows, cols])
            bui[c, :, cols] = _dot(ub[:, rows], wbi_ref[rows, cols])
        return u

    def scan(c):
        for s in range(STATE_LANES // SCAN_STRIP):
            sl = slice(s * SCAN_STRIP, (s + 1) * SCAN_STRIP)
            a_r = ar_ref[:, sl]
            a_i = ai_ref[:, sl]
            xr = xr_s[:, sl]
            xi = xi_s[:, sl]
            for t in range(SCAN_STEPS):
                step = slice(t * SUBLANES, (t + 1) * SUBLANES)
                xr, xi = (a_r * xr - a_i * xi + bur[c, step, sl],
                          a_r * xi + a_i * xr + bui[c, step, sl])
                bur[c, step, sl] = xr
                bui[c, step, sl] = xi
            xr_s[:, sl] = xr
            xi_s[:, sl] = xi

    def c_project(c, u):
        halves = []
        k_per_half = STATE_LANES // 2
        n_per_half = SSM_WIDTH // 2
        for m in range(2):
            rows = slice(m * k_per_half, (m + 1) * k_per_half)
            cols = slice(m * n_per_half, (m + 1) * n_per_half)
            halves.append(_dot(bur[c, :, rows].astype(BF16), cr_ref[rows, cols])
                          + _dot(bui[c, :, rows].astype(BF16), ci_ref[rows, cols]))
        y = jnp.concatenate(halves, axis=-1) + dskip_ref[...] * u
        s = jax.nn.gelu(y)
        s = s * jax.nn.sigmoid(_dot(s.astype(BF16), wglu_ref[...]))
        s_bt = _dot(permt_ref[...], s.astype(BF16)).astype(BF16)
        out_ref[:, c * SCAN_STEPS:(c + 1) * SCAN_STEPS, :] = s_bt.reshape(
            SUBLANES, SCAN_STEPS, SSM_WIDTH)

    us = [b_project(c) for c in range(SCAN_CHUNKS)]
    for c in range(SCAN_CHUNKS):
        scan(c)
    for c in range(SCAN_CHUNKS):
        c_project(c, us[c])


def _ssm(u, wbr, wbi, ar, ai, cr, ci, dskip, wglu):
    rows = SCAN_STEPS * SUBLANES
    B, S, _ = u.shape
    r = jnp.arange(rows)
    perm = (jnp.arange(rows)[None, :] == ((r % B) * SCAN_STEPS + r // B)[:, None]).astype(BF16)
    blk = pl.BlockSpec((B, SCAN_CHUNKS * SCAN_STEPS, SSM_WIDTH), lambda c: (0, c, 0))
    return pl.pallas_call(
        _ssm_kernel,
        grid=(S // (SCAN_CHUNKS * SCAN_STEPS),),
        in_specs=[blk, _resident(perm.shape), _resident(perm.shape),
                  _resident(wbr.shape), _resident(wbi.shape),
                  _resident(ar.shape), _resident(ai.shape),
                  _resident(cr.shape), _resident(ci.shape),
                  _resident(dskip.shape), _resident(wglu.shape)],
        out_specs=blk,
        out_shape=jax.ShapeDtypeStruct((B, S, SSM_WIDTH), BF16),
        scratch_shapes=[pltpu.VMEM((SCAN_CHUNKS, rows, STATE_LANES), F32),
                        pltpu.VMEM((SCAN_CHUNKS, rows, STATE_LANES), F32),
                        pltpu.VMEM((SUBLANES, STATE_LANES), F32),
                        pltpu.VMEM((SUBLANES, STATE_LANES), F32)],
        compiler_params=pltpu.CompilerParams(
            dimension_semantics=("arbitrary",), vmem_limit_bytes=VMEM_LIMIT),
        name="s5_branch",
    )(u, perm, perm.T, wbr, wbi, ar, ai, cr, ci, dskip, wglu)


def _merge_ffn_kernel(x_ref, o_ref, s_ref, g1_ref, wg_ref, wa_ref, wb_ref, wo_ref,
                      g2_ref, w1_ref, w3_ref, w2_ref, out_ref, acc_s):
    x = x_ref[...]
    h = _rms(x, g1_ref[...]).astype(BF16)
    mixed = jax.nn.sigmoid(_dot(h, wg_ref[:, 0:D_MODEL])) * _dot(o_ref[...], wa_ref[...])
    mixed += jax.nn.sigmoid(_dot(h, wg_ref[:, D_MODEL:])) * _dot(s_ref[...], wb_ref[...])
    x1 = x + _dot(mixed.astype(BF16), wo_ref[...])

    h2 = _rms(x1, g2_ref[...]).astype(BF16)
    acc_s[...] = x1

    for c in range(D_FF // FF_CHUNK):
        cols = slice(c * FF_CHUNK, (c + 1) * FF_CHUNK)
        t = jax.nn.silu(_dot(h2, w1_ref[:, cols])) * _dot(h2, w3_ref[:, cols])
        acc_s[...] += _dot(t.astype(BF16), w2_ref[cols, :])
    out_ref[...] = acc_s[...]


def _merge_ffn(x, o, s2, g1, wg, wa, wb, wo, g2, w1, w3, w2):
    B, S, D = x.shape
    tok = lambda width: pl.BlockSpec((None, TM_PROJ, width), lambda b, i: (b, i, 0))
    return pl.pallas_call(
        _merge_ffn_kernel,
        grid=(B, S // TM_PROJ),
        in_specs=[tok(D), tok(ATTN_WIDTH),
                  tok(SSM_WIDTH),
                  _resident(g1.shape), _resident(wg.shape), _resident(wa.shape),
                  _resident(wb.shape), _resident(wo.shape), _resident(g2.shape),
                  _resident(w1.shape), _resident(w3.shape), _resident(w2.shape)],
        out_specs=tok(D),
        out_shape=jax.ShapeDtypeStruct((B, S, D), F32),
        scratch_shapes=[pltpu.VMEM((TM_PROJ, D), F32)],
        compiler_params=pltpu.CompilerParams(
            dimension_semantics=("parallel", "parallel"), vmem_limit_bytes=VMEM_LIMIT),
        name="merge_ffn",
    )(x, o, s2, g1, wg, wa, wb, wo, g2, w1, w3, w2)


def _t5_bucket(n):
    max_exact = N_BUCKETS // 2
    is_small = n < max_exact
    nf = jnp.maximum(n, 1).astype(F32)
    large = max_exact + (jnp.log(nf / max_exact) / math.log(MAX_DISTANCE / max_exact)
                         * (N_BUCKETS - max_exact)).astype(jnp.int32)
    large = jnp.minimum(large, N_BUCKETS - 1)
    return jnp.where(is_small, n, large)


def _ssm_params(lam_re, lam_im, b_re, b_im, c_re, c_im, log_step):
    step = jnp.exp(log_step.astype(F32))[:, None]
    lr = lam_re.astype(F32)
    li = lam_im.astype(F32)
    decay = jnp.exp(lr * step)
    ab_re = decay * jnp.cos(li * step)
    ab_im = decay * jnp.sin(li * step)
    nr = ab_re - 1.0
    ni = ab_im
    den = lr * lr + li * li
    f_re = (nr * lr + ni * li) / den
    f_im = (ni * lr - nr * li) / den
    br = b_re.astype(F32)
    bi = b_im.astype(F32)
    bb_re = f_re[..., None] * br - f_im[..., None] * bi
    bb_im = f_re[..., None] * bi + f_im[..., None] * br

    def blockdiag(w):
        g, n, m = w.shape
        stacked = w.transpose(0, 2, 1).reshape(g * m, n)
        same = (jnp.arange(g * m)[:, None] // m) == (jnp.arange(g * n)[None, :] // n)
        return jnp.where(same, jnp.tile(stacked, (1, g)), 0.0).astype(BF16)

    in_blockdiag = blockdiag
    out_blockdiag = blockdiag

    bcast = lambda a: jnp.broadcast_to(a.reshape(1, STATE_LANES), (SUBLANES, STATE_LANES))
    return (in_blockdiag(bb_re), in_blockdiag(bb_im), bcast(ab_re), bcast(ab_im),
            out_blockdiag(c_re.astype(F32)), out_blockdiag(-c_im.astype(F32)))


def kernel(x, rel_bias, norm_mix, w_in, q_gain, k_gain, lambda_q1, lambda_k1, lambda_q2, lambda_k2, subln, w_a, lam_re, lam_im, b_re, b_im, c_re, c_im, d_skip, log_step, w_glu, w_b, w_o, norm_ffn, w1, w3, w2):
    B, S, D = x.shape
    depth = w_in.shape[0]
    assert B == SUBLANES and D == D_MODEL

    table = rel_bias.astype(F32)
    brel = (table[_t5_bucket(jnp.arange(MAX_DISTANCE))] - table[N_BUCKETS - 1][None]).T * LOG2E
    bias_tiles = _bias_tiles(brel)

    seg = (jnp.arange(QK_WIDTH)[:, None] // HEAD_DIM
           == jnp.arange(QK_WIDTH)[None, :] // HEAD_DIM).astype(BF16)
    n_split = 2 * QK_WIDTH + ATTN_WIDTH + SSM_WIDTH

    for l in range(depth):
        lam_init = 0.8 - 0.6 * math.exp(-0.3 * l)
        w_l = w_in[l].astype(BF16)
        qg = jnp.tile(q_gain[l].astype(F32), QK_WIDTH // HEAD_DIM)[None] * (HEAD_DIM ** -0.5 * LOG2E)
        kg = jnp.tile(k_gain[l].astype(F32), QK_WIDTH // HEAD_DIM)[None]
        q, k, v, u = _inproj(x, norm_mix[l][None], w_l[:, :n_split], seg, qg, kg)

        lamv = jnp.stack([lambda_q1[l], lambda_k1[l], lambda_q2[l], lambda_k2[l]]).astype(F32)
        o = _attention(lamv, subln[l][None].astype(F32), q, k, v, bias_tiles, lam_init)

        wbr, wbi, ar, ai, cr, ci = _ssm_params(lam_re[l], lam_im[l], b_re[l], b_im[l],
                                               c_re[l], c_im[l], log_step[l])
        s2 = _ssm(u, wbr, wbi, ar, ai, cr, ci,
                  d_skip[l][None].astype(F32), w_glu[l].astype(BF16))

        x = _merge_ffn(
            x, o, s2, norm_mix[l][None], w_l[:, n_split:],
            w_a[l].astype(BF16), w_b[l].astype(BF16), w_o[l].astype(BF16),
            norm_ffn[l][None],
            w1[l].astype(BF16), w3[l].astype(BF16), w2[l].astype(BF16))
    return x
```

```python
import functools
import math

import jax
import jax.numpy as jnp
from jax import lax
from jax.experimental import pallas as pl
from jax.experimental.pallas import tpu as pltpu

D_MODEL = 1024
N_HEADS = 4
HEAD_DIM = 64
V_DIM = 2 * HEAD_DIM
QK_WIDTH = N_HEADS * 2 * HEAD_DIM
ATTN_WIDTH = N_HEADS * V_DIM
SSM_WIDTH = D_MODEL // 2
SSM_GROUP = 16
SSM_GROUPS = SSM_WIDTH // SSM_GROUP
SSM_STATE = 64
STATE_LANES = SSM_GROUPS * SSM_STATE
D_FF = 2816
N_BUCKETS = 32
MAX_DISTANCE = 128
EPS = 1e-6
NEG = -1e30

MXU_TILE = 256
SUBLANES = 8

TM_PROJ = 512
TQ = 256
LOG2E = math.log2(math.e)
SCAN_STEPS = 64
SCAN_CHUNKS = 2
SCAN_STRIP = 512
FF_CHUNK = 256
VMEM_LIMIT = 56 * 1024 * 1024

BF16 = jnp.bfloat16
F32 = jnp.float32


def _dot(a, b):
    return jnp.dot(a, b, preferred_element_type=F32)


def _dot_nt(a, b):
    return lax.dot_general(a, b, (((1,), (1,)), ((), ())), preferred_element_type=F32)


def _rms(x, g):
    return x * lax.rsqrt(jnp.mean(x * x, axis=-1, keepdims=True) + EPS) * g


def _resident(shape):
    zeros = (0,) * len(shape)
    return pl.BlockSpec(shape, lambda *_: zeros, pipeline_mode=pl.Buffered(1))


def _bias_tiles_kernel(brel_ref, out_ref):
    h = pl.program_id(0)
    sub = MAX_DISTANCE
    r = lax.broadcasted_iota(jnp.int32, (sub, sub), 0)
    c = lax.broadcasted_iota(jnp.int32, (sub, sub), 1)

    def expand(d):
        rel = r - c + d * sub

        def body(n, tile):
            return jnp.where(rel == n, brel_ref[h, n], tile)

        tile = lax.fori_loop(0, MAX_DISTANCE, body, jnp.zeros((sub, sub), F32))
        return jnp.where(rel < 0, NEG, tile)

    by_diag = {0: expand(0), 1: expand(1)}
    for t, off in enumerate((TQ, 0)):
        for a in range(TQ // sub):
            for b in range(TQ // sub):
                d = a - b + off // sub
                fill = NEG if d < 0 else 0.0
                out_ref[t, a * sub:(a + 1) * sub, b * sub:(b + 1) * sub] = by_diag.get(
                    d, jnp.full((sub, sub), fill, F32))


def _bias_tiles(brel):
    return pl.pallas_call(
        _bias_tiles_kernel,
        grid=(N_HEADS,),
        in_specs=[pl.BlockSpec(memory_space=pltpu.SMEM)],
        out_specs=pl.BlockSpec((None, 2, TQ, TQ), lambda h: (h, 0, 0, 0)),
        out_shape=jax.ShapeDtypeStruct((N_HEADS, 2, TQ, TQ), F32),
        name="bias_tiles",
    )(brel)


def _inproj_kernel(x_ref, g_ref, w_ref, seg_ref, qg_ref, kg_ref,
                   q_ref, k_ref, v_ref, u_ref):
    h = _rms(x_ref[...], g_ref[...]).astype(BF16)

    def qk_norm(z, gain):
        ss = _dot((z * z).astype(BF16), seg_ref[...])
        return z * lax.rsqrt(ss * (1.0 / HEAD_DIM) + EPS) * gain

    zq = _dot(h, w_ref[:, 0:QK_WIDTH])
    q_ref[...] = qk_norm(zq, qg_ref[...]).astype(BF16)
    zk = _dot(h, w_ref[:, QK_WIDTH:2 * QK_WIDTH])
    k_ref[...] = qk_norm(zk, kg_ref[...]).astype(BF16)
    v_ref[...] = _dot(h, w_ref[:, 2 * QK_WIDTH:2 * QK_WIDTH + ATTN_WIDTH]).astype(BF16)
    u_ref[...] = _dot(h, w_ref[:, 2 * QK_WIDTH + ATTN_WIDTH:]).astype(BF16)


def _inproj(x, g, w, seg, qg, kg):
    B, S, D = x.shape
    W = w.shape[1]
    tok = lambda width: pl.BlockSpec((None, TM_PROJ, width), lambda b, i: (b, i, 0))
    return pl.pallas_call(
        _inproj_kernel,
        grid=(B, S // TM_PROJ),
        in_specs=[tok(D), _resident((1, D)), _resident((D, W)),
                  _resident((QK_WIDTH, QK_WIDTH)),
                  _resident((1, QK_WIDTH)), _resident((1, QK_WIDTH))],
        out_specs=[tok(QK_WIDTH), tok(QK_WIDTH), tok(ATTN_WIDTH), tok(SSM_WIDTH)],
        out_shape=[jax.ShapeDtypeStruct((B, S, QK_WIDTH), BF16),
                   jax.ShapeDtypeStruct((B, S, QK_WIDTH), BF16),
                   jax.ShapeDtypeStruct((B, S, ATTN_WIDTH), BF16),
                   jax.ShapeDtypeStruct((B, S, SSM_WIDTH), BF16)],
        compiler_params=pltpu.CompilerParams(
            dimension_semantics=("parallel", "parallel"), vmem_limit_bytes=VMEM_LIMIT),
        name="inproj",
    )(x, g, w, seg, qg, kg)


def _attn_kernel(lamv_ref, subln_ref, q_ref, k_ref, v_ref, bias_ref, o_ref,
                 *, lam_init):
    lane = lax.broadcasted_iota(jnp.int32, (TQ, V_DIM), 1)
    ones = jnp.ones((TQ, V_DIM), BF16)
    lv = lamv_ref[...]
    lam = (jnp.exp(jnp.sum(lv[0:1] * lv[1:2], axis=-1, keepdims=True))
           - jnp.exp(jnp.sum(lv[2:3] * lv[3:4], axis=-1, keepdims=True)) + lam_init)

    def attend(n_tiles):
        q_rows = slice((n_tiles - 1) * TQ, n_tiles * TQ)
        q = q_ref[q_rows, :]
        zero = jnp.zeros_like(q)
        qm = (jnp.where(lane < HEAD_DIM, q, zero), jnp.where(lane >= HEAD_DIM, q, zero))
        m = [jnp.full((TQ, V_DIM), NEG, F32)] * 2
        acc = [jnp.zeros((TQ, 2 * V_DIM), F32)] * 2
        for j in range(n_tiles):
            rows = slice(j * TQ, (j + 1) * TQ)
            kb = k_ref[rows, :]
            v1 = jnp.concatenate([v_ref[rows, :], ones], axis=1)
            bias = {n_tiles - 1: 1, n_tiles - 2: 0}.get(j)
            for mp in range(2):
                s = _dot_nt(qm[mp], kb)
                if bias is not None:
                    s = s + bias_ref[bias]
                m_new = jnp.maximum(m[mp], jnp.max(s, axis=-1, keepdims=True))
                alpha = jnp.exp2(m[mp] - m_new)
                p = jnp.exp2(s - jnp.concatenate([m_new] * (TQ // V_DIM), axis=1))
                acc[mp] = (jnp.concatenate([alpha, alpha], axis=1) * acc[mp]
                           + _dot(p.astype(BF16), v1))
                m[mp] = m_new
        o = (acc[0][:, :V_DIM] / acc[0][:, V_DIM:]
             - lam * (acc[1][:, :V_DIM] / acc[1][:, V_DIM:]))
        o_ref[q_rows, :] = (_rms(o, subln_ref[...]) * (1.0 - lam_init)).astype(BF16)

    for n_tiles in range(1, k_ref.shape[0] // TQ + 1):
        attend(n_tiles)


def _attention(lamv, subln, q, k, v, bias_tiles, lam_init):
    B, S, _ = q.shape
    head = pl.BlockSpec((None, S, V_DIM), lambda b, h: (b, 0, h))
    return pl.pallas_call(
        functools.partial(_attn_kernel, lam_init=lam_init),
        grid=(B, N_HEADS),
        in_specs=[_resident((4, HEAD_DIM)), _resident((1, V_DIM)),
                  head, head, head,
                  pl.BlockSpec((None, 2, TQ, TQ), lambda b, h: (h, 0, 0, 0))],
        out_specs=head,
        out_shape=jax.ShapeDtypeStruct((B, S, ATTN_WIDTH), BF16),
        compiler_params=pltpu.CompilerParams(
            dimension_semantics=("parallel", "parallel"), vmem_limit_bytes=VMEM_LIMIT),
        name="diff_attention",
    )(lamv, subln, q, k, v, bias_tiles)


def _ssm_kernel(u_ref, perm_ref, permt_ref, wbr_ref, wbi_ref, ar_ref, ai_ref, cr_ref, ci_ref,
                dskip_ref, wglu_ref, out_ref, bur, bui, xr_s, xi_s):
    @pl.when(pl.program_id(0) == 0)
    def _():
        xr_s[...] = jnp.zeros(xr_s.shape, F32)
        xi_s[...] = jnp.zeros(xi_s.shape, F32)


    def b_project(c):
        u_bt = u_ref[:, c * SCAN_STEPS:(c + 1) * SCAN_STEPS, :]
        u = _dot(perm_ref[...], u_bt.reshape(SCAN_STEPS * SUBLANES, SSM_WIDTH))
        ub = u.astype(BF16)
        for j in range(STATE_LANES // MXU_TILE):
            kt = (j * MXU_TILE // SSM_STATE * SSM_GROUP) // MXU_TILE
            rows = slice(kt * MXU_TILE, (kt + 1) * MXU_TILE)
            cols = slice(j * MXU_TILE, (j + 1) * MXU_TILE)
            bur[c, :, cols] = _dot(ub[:, rows], wbr_ref[rows, cols])
            bui[c, :, cols] = _dot(ub[:, rows], wbi_ref[rows, cols])
        return u

    def scan(c):
        for s in range(STATE_LANES // SCAN_STRIP):
            sl = slice(s * SCAN_STRIP, (s + 1) * SCAN_STRIP)
            a_r = ar_ref[:, sl]
            a_i = ai_ref[:, sl]
            xr = xr_s[:, sl]
            xi = xi_s[:, sl]
            for t in range(SCAN_STEPS):
                step = slice(t * SUBLANES, (t + 1) * SUBLANES)
                xr, xi = (a_r * xr - a_i * xi + bur[c, step, sl],
                          a_r * xi + a_i * xr + bui[c, step, sl])
                bur[c, step, sl] = xr
                bui[c, step, sl] = xi
            xr_s[:, sl] = xr
            xi_s[:, sl] = xi

    def c_project(c, u):
        halves = []
        k_per_half = STATE_LANES // 2
        n_per_half = SSM_WIDTH // 2
        for m in range(2):
            rows = slice(m * k_per_half, (m + 1) * k_per_half)
            cols = slice(m * n_per_half, (m + 1) * n_per_half)
            halves.append(_dot(bur[c, :, rows].astype(BF16), cr_ref[rows, cols])
                          + _dot(bui[c, :, rows].astype(BF16), ci_ref[rows, cols]))
        y = jnp.concatenate(halves, axis=-1) + dskip_ref[...] * u
        s = jax.nn.gelu(y)
        s = s * jax.nn.sigmoid(_dot(s.astype(BF16), wglu_ref[...]))
        s_bt = _dot(permt_ref[...], s.astype(BF16)).astype(BF16)
        out_ref[:, c * SCAN_STEPS:(c + 1) * SCAN_STEPS, :] = s_bt.reshape(
            SUBLANES, SCAN_STEPS, SSM_WIDTH)

    us = [b_project(c) for c in range(SCAN_CHUNKS)]
    for c in range(SCAN_CHUNKS):
        scan(c)
    for c in range(SCAN_CHUNKS):
        c_project(c, us[c])


def _ssm(u, wbr, wbi, ar, ai, cr, ci, dskip, wglu):
    rows = SCAN_STEPS * SUBLANES
    B, S, _ = u.shape
    r = jnp.arange(rows)
    perm = (jnp.arange(rows)[None, :] == ((r % B) * SCAN_STEPS + r // B)[:, None]).astype(BF16)
    blk = pl.BlockSpec((B, SCAN_CHUNKS * SCAN_STEPS, SSM_WIDTH), lambda c: (0, c, 0))
    return pl.pallas_call(
        _ssm_kernel,
        grid=(S // (SCAN_CHUNKS * SCAN_STEPS),),
        in_specs=[blk, _resident(perm.shape), _resident(perm.shape),
                  _resident(wbr.shape), _resident(wbi.shape),
                  _resident(ar.shape), _resident(ai.shape),
                  _resident(cr.shape), _resident(ci.shape),
                  _resident(dskip.shape), _resident(wglu.shape)],
        out_specs=blk,
        out_shape=jax.ShapeDtypeStruct((B, S, SSM_WIDTH), BF16),
        scratch_shapes=[pltpu.VMEM((SCAN_CHUNKS, rows, STATE_LANES), F32),
                        pltpu.VMEM((SCAN_CHUNKS, rows, STATE_LANES), F32),
                        pltpu.VMEM((SUBLANES, STATE_LANES), F32),
                        pltpu.VMEM((SUBLANES, STATE_LANES), F32)],
        compiler_params=pltpu.CompilerParams(
            dimension_semantics=("arbitrary",), vmem_limit_bytes=VMEM_LIMIT),
        name="s5_branch",
    )(u, perm, perm.T, wbr, wbi, ar, ai, cr, ci, dskip, wglu)


def _merge_ffn_kernel(x_ref, o_ref, s_ref, g1_ref, wg_ref, wa_ref, wb_ref, wo_ref,
                      g2_ref, w1_ref, w3_ref, w2_ref, out_ref, acc_s):
    x = x_ref[...]
    h = _rms(x, g1_ref[...]).astype(BF16)
    mixed = jax.nn.sigmoid(_dot(h, wg_ref[:, 0:D_MODEL])) * _dot(o_ref[...], wa_ref[...])
    mixed += jax.nn.sigmoid(_dot(h, wg_ref[:, D_MODEL:])) * _dot(s_ref[...], wb_ref[...])
    x1 = x + _dot(mixed.astype(BF16), wo_ref[...])

    h2 = _rms(x1, g2_ref[...]).astype(BF16)
    acc_s[...] = x1

    for c in range(D_FF // FF_CHUNK):
        cols = slice(c * FF_CHUNK, (c + 1) * FF_CHUNK)
        t = jax.nn.silu(_dot(h2, w1_ref[:, cols])) * _dot(h2, w3_ref[:, cols])
        acc_s[...] += _dot(t.astype(BF16), w2_ref[cols, :])
    out_ref[...] = acc_s[...]


def _merge_ffn(x, o, s2, g1, wg, wa, wb, wo, g2, w1, w3, w2):
    B, S, D = x.shape
    tok = lambda width: pl.BlockSpec((None, TM_PROJ, width), lambda b, i: (b, i, 0))
    return pl.pallas_call(
        _merge_ffn_kernel,
        grid=(B, S // TM_PROJ),
        in_specs=[tok(D), tok(ATTN_WIDTH),
                  tok(SSM_WIDTH),
                  _resident(g1.shape), _resident(wg.shape), _resident(wa.shape),
                  _resident(wb.shape), _resident(wo.shape), _resident(g2.shape),
                  _resident(w1.shape), _resident(w3.shape), _resident(w2.shape)],
        out_specs=tok(D),
        out_shape=jax.ShapeDtypeStruct((B, S, D), F32),
        scratch_shapes=[pltpu.VMEM((TM_PROJ, D), F32)],
        compiler_params=pltpu.CompilerParams(
            dimension_semantics=("parallel", "parallel"), vmem_limit_bytes=VMEM_LIMIT),
        name="merge_ffn",
    )(x, o, s2, g1, wg, wa, wb, wo, g2, w1, w3, w2)


def _t5_bucket(n):
    max_exact = N_BUCKETS // 2
    is_small = n < max_exact
    nf = jnp.maximum(n, 1).astype(F32)
    large = max_exact + (jnp.log(nf / max_exact) / math.log(MAX_DISTANCE / max_exact)
                         * (N_BUCKETS - max_exact)).astype(jnp.int32)
    large = jnp.minimum(large, N_BUCKETS - 1)
    return jnp.where(is_small, n, large)


def _ssm_params(lam_re, lam_im, b_re, b_im, c_re, c_im, log_step):
    step = jnp.exp(log_step.astype(F32))[:, None]
    lr = lam_re.astype(F32)
    li = lam_im.astype(F32)
    decay = jnp.exp(lr * step)
    ab_re = decay * jnp.cos(li * step)
    ab_im = decay * jnp.sin(li * step)
    nr = ab_re - 1.0
    ni = ab_im
    den = lr * lr + li * li
    f_re = (nr * lr + ni * li) / den
    f_im = (ni * lr - nr * li) / den
    br = b_re.astype(F32)
    bi = b_im.astype(F32)
    bb_re = f_re[..., None] * br - f_im[..., None] * bi
    bb_im = f_re[..., None] * bi + f_im[..., None] * br

    def blockdiag(w):
        g, n, m = w.shape
        stacked = w.transpose(0, 2, 1).reshape(g * m, n)
        same = (jnp.arange(g * m)[:, None] // m) == (jnp.arange(g * n)[None, :] // n)
        return jnp.where(same, jnp.tile(stacked, (1, g)), 0.0).astype(BF16)

    in_blockdiag = blockdiag
    out_blockdiag = blockdiag

    bcast = lambda a: jnp.broadcast_to(a.reshape(1, STATE_LANES), (SUBLANES, STATE_LANES))
    return (in_blockdiag(bb_re), in_blockdiag(bb_im), bcast(ab_re), bcast(ab_im),
            out_blockdiag(c_re.astype(F32)), out_blockdiag(-c_im.astype(F32)))


def kernel(x, rel_bias, norm_mix, w_in, q_gain, k_gain, lambda_q1, lambda_k1, lambda_q2, lambda_k2, subln, w_a, lam_re, lam_im, b_re, b_im, c_re, c_im, d_skip, log_step, w_glu, w_b, w_o, norm_ffn, w1, w3, w2):
    B, S, D = x.shape
    depth = w_in.shape[0]
    assert B == SUBLANES and D == D_MODEL

    table = rel_bias.astype(F32)
    brel = (table[_t5_bucket(jnp.arange(MAX_DISTANCE))] - table[N_BUCKETS - 1][None]).T * LOG2E
    bias_tiles = _bias_tiles(brel)

    seg = (jnp.arange(QK_WIDTH)[:, None] // HEAD_DIM
           == jnp.arange(QK_WIDTH)[None, :] // HEAD_DIM).astype(BF16)
    n_split = 2 * QK_WIDTH + ATTN_WIDTH + SSM_WIDTH

    for l in range(depth):
        lam_init = 0.8 - 0.6 * math.exp(-0.3 * l)
        w_l = w_in[l].astype(BF16)
        qg = jnp.tile(q_gain[l].astype(F32), QK_WIDTH // HEAD_DIM)[None] * (HEAD_DIM ** -0.5 * LOG2E)
        kg = jnp.tile(k_gain[l].astype(F32), QK_WIDTH // HEAD_DIM)[None]
        q, k, v, u = _inproj(x, norm_mix[l][None], w_l[:, :n_split], seg, qg, kg)

        lamv = jnp.stack([lambda_q1[l], lambda_k1[l], lambda_q2[l], lambda_k2[l]]).astype(F32)
        o = _attention(lamv, subln[l][None].astype(F32), q, k, v, bias_tiles, lam_init)

        wbr, wbi, ar, ai, cr, ci = _ssm_params(lam_re[l], lam_im[l], b_re[l], b_im[l],
                                               c_re[l], c_im[l], log_step[l])
        s2 = _ssm(u, wbr, wbi, ar, ai, cr, ci,
                  d_skip[l][None].astype(F32), w_glu[l].astype(BF16))

        x = _merge_ffn(
            x, o, s2, norm_mix[l][None], w_l[:, n_split:],
            w_a[l].astype(BF16), w_b[l].astype(BF16), w_o[l].astype(BF16),
            norm_ffn[l][None],
            w1[l].astype(BF16), w3[l].astype(BF16), w2[l].astype(BF16))
    return x
```

```python
import functools
import math

import jax
import jax.numpy as jnp
from jax import lax
from jax.experimental import pallas as pl
from jax.experimental.pallas import tpu as pltpu

D_MODEL = 1024
N_HEADS = 4
HEAD_DIM = 64
V_DIM = 2 * HEAD_DIM
QK_WIDTH = N_HEADS * 2 * HEAD_DIM
ATTN_WIDTH = N_HEADS * V_DIM
SSM_WIDTH = D_MODEL // 2
SSM_GROUP = 16
SSM_GROUPS = SSM_WIDTH // SSM_GROUP
SSM_STATE = 64
STATE_LANES = SSM_GROUPS * SSM_STATE
D_FF = 2816
N_BUCKETS = 32
MAX_DISTANCE = 128
EPS = 1e-6
NEG = -1e30

MXU_TILE = 256
SUBLANES = 8

TM_PROJ = 512
TQ = 256
LOG2E = math.log2(math.e)
SCAN_STEPS = 64
SCAN_CHUNKS = 2
SCAN_STRIP = 512
FF_CHUNK = 256
VMEM_LIMIT = 56 * 1024 * 1024

BF16 = jnp.bfloat16
F32 = jnp.float32


def _dot(a, b):
    return jnp.dot(a, b, preferred_element_type=F32)


def _dot_nt(a, b):
    return lax.dot_general(a, b, (((1,), (1,)), ((), ())), preferred_element_type=F32)


def _rms(x, g):
    return x * lax.rsqrt(jnp.mean(x * x, axis=-1, keepdims=True) + EPS) * g


def _resident(shape):
    zeros = (0,) * len(shape)
    return pl.BlockSpec(shape, lambda *_: zeros, pipeline_mode=pl.Buffered(1))


def _bias_tiles_kernel(brev_ref, out_ref):
    sub = MAX_DISTANCE
    r = lax.broadcasted_iota(jnp.int32, (sub, sub), 0)
    c = lax.broadcasted_iota(jnp.int32, (sub, sub), 1)
    y = pltpu.roll(jnp.broadcast_to(brev_ref[...], (sub, sub)), 0, 1, stride=1, stride_axis=0)
    by_diag = {0: jnp.where(c <= r, y, NEG), 1: jnp.where(c > r, y, 0.0)}
    for t, off in enumerate((TQ, 0)):
        for a in range(TQ // sub):
            for b in range(TQ // sub):
                d = a - b + off // sub
                fill = NEG if d < 0 else 0.0
                out_ref[t, a * sub:(a + 1) * sub, b * sub:(b + 1) * sub] = by_diag.get(
                    d, jnp.full((sub, sub), fill, F32))


def _bias_tiles(brev):
    return pl.pallas_call(
        _bias_tiles_kernel,
        grid=(N_HEADS,),
        in_specs=[pl.BlockSpec((None, 1, MAX_DISTANCE), lambda h: (h, 0, 0))],
        out_specs=pl.BlockSpec((None, 2, TQ, TQ), lambda h: (h, 0, 0, 0)),
        out_shape=jax.ShapeDtypeStruct((N_HEADS, 2, TQ, TQ), F32),
        name="bias_tiles",
    )(brev)


def _inproj_kernel(x_ref, g_ref, w_ref, seg_ref, qg_ref, kg_ref,
                   q_ref, k_ref, v_ref, u_ref):
    h = _rms(x_ref[...], g_ref[...]).astype(BF16)

    def qk_norm(z, gain):
        z2 = (z * z).astype(BF16)
        ss = jnp.concatenate(
            [_dot(z2[:, t * MXU_TILE:(t + 1) * MXU_TILE], seg_ref[...])
             for t in range(QK_WIDTH // MXU_TILE)], axis=1)
        return z * lax.rsqrt(ss * (1.0 / HEAD_DIM) + EPS) * gain

    zq = _dot(h, w_ref[:, 0:QK_WIDTH])
    q_ref[...] = qk_norm(zq, qg_ref[...]).astype(BF16)
    zk = _dot(h, w_ref[:, QK_WIDTH:2 * QK_WIDTH])
    k_ref[...] = qk_norm(zk, kg_ref[...]).astype(BF16)
    v_ref[...] = _dot(h, w_ref[:, 2 * QK_WIDTH:2 * QK_WIDTH + ATTN_WIDTH]).astype(BF16)
    u_ref[...] = _dot(h, w_ref[:, 2 * QK_WIDTH + ATTN_WIDTH:]).astype(BF16)


def _inproj(x, g, w, seg, qg, kg):
    B, S, D = x.shape
    W = w.shape[1]
    tok = lambda width: pl.BlockSpec((None, TM_PROJ, width), lambda b, i: (b, i, 0))
    return pl.pallas_call(
        _inproj_kernel,
        grid=(B, S // TM_PROJ),
        in_specs=[tok(D), _resident((1, D)), _resident((D, W)),
                  _resident((MXU_TILE, MXU_TILE)),
                  _resident((1, QK_WIDTH)), _resident((1, QK_WIDTH))],
        out_specs=[tok(QK_WIDTH), tok(QK_WIDTH), tok(ATTN_WIDTH), tok(SSM_WIDTH)],
        out_shape=[jax.ShapeDtypeStruct((B, S, QK_WIDTH), BF16),
                   jax.ShapeDtypeStruct((B, S, QK_WIDTH), BF16),
                   jax.ShapeDtypeStruct((B, S, ATTN_WIDTH), BF16),
                   jax.ShapeDtypeStruct((B, S, SSM_WIDTH), BF16)],
        compiler_params=pltpu.CompilerParams(
            dimension_semantics=("parallel", "parallel"), vmem_limit_bytes=VMEM_LIMIT),
        name="inproj",
    )(x, g, w, seg, qg, kg)


def _attn_kernel(lamv_ref, subln_ref, q_ref, k_ref, v_ref, bias_ref, o_ref,
                 *, lam_init):
    lane = lax.broadcasted_iota(jnp.int32, (TQ, V_DIM), 1)
    ones = jnp.ones((TQ, V_DIM), BF16)
    lv = lamv_ref[...]
    lam = (jnp.exp(jnp.sum(lv[0:1] * lv[1:2], axis=-1, keepdims=True))
           - jnp.exp(jnp.sum(lv[2:3] * lv[3:4], axis=-1, keepdims=True)) + lam_init)

    def attend(n_tiles):
        q_rows = slice((n_tiles - 1) * TQ, n_tiles * TQ)
        q = q_ref[q_rows, :]
        zero = jnp.zeros_like(q)
        qm = (jnp.where(lane < HEAD_DIM, q, zero), jnp.where(lane >= HEAD_DIM, q, zero))
        m = [None] * 2
        acc = [None] * 2
        for j in range(n_tiles):
            rows = slice(j * TQ, (j + 1) * TQ)
            kb = k_ref[rows, :]
            v1 = jnp.concatenate([v_ref[rows, :], ones], axis=1)
            bias = {n_tiles - 1: 1, n_tiles - 2: 0}.get(j)
            for mp in range(2):
                s = _dot_nt(qm[mp], kb)
                if bias is not None:
                    s = s + bias_ref[bias]
                m_cur = jnp.max(s, axis=-1, keepdims=True)
                if j == 0:
                    m_new = jnp.broadcast_to(m_cur, (TQ, V_DIM))
                else:
                    m_new = jnp.maximum(m[mp], m_cur)
                p = jnp.exp2(s - jnp.concatenate([m_new] * (TQ // V_DIM), axis=1))
                pv = _dot(p.astype(BF16), v1)
                if j == 0:
                    acc[mp] = pv
                else:
                    alpha = jnp.exp2(m[mp] - m_new)
                    acc[mp] = jnp.concatenate([alpha, alpha], axis=1) * acc[mp] + pv
                m[mp] = m_new
        o = (acc[0][:, :V_DIM] / acc[0][:, V_DIM:]
             - lam * (acc[1][:, :V_DIM] / acc[1][:, V_DIM:]))
        o_ref[q_rows, :] = (_rms(o, subln_ref[...]) * (1.0 - lam_init)).astype(BF16)

    for n_tiles in range(1, k_ref.shape[0] // TQ + 1):
        attend(n_tiles)


def _attention(lamv, subln, q, k, v, bias_tiles, lam_init):
    B, S, _ = q.shape
    head = pl.BlockSpec((None, S, V_DIM), lambda b, h: (b, 0, h))
    return pl.pallas_call(
        functools.partial(_attn_kernel, lam_init=lam_init),
        grid=(B, N_HEADS),
        in_specs=[_resident((4, HEAD_DIM)), _resident((1, V_DIM)),
                  head, head, head,
                  pl.BlockSpec((None, 2, TQ, TQ), lambda b, h: (h, 0, 0, 0))],
        out_specs=head,
        out_shape=jax.ShapeDtypeStruct((B, S, ATTN_WIDTH), BF16),
        compiler_params=pltpu.CompilerParams(
            dimension_semantics=("parallel", "parallel"), vmem_limit_bytes=VMEM_LIMIT),
        name="diff_attention",
    )(lamv, subln, q, k, v, bias_tiles)


def _ssm_kernel(u_ref, perm_ref, permt_ref, wbr_ref, wbi_ref, ar_ref, ai_ref, cr_ref, ci_ref,
                dskip_ref, wglu_ref, out_ref, bur, bui, xr_s, xi_s):
    @pl.when(pl.program_id(0) == 0)
    def _():
        xr_s[...] = jnp.zeros(xr_s.shape, F32)
        xi_s[...] = jnp.zeros(xi_s.shape, F32)


    def b_project(c):
        u_bt = u_ref[:, c * SCAN_STEPS:(c + 1) * SCAN_STEPS, :]
        u = _dot(perm_ref[...], u_bt.reshape(SCAN_STEPS * SUBLANES, SSM_WIDTH))
        ub = u.astype(BF16)
        for j in range(STATE_LANES // MXU_TILE):
            kt = (j * MXU_TILE // SSM_STATE * SSM_GROUP) // MXU_TILE
            rows = slice(kt * MXU_TILE, (kt + 1) * MXU_TILE)
            cols = slice(j * MXU_TILE, (j + 1) * MXU_TILE)
            bur[c, :, cols] = _dot(ub[:, rows], wbr_ref[rows, cols])
            bui[c, :, cols] = _dot(ub[:, rows], wbi_ref[rows, cols])
        return u

    def scan(c):
        for s in range(STATE_LANES // SCAN_STRIP):
            sl = slice(s * SCAN_STRIP, (s + 1) * SCAN_STRIP)
            a_r = ar_ref[:, sl]
            a_i = ai_ref[:, sl]
            xr = xr_s[:, sl]
            xi = xi_s[:, sl]
            for t in range(SCAN_STEPS):
                step = slice(t * SUBLANES, (t + 1) * SUBLANES)
                xr, xi = (a_r * xr - a_i * xi + bur[c, step, sl],
                          a_r * xi + a_i * xr + bui[c, step, sl])
                bur[c, step, sl] = xr
                bui[c, step, sl] = xi
            xr_s[:, sl] = xr
            xi_s[:, sl] = xi

    def c_project(c, u):
        halves = []
        k_per_half = STATE_LANES // 2
        n_per_half = SSM_WIDTH // 2
        for m in range(2):
            rows = slice(m * k_per_half, (m + 1) * k_per_half)
            cols = slice(m * n_per_half, (m + 1) * n_per_half)
            halves.append(_dot(bur[c, :, rows].astype(BF16), cr_ref[rows, cols])
                          + _dot(bui[c, :, rows].astype(BF16), ci_ref[rows, cols]))
        y = jnp.concatenate(halves, axis=-1) + dskip_ref[...] * u
        s = jax.nn.gelu(y)
        s = s * jax.nn.sigmoid(_dot(s.astype(BF16), wglu_ref[...]))
        s_bt = _dot(permt_ref[...], s.astype(BF16)).astype(BF16)
        out_ref[:, c * SCAN_STEPS:(c + 1) * SCAN_STEPS, :] = s_bt.reshape(
            SUBLANES, SCAN_STEPS, SSM_WIDTH)

    us = [b_project(c) for c in range(SCAN_CHUNKS)]
    for c in range(SCAN_CHUNKS):
        scan(c)
    for c in range(SCAN_CHUNKS):
        c_project(c, us[c])


def _ssm(u, wbr, wbi, ar, ai, cr, ci, dskip, wglu):
    rows = SCAN_STEPS * SUBLANES
    B, S, _ = u.shape
    r = jnp.arange(rows)
    perm = (jnp.arange(rows)[None, :] == ((r % B) * SCAN_STEPS + r // B)[:, None]).astype(BF16)
    blk = pl.BlockSpec((B, SCAN_CHUNKS * SCAN_STEPS, SSM_WIDTH), lambda c: (0, c, 0))
    return pl.pallas_call(
        _ssm_kernel,
        grid=(S // (SCAN_CHUNKS * SCAN_STEPS),),
        in_specs=[blk, _resident(perm.shape), _resident(perm.shape),
                  _resident(wbr.shape), _resident(wbi.shape),
                  _resident(ar.shape), _resident(ai.shape),
                  _resident(cr.shape), _resident(ci.shape),
                  _resident(dskip.shape), _resident(wglu.shape)],
        out_specs=blk,
        out_shape=jax.ShapeDtypeStruct((B, S, SSM_WIDTH), BF16),
        scratch_shapes=[pltpu.VMEM((SCAN_CHUNKS, rows, STATE_LANES), F32),
                        pltpu.VMEM((SCAN_CHUNKS, rows, STATE_LANES), F32),
                        pltpu.VMEM((SUBLANES, STATE_LANES), F32),
                        pltpu.VMEM((SUBLANES, STATE_LANES), F32)],
        compiler_params=pltpu.CompilerParams(
            dimension_semantics=("arbitrary",), vmem_limit_bytes=VMEM_LIMIT),
        name="s5_branch",
    )(u, perm, perm.T, wbr, wbi, ar, ai, cr, ci, dskip, wglu)


def _merge_ffn_kernel(x_ref, o_ref, s_ref, g1_ref, wg_ref, wa_ref, wb_ref, wo_ref,
                      g2_ref, w1_ref, w3_ref, w2_ref, out_ref, acc_s):
    x = x_ref[...]
    h = _rms(x, g1_ref[...]).astype(BF16)
    mixed = jax.nn.sigmoid(_dot(h, wg_ref[:, 0:D_MODEL])) * _dot(o_ref[...], wa_ref[...])
    mixed += jax.nn.sigmoid(_dot(h, wg_ref[:, D_MODEL:])) * _dot(s_ref[...], wb_ref[...])
    x1 = x + _dot(mixed.astype(BF16), wo_ref[...])

    h2 = _rms(x1, g2_ref[...]).astype(BF16)
    acc_s[...] = x1

    for c in range(D_FF // FF_CHUNK):
        cols = slice(c * FF_CHUNK, (c + 1) * FF_CHUNK)
        t = jax.nn.silu(_dot(h2, w1_ref[:, cols])) * _dot(h2, w3_ref[:, cols])
        acc_s[...] += _dot(t.astype(BF16), w2_ref[cols, :])
    out_ref[...] = acc_s[...]


def _merge_ffn(x, o, s2, g1, wg, wa, wb, wo, g2, w1, w3, w2):
    B, S, D = x.shape
    tok = lambda width: pl.BlockSpec((None, TM_PROJ, width), lambda b, i: (b, i, 0))
    return pl.pallas_call(
        _merge_ffn_kernel,
        grid=(B, S // TM_PROJ),
        in_specs=[tok(D), tok(ATTN_WIDTH),
                  tok(SSM_WIDTH),
                  _resident(g1.shape), _resident(wg.shape), _resident(wa.shape),
                  _resident(wb.shape), _resident(wo.shape), _resident(g2.shape),
                  _resident(w1.shape), _resident(w3.shape), _resident(w2.shape)],
        out_specs=tok(D),
        out_shape=jax.ShapeDtypeStruct((B, S, D), F32),
        scratch_shapes=[pltpu.VMEM((TM_PROJ, D), F32)],
        compiler_params=pltpu.CompilerParams(
            dimension_semantics=("parallel", "parallel"), vmem_limit_bytes=VMEM_LIMIT),
        name="merge_ffn",
    )(x, o, s2, g1, wg, wa, wb, wo, g2, w1, w3, w2)


def _t5_bucket(n):
    max_exact = N_BUCKETS // 2
    is_small = n < max_exact
    nf = jnp.maximum(n, 1).astype(F32)
    large = max_exact + (jnp.log(nf / max_exact) / math.log(MAX_DISTANCE / max_exact)
                         * (N_BUCKETS - max_exact)).astype(jnp.int32)
    large = jnp.minimum(large, N_BUCKETS - 1)
    return jnp.where(is_small, n, large)


def _ssm_params(lam_re, lam_im, b_re, b_im, c_re, c_im, log_step):
    step = jnp.exp(log_step.astype(F32))[:, None]
    lr = lam_re.astype(F32)
    li = lam_im.astype(F32)
    decay = jnp.exp(lr * step)
    ab_re = decay * jnp.cos(li * step)
    ab_im = decay * jnp.sin(li * step)
    nr = ab_re - 1.0
    ni = ab_im
    den = lr * lr + li * li
    f_re = (nr * lr + ni * li) / den
    f_im = (ni * lr - nr * li) / den
    br = b_re.astype(F32)
    bi = b_im.astype(F32)
    bb_re = f_re[..., None] * br - f_im[..., None] * bi
    bb_im = f_re[..., None] * bi + f_im[..., None] * br

    def blockdiag(w):
        g, n, m = w.shape
        stacked = w.transpose(0, 2, 1).reshape(g * m, n)
        same = (jnp.arange(g * m)[:, None] // m) == (jnp.arange(g * n)[None, :] // n)
        return jnp.where(same, jnp.tile(stacked, (1, g)), 0.0).astype(BF16)

    in_blockdiag = blockdiag
    out_blockdiag = blockdiag

    bcast = lambda a: jnp.broadcast_to(a.reshape(1, STATE_LANES), (SUBLANES, STATE_LANES))
    return (in_blockdiag(bb_re), in_blockdiag(bb_im), bcast(ab_re), bcast(ab_im),
            out_blockdiag(c_re.astype(F32)), out_blockdiag(-c_im.astype(F32)))


def kernel(x, rel_bias, norm_mix, w_in, q_gain, k_gain, lambda_q1, lambda_k1, lambda_q2, lambda_k2, subln, w_a, lam_re, lam_im, b_re, b_im, c_re, c_im, d_skip, log_step, w_glu, w_b, w_o, norm_ffn, w1, w3, w2):
    B, S, D = x.shape
    depth = w_in.shape[0]
    assert B == SUBLANES and D == D_MODEL

    table = rel_bias.astype(F32)
    dist = (-jnp.arange(MAX_DISTANCE)) % MAX_DISTANCE
    brev = (table[_t5_bucket(dist)] - table[N_BUCKETS - 1][None]).T * LOG2E
    bias_tiles = _bias_tiles(brev[:, None, :])

    seg = (jnp.arange(MXU_TILE)[:, None] // HEAD_DIM
           == jnp.arange(MXU_TILE)[None, :] // HEAD_DIM).astype(BF16)
    n_split = 2 * QK_WIDTH + ATTN_WIDTH + SSM_WIDTH

    for l in range(depth):
        lam_init = 0.8 - 0.6 * math.exp(-0.3 * l)
        w_l = w_in[l].astype(BF16)
        qg = jnp.tile(q_gain[l].astype(F32), QK_WIDTH // HEAD_DIM)[None] * (HEAD_DIM ** -0.5 * LOG2E)
        kg = jnp.tile(k_gain[l].astype(F32), QK_WIDTH // HEAD_DIM)[None]
        q, k, v, u = _inproj(x, norm_mix[l][None], w_l[:, :n_split], seg, qg, kg)

        lamv = jnp.stack([lambda_q1[l], lambda_k1[l], lambda_q2[l], lambda_k2[l]]).astype(F32)
        o = _attention(lamv, subln[l][None].astype(F32), q, k, v, bias_tiles, lam_init)

        wbr, wbi, ar, ai, cr, ci = _ssm_params(lam_re[l], lam_im[l], b_re[l], b_im[l],
                                               c_re[l], c_im[l], log_step[l])
        s2 = _ssm(u, wbr, wbi, ar, ai, cr, ci,
                  d_skip[l][None].astype(F32), w_glu[l].astype(BF16))

        x = _merge_ffn(
            x, o, s2, norm_mix[l][None], w_l[:, n_split:],
            w_a[l].astype(BF16), w_b[l].astype(BF16), w_o[l].astype(BF16),
            norm_ffn[l][None],
            w1[l].astype(BF16), w3[l].astype(BF16), w2[l].astype(BF16))
    return x
```

```python
import functools
import math

import jax
import jax.numpy as jnp
from jax import lax
from jax.experimental import pallas as pl
from jax.experimental.pallas import tpu as pltpu

D_MODEL = 1024
N_HEADS = 4
HEAD_DIM = 64
V_DIM = 2 * HEAD_DIM
QK_WIDTH = N_HEADS * 2 * HEAD_DIM
ATTN_WIDTH = N_HEADS * V_DIM
SSM_WIDTH = D_MODEL // 2
SSM_GROUP = 16
SSM_GROUPS = SSM_WIDTH // SSM_GROUP
SSM_STATE = 64
STATE_LANES = SSM_GROUPS * SSM_STATE
D_FF = 2816
N_BUCKETS = 32
MAX_DISTANCE = 128
EPS = 1e-6
NEG = -1e30

MXU_TILE = 256
SUBLANES = 8

TM_PROJ = 512
TQ = 256
LOG2E = math.log2(math.e)
SCAN_STEPS = 64
SCAN_CHUNKS = 2
SCAN_STRIP = 512
FF_CHUNK = 256
VMEM_LIMIT = 56 * 1024 * 1024

BF16 = jnp.bfloat16
F32 = jnp.float32


def _dot(a, b):
    return jnp.dot(a, b, preferred_element_type=F32)


def _dot_nt(a, b):
    return lax.dot_general(a, b, (((1,), (1,)), ((), ())), preferred_element_type=F32)


def _rms(x, g):
    return x * lax.rsqrt(jnp.mean(x * x, axis=-1, keepdims=True) + EPS) * g


def _resident(shape):
    zeros = (0,) * len(shape)
    return pl.BlockSpec(shape, lambda *_: zeros, pipeline_mode=pl.Buffered(1))


STAGE_ROWS = 128
STAGE_COLS = 1024
STAGE_DEPTH = 8
_STAGE_SCRATCH = [pltpu.VMEM((STAGE_DEPTH, STAGE_ROWS, STAGE_COLS), F32),
                  pltpu.SemaphoreType.DMA((STAGE_DEPTH,))]
_HBM = pl.BlockSpec(memory_space=pl.ANY)


def _stage_bf16(jobs, stage, sems):
    pieces = []
    for src, col0, dst in jobs:
        n_rows, n_cols = dst.shape
        for r0 in range(0, n_rows, STAGE_ROWS):
            for c0 in range(0, n_cols, STAGE_COLS):
                pieces.append((src, col0, dst, r0, c0, min(STAGE_COLS, n_cols - c0)))

    def copy(i):
        src, col0, _, r0, c0, nc = pieces[i]
        slot = i % STAGE_DEPTH
        return pltpu.make_async_copy(
            src.at[pl.ds(r0, STAGE_ROWS), pl.ds(col0 + c0, nc)],
            stage.at[slot, :, pl.ds(0, nc)], sems.at[slot])

    for i in range(min(STAGE_DEPTH, len(pieces))):
        copy(i).start()
    for i, (_, _, dst, r0, c0, nc) in enumerate(pieces):
        copy(i).wait()
        dst[r0:r0 + STAGE_ROWS, c0:c0 + nc] = stage[i % STAGE_DEPTH, :, 0:nc].astype(BF16)
        if i + STAGE_DEPTH < len(pieces):
            copy(i + STAGE_DEPTH).start()


def _bias_tiles_kernel(brev_ref, out_ref):
    sub = MAX_DISTANCE
    r = lax.broadcasted_iota(jnp.int32, (sub, sub), 0)
    c = lax.broadcasted_iota(jnp.int32, (sub, sub), 1)
    y = pltpu.roll(jnp.broadcast_to(brev_ref[...], (sub, sub)), 0, 1, stride=1, stride_axis=0)
    by_diag = {0: jnp.where(c <= r, y, NEG), 1: jnp.where(c > r, y, 0.0)}
    for t, off in enumerate((TQ, 0)):
        for a in range(TQ // sub):
            for b in range(TQ // sub):
                d = a - b + off // sub
                fill = NEG if d < 0 else 0.0
                out_ref[t, a * sub:(a + 1) * sub, b * sub:(b + 1) * sub] = by_diag.get(
                    d, jnp.full((sub, sub), fill, F32))


def _bias_tiles(brev):
    return pl.pallas_call(
        _bias_tiles_kernel,
        grid=(N_HEADS,),
        in_specs=[pl.BlockSpec((None, 1, MAX_DISTANCE), lambda h: (h, 0, 0))],
        out_specs=pl.BlockSpec((None, 2, TQ, TQ), lambda h: (h, 0, 0, 0)),
        out_shape=jax.ShapeDtypeStruct((N_HEADS, 2, TQ, TQ), F32),
        name="bias_tiles",
    )(brev)


def _inproj_kernel(x_ref, g_ref, w_hbm, seg_ref, qg_ref, kg_ref,
                   q_ref, k_ref, v_ref, u_ref, w_ref, stage, sems, *, layer):
    @pl.when((pl.program_id(0) == 0) & (pl.program_id(1) == 0))
    def _():
        _stage_bf16([(w_hbm.at[layer], 0, w_ref)], stage, sems)

    h = _rms(x_ref[...], g_ref[...]).astype(BF16)

    def qk_norm(z, gain):
        z2 = (z * z).astype(BF16)
        ss = jnp.concatenate(
            [_dot(z2[:, t * MXU_TILE:(t + 1) * MXU_TILE], seg_ref[...])
             for t in range(QK_WIDTH // MXU_TILE)], axis=1)
        return z * lax.rsqrt(ss * (1.0 / HEAD_DIM) + EPS) * gain

    zq = _dot(h, w_ref[:, 0:QK_WIDTH])
    q_ref[...] = qk_norm(zq, qg_ref[...]).astype(BF16)
    zk = _dot(h, w_ref[:, QK_WIDTH:2 * QK_WIDTH])
    k_ref[...] = qk_norm(zk, kg_ref[...]).astype(BF16)
    v_ref[...] = _dot(h, w_ref[:, 2 * QK_WIDTH:2 * QK_WIDTH + ATTN_WIDTH]).astype(BF16)
    u_ref[...] = _dot(h, w_ref[:, 2 * QK_WIDTH + ATTN_WIDTH:]).astype(BF16)


def _inproj(x, g, w_in, layer, seg, qg, kg):
    B, S, D = x.shape
    n_cols = 2 * QK_WIDTH + ATTN_WIDTH + SSM_WIDTH
    tok = lambda width: pl.BlockSpec((None, TM_PROJ, width), lambda b, i: (b, i, 0))
    return pl.pallas_call(
        functools.partial(_inproj_kernel, layer=layer),
        grid=(B, S // TM_PROJ),
        in_specs=[tok(D), _resident((1, D)), _HBM,
                  _resident((MXU_TILE, MXU_TILE)),
                  _resident((1, QK_WIDTH)), _resident((1, QK_WIDTH))],
        out_specs=[tok(QK_WIDTH), tok(QK_WIDTH), tok(ATTN_WIDTH), tok(SSM_WIDTH)],
        out_shape=[jax.ShapeDtypeStruct((B, S, QK_WIDTH), BF16),
                   jax.ShapeDtypeStruct((B, S, QK_WIDTH), BF16),
                   jax.ShapeDtypeStruct((B, S, ATTN_WIDTH), BF16),
                   jax.ShapeDtypeStruct((B, S, SSM_WIDTH), BF16)],
        scratch_shapes=[pltpu.VMEM((D, n_cols), BF16)] + _STAGE_SCRATCH,
        compiler_params=pltpu.CompilerParams(
            dimension_semantics=("arbitrary", "arbitrary"), vmem_limit_bytes=VMEM_LIMIT),
        name="inproj",
    )(x, g, w_in, seg, qg, kg)


def _attn_kernel(lamv_ref, subln_ref, q_ref, k_ref, v_ref, bias_ref, o_ref,
                 *, lam_init):
    lane = lax.broadcasted_iota(jnp.int32, (TQ, V_DIM), 1)
    ones = jnp.ones((TQ, V_DIM), BF16)
    lv = lamv_ref[...]
    lam = (jnp.exp(jnp.sum(lv[0:1] * lv[1:2], axis=-1, keepdims=True))
           - jnp.exp(jnp.sum(lv[2:3] * lv[3:4], axis=-1, keepdims=True)) + lam_init)

    def attend(n_tiles):
        q_rows = slice((n_tiles - 1) * TQ, n_tiles * TQ)
        q = q_ref[q_rows, :]
        zero = jnp.zeros_like(q)
        qm = (jnp.where(lane < HEAD_DIM, q, zero), jnp.where(lane >= HEAD_DIM, q, zero))
        m = [None] * 2
        acc = [None] * 2
        for j in range(n_tiles):
            rows = slice(j * TQ, (j + 1) * TQ)
            kb = k_ref[rows, :]
            v1 = jnp.concatenate([v_ref[rows, :], ones], axis=1)
            bias = {n_tiles - 1: 1, n_tiles - 2: 0}.get(j)
            for mp in range(2):
                s = _dot_nt(qm[mp], kb)
                if bias is not None:
                    s = s + bias_ref[bias]
                m_cur = jnp.max(s, axis=-1, keepdims=True)
                if j == 0:
                    m_new = jnp.broadcast_to(m_cur, (TQ, V_DIM))
                else:
                    m_new = jnp.maximum(m[mp], m_cur)
                p = jnp.exp2(s - jnp.concatenate([m_new] * (TQ // V_DIM), axis=1))
                pv = _dot(p.astype(BF16), v1)
                if j == 0:
                    acc[mp] = pv
                else:
                    alpha = jnp.exp2(m[mp] - m_new)
                    acc[mp] = jnp.concatenate([alpha, alpha], axis=1) * acc[mp] + pv
                m[mp] = m_new
        o = (acc[0][:, :V_DIM] / acc[0][:, V_DIM:]
             - lam * (acc[1][:, :V_DIM] / acc[1][:, V_DIM:]))
        o_ref[q_rows, :] = (_rms(o, subln_ref[...]) * (1.0 - lam_init)).astype(BF16)

    for n_tiles in range(1, k_ref.shape[0] // TQ + 1):
        attend(n_tiles)


def _attention(lamv, subln, q, k, v, bias_tiles, lam_init):
    B, S, _ = q.shape
    head = pl.BlockSpec((None, S, V_DIM), lambda b, h: (b, 0, h))
    return pl.pallas_call(
        functools.partial(_attn_kernel, lam_init=lam_init),
        grid=(B, N_HEADS),
        in_specs=[_resident((4, HEAD_DIM)), _resident((1, V_DIM)),
                  head, head, head,
                  pl.BlockSpec((None, 2, TQ, TQ), lambda b, h: (h, 0, 0, 0))],
        out_specs=head,
        out_shape=jax.ShapeDtypeStruct((B, S, ATTN_WIDTH), BF16),
        compiler_params=pltpu.CompilerParams(
            dimension_semantics=("parallel", "parallel"), vmem_limit_bytes=VMEM_LIMIT),
        name="diff_attention",
    )(lamv, subln, q, k, v, bias_tiles)


def _ssm_kernel(u_ref, perm_ref, permt_ref, wbr_ref, wbi_ref, ar_ref, ai_ref, cr_ref, ci_ref,
                dskip_ref, wglu_ref, out_ref, bur, bui, xr_s, xi_s):
    @pl.when(pl.program_id(0) == 0)
    def _():
        xr_s[...] = jnp.zeros(xr_s.shape, F32)
        xi_s[...] = jnp.zeros(xi_s.shape, F32)


    def b_project(c):
        u_bt = u_ref[:, c * SCAN_STEPS:(c + 1) * SCAN_STEPS, :]
        u = _dot(perm_ref[...], u_bt.reshape(SCAN_STEPS * SUBLANES, SSM_WIDTH))
        ub = u.astype(BF16)
        for j in range(STATE_LANES // MXU_TILE):
            kt = (j * MXU_TILE // SSM_STATE * SSM_GROUP) // MXU_TILE
            rows = slice(kt * MXU_TILE, (kt + 1) * MXU_TILE)
            cols = slice(j * MXU_TILE, (j + 1) * MXU_TILE)
            bur[c, :, cols] = _dot(ub[:, rows], wbr_ref[rows, cols])
            bui[c, :, cols] = _dot(ub[:, rows], wbi_ref[rows, cols])
        return u

    def scan(c):
        for s in range(STATE_LANES // SCAN_STRIP):
            sl = slice(s * SCAN_STRIP, (s + 1) * SCAN_STRIP)
            a_r = ar_ref[:, sl]
            a_i = ai_ref[:, sl]
            xr = xr_s[:, sl]
            xi = xi_s[:, sl]
            for t in range(SCAN_STEPS):
                step = slice(t * SUBLANES, (t + 1) * SUBLANES)
                xr, xi = (a_r * xr - a_i * xi + bur[c, step, sl],
                          a_r * xi + a_i * xr + bui[c, step, sl])
                bur[c, step, sl] = xr
                bui[c, step, sl] = xi
            xr_s[:, sl] = xr
            xi_s[:, sl] = xi

    def c_project(c, u):
        halves = []
        k_per_half = STATE_LANES // 2
        n_per_half = SSM_WIDTH // 2
        for m in range(2):
            rows = slice(m * k_per_half, (m + 1) * k_per_half)
            cols = slice(m * n_per_half, (m + 1) * n_per_half)
            halves.append(_dot(bur[c, :, rows].astype(BF16), cr_ref[rows, cols])
                          + _dot(bui[c, :, rows].astype(BF16), ci_ref[rows, cols]))
        y = jnp.concatenate(halves, axis=-1) + dskip_ref[...] * u
        s = jax.nn.gelu(y)
        s = s * jax.nn.sigmoid(_dot(s.astype(BF16), wglu_ref[...]))
        s_bt = _dot(permt_ref[...], s.astype(BF16)).astype(BF16)
        out_ref[:, c * SCAN_STEPS:(c + 1) * SCAN_STEPS, :] = s_bt.reshape(
            SUBLANES, SCAN_STEPS, SSM_WIDTH)

    us = [b_project(c) for c in range(SCAN_CHUNKS)]
    for c in range(SCAN_CHUNKS):
        scan(c)
    for c in range(SCAN_CHUNKS):
        c_project(c, us[c])


def _ssm(u, wbr, wbi, ar, ai, cr, ci, dskip, wglu):
    rows = SCAN_STEPS * SUBLANES
    B, S, _ = u.shape
    r = jnp.arange(rows)
    perm = (jnp.arange(rows)[None, :] == ((r % B) * SCAN_STEPS + r // B)[:, None]).astype(BF16)
    blk = pl.BlockSpec((B, SCAN_CHUNKS * SCAN_STEPS, SSM_WIDTH), lambda c: (0, c, 0))
    return pl.pallas_call(
        _ssm_kernel,
        grid=(S // (SCAN_CHUNKS * SCAN_STEPS),),
        in_specs=[blk, _resident(perm.shape), _resident(perm.shape),
                  _resident(wbr.shape), _resident(wbi.shape),
                  _resident(ar.shape), _resident(ai.shape),
                  _resident(cr.shape), _resident(ci.shape),
                  _resident(dskip.shape), _resident(wglu.shape)],
        out_specs=blk,
        out_shape=jax.ShapeDtypeStruct((B, S, SSM_WIDTH), BF16),
        scratch_shapes=[pltpu.VMEM((SCAN_CHUNKS, rows, STATE_LANES), F32),
                        pltpu.VMEM((SCAN_CHUNKS, rows, STATE_LANES), F32),
                        pltpu.VMEM((SUBLANES, STATE_LANES), F32),
                        pltpu.VMEM((SUBLANES, STATE_LANES), F32)],
        compiler_params=pltpu.CompilerParams(
            dimension_semantics=("arbitrary",), vmem_limit_bytes=VMEM_LIMIT),
        name="s5_branch",
    )(u, perm, perm.T, wbr, wbi, ar, ai, cr, ci, dskip, wglu)


def _merge_ffn_kernel(x_ref, o_ref, s_ref, g1_ref, g2_ref,
                      w_in_hbm, wa_hbm, wb_hbm, wo_hbm, w1_hbm, w3_hbm, w2_hbm, out_ref,
                      wg_ref, wa_ref, wb_ref, wo_ref, w1_ref, w3_ref, w2_ref,
                      acc_s, stage, sems, *, layer):
    @pl.when((pl.program_id(0) == 0) & (pl.program_id(1) == 0))
    def _():
        gate_col0 = w_in_hbm.shape[2] - wg_ref.shape[1]
        _stage_bf16([(w_in_hbm.at[layer], gate_col0, wg_ref), (wa_hbm.at[layer], 0, wa_ref),
                     (wb_hbm.at[layer], 0, wb_ref), (wo_hbm.at[layer], 0, wo_ref),
                     (w1_hbm.at[layer], 0, w1_ref), (w3_hbm.at[layer], 0, w3_ref),
                     (w2_hbm.at[layer], 0, w2_ref)], stage, sems)

    x = x_ref[...]
    h = _rms(x, g1_ref[...]).astype(BF16)
    mixed = jax.nn.sigmoid(_dot(h, wg_ref[:, 0:D_MODEL])) * _dot(o_ref[...], wa_ref[...])
    mixed += jax.nn.sigmoid(_dot(h, wg_ref[:, D_MODEL:])) * _dot(s_ref[...], wb_ref[...])
    x1 = x + _dot(mixed.astype(BF16), wo_ref[...])

    h2 = _rms(x1, g2_ref[...]).astype(BF16)
    acc_s[...] = x1

    for c in range(D_FF // FF_CHUNK):
        cols = slice(c * FF_CHUNK, (c + 1) * FF_CHUNK)
        t = jax.nn.silu(_dot(h2, w1_ref[:, cols])) * _dot(h2, w3_ref[:, cols])
        acc_s[...] += _dot(t.astype(BF16), w2_ref[cols, :])
    out_ref[...] = acc_s[...]


def _merge_ffn(x, o, s2, g1, g2, w_in, w_a, w_b, w_o, w1, w3, w2, layer):
    B, S, D = x.shape
    tok = lambda width: pl.BlockSpec((None, TM_PROJ, width), lambda b, i: (b, i, 0))
    bf16_weight = lambda shape: pltpu.VMEM(shape, BF16)
    return pl.pallas_call(
        functools.partial(_merge_ffn_kernel, layer=layer),
        grid=(B, S // TM_PROJ),
        in_specs=[tok(D), tok(ATTN_WIDTH), tok(SSM_WIDTH),
                  _resident(g1.shape), _resident(g2.shape)] + [_HBM] * 7,
        out_specs=tok(D),
        out_shape=jax.ShapeDtypeStruct((B, S, D), F32),
        scratch_shapes=[bf16_weight((D, 2 * D)), bf16_weight(w_a.shape[1:]),
                        bf16_weight(w_b.shape[1:]), bf16_weight(w_o.shape[1:]),
                        bf16_weight(w1.shape[1:]), bf16_weight(w3.shape[1:]),
                        bf16_weight(w2.shape[1:]),
                        pltpu.VMEM((TM_PROJ, D), F32)] + _STAGE_SCRATCH,
        compiler_params=pltpu.CompilerParams(
            dimension_semantics=("arbitrary", "arbitrary"), vmem_limit_bytes=VMEM_LIMIT),
        name="merge_ffn",
    )(x, o, s2, g1, g2, w_in, w_a, w_b, w_o, w1, w3, w2)


def _t5_bucket(n):
    max_exact = N_BUCKETS // 2
    is_small = n < max_exact
    nf = jnp.maximum(n, 1).astype(F32)
    large = max_exact + (jnp.log(nf / max_exact) / math.log(MAX_DISTANCE / max_exact)
                         * (N_BUCKETS - max_exact)).astype(jnp.int32)
    large = jnp.minimum(large, N_BUCKETS - 1)
    return jnp.where(is_small, n, large)


def _ssm_params(lam_re, lam_im, b_re, b_im, c_re, c_im, log_step):
    step = jnp.exp(log_step.astype(F32))[:, None]
    lr = lam_re.astype(F32)
    li = lam_im.astype(F32)
    decay = jnp.exp(lr * step)
    ab_re = decay * jnp.cos(li * step)
    ab_im = decay * jnp.sin(li * step)
    nr = ab_re - 1.0
    ni = ab_im
    den = lr * lr + li * li
    f_re = (nr * lr + ni * li) / den
    f_im = (ni * lr - nr * li) / den
    br = b_re.astype(F32)
    bi = b_im.astype(F32)
    bb_re = f_re[..., None] * br - f_im[..., None] * bi
    bb_im = f_re[..., None] * bi + f_im[..., None] * br

    def blockdiag(w):
        g, n, m = w.shape
        stacked = w.transpose(0, 2, 1).reshape(g * m, n)
        same = (jnp.arange(g * m)[:, None] // m) == (jnp.arange(g * n)[None, :] // n)
        return jnp.where(same, jnp.tile(stacked, (1, g)), 0.0).astype(BF16)

    in_blockdiag = blockdiag
    out_blockdiag = blockdiag

    bcast = lambda a: jnp.broadcast_to(a.reshape(1, STATE_LANES), (SUBLANES, STATE_LANES))
    return (in_blockdiag(bb_re), in_blockdiag(bb_im), bcast(ab_re), bcast(ab_im),
            out_blockdiag(c_re.astype(F32)), out_blockdiag(-c_im.astype(F32)))


def kernel(x, rel_bias, norm_mix, w_in, q_gain, k_gain, lambda_q1, lambda_k1, lambda_q2, lambda_k2, subln, w_a, lam_re, lam_im, b_re, b_im, c_re, c_im, d_skip, log_step, w_glu, w_b, w_o, norm_ffn, w1, w3, w2):
    B, S, D = x.shape
    depth = w_in.shape[0]
    assert B == SUBLANES and D == D_MODEL

    table = rel_bias.astype(F32)
    dist = (-jnp.arange(MAX_DISTANCE)) % MAX_DISTANCE
    brev = (table[_t5_bucket(dist)] - table[N_BUCKETS - 1][None]).T * LOG2E
    bias_tiles = _bias_tiles(brev[:, None, :])

    seg = (jnp.arange(MXU_TILE)[:, None] // HEAD_DIM
           == jnp.arange(MXU_TILE)[None, :] // HEAD_DIM).astype(BF16)

    for l in range(depth):
        lam_init = 0.8 - 0.6 * math.exp(-0.3 * l)
        qg = jnp.tile(q_gain[l].astype(F32), QK_WIDTH // HEAD_DIM)[None] * (HEAD_DIM ** -0.5 * LOG2E)
        kg = jnp.tile(k_gain[l].astype(F32), QK_WIDTH // HEAD_DIM)[None]
        q, k, v, u = _inproj(x, norm_mix[l][None], w_in, l, seg, qg, kg)

        lamv = jnp.stack([lambda_q1[l], lambda_k1[l], lambda_q2[l], lambda_k2[l]]).astype(F32)
        o = _attention(lamv, subln[l][None].astype(F32), q, k, v, bias_tiles, lam_init)

        wbr, wbi, ar, ai, cr, ci = _ssm_params(lam_re[l], lam_im[l], b_re[l], b_im[l],
                                               c_re[l], c_im[l], log_step[l])
        s2 = _ssm(u, wbr, wbi, ar, ai, cr, ci,
                  d_skip[l][None].astype(F32), w_glu[l].astype(BF16))

        x = _merge_ffn(x, o, s2, norm_mix[l][None], norm_ffn[l][None],
                       w_in, w_a, w_b, w_o, w1, w3, w2, l)
    return x
```

```python
import functools
import math

import jax
import jax.numpy as jnp
from jax import lax
from jax.experimental import pallas as pl
from jax.experimental.pallas import tpu as pltpu

D_MODEL = 1024
N_HEADS = 4
HEAD_DIM = 64
V_DIM = 2 * HEAD_DIM
QK_WIDTH = N_HEADS * 2 * HEAD_DIM
ATTN_WIDTH = N_HEADS * V_DIM
SSM_WIDTH = D_MODEL // 2
SSM_GROUP = 16
SSM_GROUPS = SSM_WIDTH // SSM_GROUP
SSM_STATE = 64
STATE_LANES = SSM_GROUPS * SSM_STATE
D_FF = 2816
N_BUCKETS = 32
MAX_DISTANCE = 128
EPS = 1e-6
NEG = -1e30

MXU_TILE = 256
SUBLANES = 8

TM_PROJ = 512
TM_FFN = 1024
TQ = 256
LOG2E = math.log2(math.e)
SCAN_STEPS = 64
SCAN_CHUNKS = 2
SCAN_STRIP = 512
FF_CHUNK = 256
VMEM_LIMIT = 62 * 1024 * 1024

BF16 = jnp.bfloat16
F32 = jnp.float32


def _dot(a, b):
    return jnp.dot(a, b, preferred_element_type=F32)


def _dot_nt(a, b):
    return lax.dot_general(a, b, (((1,), (1,)), ((), ())), preferred_element_type=F32)


def _rms(x, g):
    return x * lax.rsqrt(jnp.mean(x * x, axis=-1, keepdims=True) + EPS) * g


def _resident(shape):
    zeros = (0,) * len(shape)
    return pl.BlockSpec(shape, lambda *_: zeros, pipeline_mode=pl.Buffered(1))


STAGE_ROWS = 128
STAGE_COLS = 1024
STAGE_DEPTH = 8
_STAGE_SCRATCH = [pltpu.VMEM((STAGE_DEPTH, STAGE_ROWS, STAGE_COLS), F32),
                  pltpu.SemaphoreType.DMA((STAGE_DEPTH,))]
_HBM = pl.BlockSpec(memory_space=pl.ANY)


def _stage_bf16(jobs, stage, sems):
    pieces = []
    for src, col0, dst in jobs:
        n_rows, n_cols = dst.shape
        for r0 in range(0, n_rows, STAGE_ROWS):
            for c0 in range(0, n_cols, STAGE_COLS):
                pieces.append((src, col0, dst, r0, c0, min(STAGE_COLS, n_cols - c0)))

    def copy(i):
        src, col0, _, r0, c0, nc = pieces[i]
        slot = i % STAGE_DEPTH
        return pltpu.make_async_copy(
            src.at[pl.ds(r0, STAGE_ROWS), pl.ds(col0 + c0, nc)],
            stage.at[slot, :, pl.ds(0, nc)], sems.at[slot])

    for i in range(min(STAGE_DEPTH, len(pieces))):
        copy(i).start()
    for i, (_, _, dst, r0, c0, nc) in enumerate(pieces):
        copy(i).wait()
        dst[r0:r0 + STAGE_ROWS, c0:c0 + nc] = stage[i % STAGE_DEPTH, :, 0:nc].astype(BF16)
        if i + STAGE_DEPTH < len(pieces):
            copy(i + STAGE_DEPTH).start()


def _bias_tiles_kernel(brev_ref, out_ref):
    sub = MAX_DISTANCE
    r = lax.broadcasted_iota(jnp.int32, (sub, sub), 0)
    c = lax.broadcasted_iota(jnp.int32, (sub, sub), 1)
    y = pltpu.roll(jnp.broadcast_to(brev_ref[...], (sub, sub)), 0, 1, stride=1, stride_axis=0)
    by_diag = {0: jnp.where(c <= r, y, NEG), 1: jnp.where(c > r, y, 0.0)}
    for t, off in enumerate((TQ, 0)):
        for a in range(TQ // sub):
            for b in range(TQ // sub):
                d = a - b + off // sub
                fill = NEG if d < 0 else 0.0
                out_ref[t, a * sub:(a + 1) * sub, b * sub:(b + 1) * sub] = by_diag.get(
                    d, jnp.full((sub, sub), fill, F32))


def _bias_tiles(brev):
    return pl.pallas_call(
        _bias_tiles_kernel,
        grid=(N_HEADS,),
        in_specs=[pl.BlockSpec((None, 1, MAX_DISTANCE), lambda h: (h, 0, 0))],
        out_specs=pl.BlockSpec((None, 2, TQ, TQ), lambda h: (h, 0, 0, 0)),
        out_shape=jax.ShapeDtypeStruct((N_HEADS, 2, TQ, TQ), F32),
        name="bias_tiles",
    )(brev)


def _inproj_kernel(x_ref, g_ref, w_hbm, seg_ref, qg_ref, kg_ref,
                   q_ref, k_ref, v_ref, u_ref, w_ref, stage, sems, *, layer):
    @pl.when((pl.program_id(0) == 0) & (pl.program_id(1) == 0))
    def _():
        _stage_bf16([(w_hbm.at[layer], 0, w_ref)], stage, sems)

    h = _rms(x_ref[...], g_ref[...]).astype(BF16)

    def qk_norm(z, gain):
        z2 = (z * z).astype(BF16)
        ss = jnp.concatenate(
            [_dot(z2[:, t * MXU_TILE:(t + 1) * MXU_TILE], seg_ref[...])
             for t in range(QK_WIDTH // MXU_TILE)], axis=1)
        return z * lax.rsqrt(ss * (1.0 / HEAD_DIM) + EPS) * gain

    zq = _dot(h, w_ref[:, 0:QK_WIDTH])
    q_ref[...] = qk_norm(zq, qg_ref[...]).astype(BF16)
    zk = _dot(h, w_ref[:, QK_WIDTH:2 * QK_WIDTH])
    k_ref[...] = qk_norm(zk, kg_ref[...]).astype(BF16)
    v_ref[...] = _dot(h, w_ref[:, 2 * QK_WIDTH:2 * QK_WIDTH + ATTN_WIDTH]).astype(BF16)
    u_ref[...] = _dot(h, w_ref[:, 2 * QK_WIDTH + ATTN_WIDTH:]).astype(BF16)


def _inproj(x, g, w_in, layer, seg, qg, kg):
    B, S, D = x.shape
    n_cols = 2 * QK_WIDTH + ATTN_WIDTH + SSM_WIDTH
    tok = lambda width: pl.BlockSpec((None, TM_PROJ, width), lambda b, i: (b, i, 0))
    return pl.pallas_call(
        functools.partial(_inproj_kernel, layer=layer),
        grid=(B, S // TM_PROJ),
        in_specs=[tok(D), _resident((1, D)), _HBM,
                  _resident((MXU_TILE, MXU_TILE)),
                  _resident((1, QK_WIDTH)), _resident((1, QK_WIDTH))],
        out_specs=[tok(QK_WIDTH), tok(QK_WIDTH), tok(ATTN_WIDTH), tok(SSM_WIDTH)],
        out_shape=[jax.ShapeDtypeStruct((B, S, QK_WIDTH), BF16),
                   jax.ShapeDtypeStruct((B, S, QK_WIDTH), BF16),
                   jax.ShapeDtypeStruct((B, S, ATTN_WIDTH), BF16),
                   jax.ShapeDtypeStruct((B, S, SSM_WIDTH), BF16)],
        scratch_shapes=[pltpu.VMEM((D, n_cols), BF16)] + _STAGE_SCRATCH,
        compiler_params=pltpu.CompilerParams(
            dimension_semantics=("arbitrary", "arbitrary"), vmem_limit_bytes=VMEM_LIMIT),
        name="inproj",
    )(x, g, w_in, seg, qg, kg)


def _attn_kernel(lamv_ref, subln_ref, q_ref, k_ref, v_ref, bias_ref, o_ref,
                 *, lam_init):
    lane = lax.broadcasted_iota(jnp.int32, (TQ, V_DIM), 1)
    ones = jnp.ones((TQ, V_DIM), BF16)
    lv = lamv_ref[...]
    lam = (jnp.exp(jnp.sum(lv[0:1] * lv[1:2], axis=-1, keepdims=True))
           - jnp.exp(jnp.sum(lv[2:3] * lv[3:4], axis=-1, keepdims=True)) + lam_init)

    def attend(n_tiles):
        q_rows = slice((n_tiles - 1) * TQ, n_tiles * TQ)
        q = q_ref[q_rows, :]
        zero = jnp.zeros_like(q)
        qm = (jnp.where(lane < HEAD_DIM, q, zero), jnp.where(lane >= HEAD_DIM, q, zero))
        m = [None] * 2
        acc = [None] * 2
        for j in range(n_tiles):
            rows = slice(j * TQ, (j + 1) * TQ)
            kb = k_ref[rows, :]
            v1 = jnp.concatenate([v_ref[rows, :], ones], axis=1)
            bias = {n_tiles - 1: 1, n_tiles - 2: 0}.get(j)
            for mp in range(2):
                s = _dot_nt(qm[mp], kb)
                if bias is not None:
                    s = s + bias_ref[bias]
                m_cur = jnp.max(s, axis=-1, keepdims=True)
                if j == 0:
                    m_new = jnp.broadcast_to(m_cur, (TQ, V_DIM))
                else:
                    m_new = jnp.maximum(m[mp], m_cur)
                p = jnp.exp2(s - jnp.concatenate([m_new] * (TQ // V_DIM), axis=1))
                pv = _dot(p.astype(BF16), v1)
                if j == 0:
                    acc[mp] = pv
                else:
                    alpha = jnp.exp2(m[mp] - m_new)
                    acc[mp] = jnp.concatenate([alpha, alpha], axis=1) * acc[mp] + pv
                m[mp] = m_new
        o = (acc[0][:, :V_DIM] / acc[0][:, V_DIM:]
             - lam * (acc[1][:, :V_DIM] / acc[1][:, V_DIM:]))
        o_ref[q_rows, :] = (_rms(o, subln_ref[...]) * (1.0 - lam_init)).astype(BF16)

    for n_tiles in range(1, k_ref.shape[0] // TQ + 1):
        attend(n_tiles)


def _attention(lamv, subln, q, k, v, bias_tiles, lam_init):
    B, S, _ = q.shape
    head = pl.BlockSpec((None, S, V_DIM), lambda b, h: (b, 0, h))
    return pl.pallas_call(
        functools.partial(_attn_kernel, lam_init=lam_init),
        grid=(B, N_HEADS),
        in_specs=[_resident((4, HEAD_DIM)), _resident((1, V_DIM)),
                  head, head, head,
                  pl.BlockSpec((None, 2, TQ, TQ), lambda b, h: (h, 0, 0, 0))],
        out_specs=head,
        out_shape=jax.ShapeDtypeStruct((B, S, ATTN_WIDTH), BF16),
        compiler_params=pltpu.CompilerParams(
            dimension_semantics=("parallel", "parallel"), vmem_limit_bytes=VMEM_LIMIT),
        name="diff_attention",
    )(lamv, subln, q, k, v, bias_tiles)


def _ssm_kernel(u_ref, perm_ref, permt_ref, wbr_ref, wbi_ref, ar_ref, ai_ref, cr_ref, ci_ref,
                dskip_ref, wglu_ref, out_ref, bur, bui, xr_s, xi_s):
    @pl.when(pl.program_id(0) == 0)
    def _():
        xr_s[...] = jnp.zeros(xr_s.shape, F32)
        xi_s[...] = jnp.zeros(xi_s.shape, F32)


    def b_project(c):
        u_bt = u_ref[:, c * SCAN_STEPS:(c + 1) * SCAN_STEPS, :]
        u = _dot(perm_ref[...], u_bt.reshape(SCAN_STEPS * SUBLANES, SSM_WIDTH))
        ub = u.astype(BF16)
        for j in range(STATE_LANES // MXU_TILE):
            kt = (j * MXU_TILE // SSM_STATE * SSM_GROUP) // MXU_TILE
            rows = slice(kt * MXU_TILE, (kt + 1) * MXU_TILE)
            cols = slice(j * MXU_TILE, (j + 1) * MXU_TILE)
            bur[c, :, cols] = _dot(ub[:, rows], wbr_ref[rows, cols])
            bui[c, :, cols] = _dot(ub[:, rows], wbi_ref[rows, cols])
        return u

    def scan(c):
        for s in range(STATE_LANES // SCAN_STRIP):
            sl = slice(s * SCAN_STRIP, (s + 1) * SCAN_STRIP)
            a_r = ar_ref[:, sl]
            a_i = ai_ref[:, sl]
            xr = xr_s[:, sl]
            xi = xi_s[:, sl]
            for t in range(SCAN_STEPS):
                step = slice(t * SUBLANES, (t + 1) * SUBLANES)
                xr, xi = (a_r * xr - a_i * xi + bur[c, step, sl],
                          a_r * xi + a_i * xr + bui[c, step, sl])
                bur[c, step, sl] = xr
                bui[c, step, sl] = xi
            xr_s[:, sl] = xr
            xi_s[:, sl] = xi

    def c_project(c, u):
        halves = []
        k_per_half = STATE_LANES // 2
        n_per_half = SSM_WIDTH // 2
        for m in range(2):
            rows = slice(m * k_per_half, (m + 1) * k_per_half)
            cols = slice(m * n_per_half, (m + 1) * n_per_half)
            halves.append(_dot(bur[c, :, rows].astype(BF16), cr_ref[rows, cols])
                          + _dot(bui[c, :, rows].astype(BF16), ci_ref[rows, cols]))
        y = jnp.concatenate(halves, axis=-1) + dskip_ref[...] * u
        s = jax.nn.gelu(y)
        s = s * jax.nn.sigmoid(_dot(s.astype(BF16), wglu_ref[...]))
        s_bt = _dot(permt_ref[...], s.astype(BF16)).astype(BF16)
        out_ref[:, c * SCAN_STEPS:(c + 1) * SCAN_STEPS, :] = s_bt.reshape(
            SUBLANES, SCAN_STEPS, SSM_WIDTH)

    us = [b_project(c) for c in range(SCAN_CHUNKS)]
    for c in range(SCAN_CHUNKS):
        scan(c)
    for c in range(SCAN_CHUNKS):
        c_project(c, us[c])


def _ssm(u, wbr, wbi, ar, ai, cr, ci, dskip, wglu):
    rows = SCAN_STEPS * SUBLANES
    B, S, _ = u.shape
    r = jnp.arange(rows)
    perm = (jnp.arange(rows)[None, :] == ((r % B) * SCAN_STEPS + r // B)[:, None]).astype(BF16)
    blk = pl.BlockSpec((B, SCAN_CHUNKS * SCAN_STEPS, SSM_WIDTH), lambda c: (0, c, 0))
    return pl.pallas_call(
        _ssm_kernel,
        grid=(S // (SCAN_CHUNKS * SCAN_STEPS),),
        in_specs=[blk, _resident(perm.shape), _resident(perm.shape),
                  _resident(wbr.shape), _resident(wbi.shape),
                  _resident(ar.shape), _resident(ai.shape),
                  _resident(cr.shape), _resident(ci.shape),
                  _resident(dskip.shape), _resident(wglu.shape)],
        out_specs=blk,
        out_shape=jax.ShapeDtypeStruct((B, S, SSM_WIDTH), BF16),
        scratch_shapes=[pltpu.VMEM((SCAN_CHUNKS, rows, STATE_LANES), F32),
                        pltpu.VMEM((SCAN_CHUNKS, rows, STATE_LANES), F32),
                        pltpu.VMEM((SUBLANES, STATE_LANES), F32),
                        pltpu.VMEM((SUBLANES, STATE_LANES), F32)],
        compiler_params=pltpu.CompilerParams(
            dimension_semantics=("arbitrary",), vmem_limit_bytes=VMEM_LIMIT),
        name="s5_branch",
    )(u, perm, perm.T, wbr, wbi, ar, ai, cr, ci, dskip, wglu)


def _merge_ffn_kernel(x_ref, o_ref, s_ref, g1_ref, g2_ref,
                      w_in_hbm, wa_hbm, wb_hbm, wo_hbm, w1_hbm, w3_hbm, w2_hbm, out_ref,
                      wg_ref, wa_ref, wb_ref, wo_ref, w1_ref, w3_ref, w2_ref,
                      acc_s, stage, sems, *, layer):
    @pl.when((pl.program_id(0) == 0) & (pl.program_id(1) == 0))
    def _():
        gate_col0 = w_in_hbm.shape[2] - wg_ref.shape[1]
        _stage_bf16([(w_in_hbm.at[layer], gate_col0, wg_ref), (wa_hbm.at[layer], 0, wa_ref),
                     (wb_hbm.at[layer], 0, wb_ref), (wo_hbm.at[layer], 0, wo_ref),
                     (w1_hbm.at[layer], 0, w1_ref), (w3_hbm.at[layer], 0, w3_ref),
                     (w2_hbm.at[layer], 0, w2_ref)], stage, sems)

    x = x_ref[...]
    h = _rms(x, g1_ref[...]).astype(BF16)
    mixed = jax.nn.sigmoid(_dot(h, wg_ref[:, 0:D_MODEL])) * _dot(o_ref[...], wa_ref[...])
    mixed += jax.nn.sigmoid(_dot(h, wg_ref[:, D_MODEL:])) * _dot(s_ref[...], wb_ref[...])
    x1 = x + _dot(mixed.astype(BF16), wo_ref[...])

    h2 = _rms(x1, g2_ref[...]).astype(BF16)
    acc_s[...] = x1

    for c in range(D_FF // FF_CHUNK):
        cols = slice(c * FF_CHUNK, (c + 1) * FF_CHUNK)
        t = jax.nn.silu(_dot(h2, w1_ref[:, cols])) * _dot(h2, w3_ref[:, cols])
        acc_s[...] += _dot(t.astype(BF16), w2_ref[cols, :])
    out_ref[...] = acc_s[...]


def _merge_ffn(x, o, s2, g1, g2, w_in, w_a, w_b, w_o, w1, w3, w2, layer):
    B, S, D = x.shape
    tok = lambda width: pl.BlockSpec((None, TM_FFN, width), lambda b, i: (b, i, 0))
    bf16_weight = lambda shape: pltpu.VMEM(shape, BF16)
    return pl.pallas_call(
        functools.partial(_merge_ffn_kernel, layer=layer),
        grid=(B, S // TM_FFN),
        in_specs=[tok(D), tok(ATTN_WIDTH), tok(SSM_WIDTH),
                  _resident(g1.shape), _resident(g2.shape)] + [_HBM] * 7,
        out_specs=tok(D),
        out_shape=jax.ShapeDtypeStruct((B, S, D), F32),
        scratch_shapes=[bf16_weight((D, 2 * D)), bf16_weight(w_a.shape[1:]),
                        bf16_weight(w_b.shape[1:]), bf16_weight(w_o.shape[1:]),
                        bf16_weight(w1.shape[1:]), bf16_weight(w3.shape[1:]),
                        bf16_weight(w2.shape[1:]),
                        pltpu.VMEM((TM_FFN, D), F32)] + _STAGE_SCRATCH,
        compiler_params=pltpu.CompilerParams(
            dimension_semantics=("arbitrary", "arbitrary"), vmem_limit_bytes=VMEM_LIMIT),
        name="merge_ffn",
    )(x, o, s2, g1, g2, w_in, w_a, w_b, w_o, w1, w3, w2)


def _t5_bucket(n):
    max_exact = N_BUCKETS // 2
    is_small = n < max_exact
    nf = jnp.maximum(n, 1).astype(F32)
    large = max_exact + (jnp.log(nf / max_exact) / math.log(MAX_DISTANCE / max_exact)
                         * (N_BUCKETS - max_exact)).astype(jnp.int32)
    large = jnp.minimum(large, N_BUCKETS - 1)
    return jnp.where(is_small, n, large)


def _ssm_params(lam_re, lam_im, b_re, b_im, c_re, c_im, log_step):
    step = jnp.exp(log_step.astype(F32))[:, None]
    lr = lam_re.astype(F32)
    li = lam_im.astype(F32)
    decay = jnp.exp(lr * step)
    ab_re = decay * jnp.cos(li * step)
    ab_im = decay * jnp.sin(li * step)
    nr = ab_re - 1.0
    ni = ab_im
    den = lr * lr + li * li
    f_re = (nr * lr + ni * li) / den
    f_im = (ni * lr - nr * li) / den
    br = b_re.astype(F32)
    bi = b_im.astype(F32)
    bb_re = f_re[..., None] * br - f_im[..., None] * bi
    bb_im = f_re[..., None] * bi + f_im[..., None] * br

    def blockdiag(w):
        g, n, m = w.shape
        stacked = w.transpose(0, 2, 1).reshape(g * m, n)
        same = (jnp.arange(g * m)[:, None] // m) == (jnp.arange(g * n)[None, :] // n)
        return jnp.where(same, jnp.tile(stacked, (1, g)), 0.0).astype(BF16)

    in_blockdiag = blockdiag
    out_blockdiag = blockdiag

    bcast = lambda a: jnp.broadcast_to(a.reshape(1, STATE_LANES), (SUBLANES, STATE_LANES))
    return (in_blockdiag(bb_re), in_blockdiag(bb_im), bcast(ab_re), bcast(ab_im),
            out_blockdiag(c_re.astype(F32)), out_blockdiag(-c_im.astype(F32)))


def kernel(x, rel_bias, norm_mix, w_in, q_gain, k_gain, lambda_q1, lambda_k1, lambda_q2, lambda_k2, subln, w_a, lam_re, lam_im, b_re, b_im, c_re, c_im, d_skip, log_step, w_glu, w_b, w_o, norm_ffn, w1, w3, w2):
    B, S, D = x.shape
    depth = w_in.shape[0]
    assert B == SUBLANES and D == D_MODEL

    table = rel_bias.astype(F32)
    dist = (-jnp.arange(MAX_DISTANCE)) % MAX_DISTANCE
    brev = (table[_t5_bucket(dist)] - table[N_BUCKETS - 1][None]).T * LOG2E
    bias_tiles = _bias_tiles(brev[:, None, :])

    seg = (jnp.arange(MXU_TILE)[:, None] // HEAD_DIM
           == jnp.arange(MXU_TILE)[None, :] // HEAD_DIM).astype(BF16)

    for l in range(depth):
        lam_init = 0.8 - 0.6 * math.exp(-0.3 * l)
        qg = jnp.tile(q_gain[l].astype(F32), QK_WIDTH // HEAD_DIM)[None] * (HEAD_DIM ** -0.5 * LOG2E)
        kg = jnp.tile(k_gain[l].astype(F32), QK_WIDTH // HEAD_DIM)[None]
        q, k, v, u = _inproj(x, norm_mix[l][None], w_in, l, seg, qg, kg)

        lamv = jnp.stack([lambda_q1[l], lambda_k1[l], lambda_q2[l], lambda_k2[l]]).astype(F32)
        o = _attention(lamv, subln[l][None].astype(F32), q, k, v, bias_tiles, lam_init)

        wbr, wbi, ar, ai, cr, ci = _ssm_params(lam_re[l], lam_im[l], b_re[l], b_im[l],
                                               c_re[l], c_im[l], log_step[l])
        s2 = _ssm(u, wbr, wbi, ar, ai, cr, ci,
                  d_skip[l][None].astype(F32), w_glu[l].astype(BF16))

        x = _merge_ffn(x, o, s2, norm_mix[l][None], norm_ffn[l][None],
                       w_in, w_a, w_b, w_o, w1, w3, w2, l)
    return x
```

```python
import functools
import math

import jax
import jax.numpy as jnp
from jax import lax
from jax.experimental import pallas as pl
from jax.experimental.pallas import tpu as pltpu

D_MODEL = 1024
N_HEADS = 4
HEAD_DIM = 64
V_DIM = 2 * HEAD_DIM
QK_WIDTH = N_HEADS * 2 * HEAD_DIM
ATTN_WIDTH = N_HEADS * V_DIM
SSM_WIDTH = D_MODEL // 2
SSM_GROUP = 16
SSM_GROUPS = SSM_WIDTH // SSM_GROUP
SSM_STATE = 64
STATE_LANES = SSM_GROUPS * SSM_STATE
D_FF = 2816
N_BUCKETS = 32
MAX_DISTANCE = 128
EPS = 1e-6
NEG = -1e30

MXU_TILE = 256
SUBLANES = 8

TM_PROJ = 1024
TM_FFN = 1024
TQ = 256
LOG2E = math.log2(math.e)
SCAN_STEPS = 64
SCAN_CHUNKS = 2
SCAN_STRIP = 512
FF_CHUNK = 256
VMEM_LIMIT = 62 * 1024 * 1024

BF16 = jnp.bfloat16
F32 = jnp.float32


def _dot(a, b):
    return jnp.dot(a, b, preferred_element_type=F32)


def _dot_nt(a, b):
    return lax.dot_general(a, b, (((1,), (1,)), ((), ())), preferred_element_type=F32)


def _rms(x, g):
    return x * lax.rsqrt(jnp.mean(x * x, axis=-1, keepdims=True) + EPS) * g


def _resident(shape):
    zeros = (0,) * len(shape)
    return pl.BlockSpec(shape, lambda *_: zeros, pipeline_mode=pl.Buffered(1))


def _per_layer(arr, layer):
    zeros = (0,) * (arr.ndim - 1)
    return pl.BlockSpec((None,) + arr.shape[1:], lambda *_: (layer,) + zeros,
                        pipeline_mode=pl.Buffered(1))


STAGE_ROWS = 128
STAGE_COLS = 1024
STAGE_DEPTH = 8
_STAGE_SCRATCH = [pltpu.VMEM((STAGE_DEPTH, STAGE_ROWS, STAGE_COLS), F32),
                  pltpu.SemaphoreType.DMA((STAGE_DEPTH,))]
_HBM = pl.BlockSpec(memory_space=pl.ANY)


def _stage_bf16(jobs, stage, sems):
    pieces = []
    for src, col0, dst in jobs:
        n_rows, n_cols = dst.shape
        for r0 in range(0, n_rows, STAGE_ROWS):
            for c0 in range(0, n_cols, STAGE_COLS):
                pieces.append((src, col0, dst, r0, c0, min(STAGE_COLS, n_cols - c0)))

    def copy(i):
        src, col0, _, r0, c0, nc = pieces[i]
        slot = i % STAGE_DEPTH
        return pltpu.make_async_copy(
            src.at[pl.ds(r0, STAGE_ROWS), pl.ds(col0 + c0, nc)],
            stage.at[slot, :, pl.ds(0, nc)], sems.at[slot])

    for i in range(min(STAGE_DEPTH, len(pieces))):
        copy(i).start()
    for i, (_, _, dst, r0, c0, nc) in enumerate(pieces):
        copy(i).wait()
        dst[r0:r0 + STAGE_ROWS, c0:c0 + nc] = stage[i % STAGE_DEPTH, :, 0:nc].astype(BF16)
        if i + STAGE_DEPTH < len(pieces):
            copy(i + STAGE_DEPTH).start()


def _bias_tiles_kernel(brev_ref, out_ref):
    sub = MAX_DISTANCE
    r = lax.broadcasted_iota(jnp.int32, (sub, sub), 0)
    c = lax.broadcasted_iota(jnp.int32, (sub, sub), 1)
    y = pltpu.roll(jnp.broadcast_to(brev_ref[...], (sub, sub)), 0, 1, stride=1, stride_axis=0)
    by_diag = {0: jnp.where(c <= r, y, NEG), 1: jnp.where(c > r, y, 0.0)}
    for t, off in enumerate((TQ, 0)):
        for a in range(TQ // sub):
            for b in range(TQ // sub):
                d = a - b + off // sub
                fill = NEG if d < 0 else 0.0
                out_ref[t, a * sub:(a + 1) * sub, b * sub:(b + 1) * sub] = by_diag.get(
                    d, jnp.full((sub, sub), fill, F32))


def _bias_tiles(brev):
    return pl.pallas_call(
        _bias_tiles_kernel,
        grid=(N_HEADS,),
        in_specs=[pl.BlockSpec((None, 1, MAX_DISTANCE), lambda h: (h, 0, 0))],
        out_specs=pl.BlockSpec((None, 2, TQ, TQ), lambda h: (h, 0, 0, 0)),
        out_shape=jax.ShapeDtypeStruct((N_HEADS, 2, TQ, TQ), F32),
        name="bias_tiles",
    )(brev)


def _inproj_kernel(x_ref, g_ref, w_hbm, seg_ref, qg_ref, kg_ref,
                   q_ref, k_ref, v_ref, u_ref, w_ref, stage, sems, *, layer):
    @pl.when((pl.program_id(0) == 0) & (pl.program_id(1) == 0))
    def _():
        _stage_bf16([(w_hbm.at[layer], 0, w_ref)], stage, sems)

    h = _rms(x_ref[...], g_ref[...]).astype(BF16)

    def qk_norm(z, gain):
        z2 = (z * z).astype(BF16)
        ss = jnp.concatenate(
            [_dot(z2[:, t * MXU_TILE:(t + 1) * MXU_TILE], seg_ref[...])
             for t in range(QK_WIDTH // MXU_TILE)], axis=1)
        return z * lax.rsqrt(ss * (1.0 / HEAD_DIM) + EPS) * gain

    zq = _dot(h, w_ref[:, 0:QK_WIDTH])
    q_ref[...] = qk_norm(zq, qg_ref[...]).astype(BF16)
    zk = _dot(h, w_ref[:, QK_WIDTH:2 * QK_WIDTH])
    k_ref[...] = qk_norm(zk, kg_ref[...]).astype(BF16)
    v_ref[...] = _dot(h, w_ref[:, 2 * QK_WIDTH:2 * QK_WIDTH + ATTN_WIDTH]).astype(BF16)
    u_ref[...] = _dot(h, w_ref[:, 2 * QK_WIDTH + ATTN_WIDTH:]).astype(BF16)


def _inproj(x, g, w_in, layer, seg, qg, kg):
    B, S, D = x.shape
    n_cols = 2 * QK_WIDTH + ATTN_WIDTH + SSM_WIDTH
    tok = lambda width: pl.BlockSpec((None, TM_PROJ, width), lambda b, i: (b, i, 0))
    return pl.pallas_call(
        functools.partial(_inproj_kernel, layer=layer),
        grid=(B, S // TM_PROJ),
        in_specs=[tok(D), _per_layer(g, layer), _HBM, _resident(seg.shape),
                  _per_layer(qg, layer), _per_layer(kg, layer)],
        out_specs=[tok(QK_WIDTH), tok(QK_WIDTH), tok(ATTN_WIDTH), tok(SSM_WIDTH)],
        out_shape=[jax.ShapeDtypeStruct((B, S, QK_WIDTH), BF16),
                   jax.ShapeDtypeStruct((B, S, QK_WIDTH), BF16),
                   jax.ShapeDtypeStruct((B, S, ATTN_WIDTH), BF16),
                   jax.ShapeDtypeStruct((B, S, SSM_WIDTH), BF16)],
        scratch_shapes=[pltpu.VMEM((D, n_cols), BF16)] + _STAGE_SCRATCH,
        compiler_params=pltpu.CompilerParams(
            dimension_semantics=("arbitrary", "arbitrary"), vmem_limit_bytes=VMEM_LIMIT),
        name="inproj",
    )(x, g, w_in, seg, qg, kg)


def _attn_kernel(lamv_ref, subln_ref, q_ref, k_ref, v_ref, bias_ref, o_ref,
                 *, lam_init):
    lane = lax.broadcasted_iota(jnp.int32, (TQ, V_DIM), 1)
    ones = jnp.ones((TQ, V_DIM), BF16)
    lv = lamv_ref[...]
    lam = (jnp.exp(jnp.sum(lv[0:1] * lv[1:2], axis=-1, keepdims=True))
           - jnp.exp(jnp.sum(lv[2:3] * lv[3:4], axis=-1, keepdims=True)) + lam_init)

    def attend(n_tiles):
        q_rows = slice((n_tiles - 1) * TQ, n_tiles * TQ)
        q = q_ref[q_rows, :]
        zero = jnp.zeros_like(q)
        qm = (jnp.where(lane < HEAD_DIM, q, zero), jnp.where(lane >= HEAD_DIM, q, zero))
        m = [None] * 2
        acc = [None] * 2
        for j in range(n_tiles):
            rows = slice(j * TQ, (j + 1) * TQ)
            kb = k_ref[rows, :]
            v1 = jnp.concatenate([v_ref[rows, :], ones], axis=1)
            bias = {n_tiles - 1: 1, n_tiles - 2: 0}.get(j)
            for mp in range(2):
                s = _dot_nt(qm[mp], kb)
                if bias is not None:
                    s = s + bias_ref[bias]
                m_cur = jnp.max(s, axis=-1, keepdims=True)
                if j == 0:
                    m_new = jnp.broadcast_to(m_cur, (TQ, V_DIM))
                else:
                    m_new = jnp.maximum(m[mp], m_cur)
                p = jnp.exp2(s - jnp.concatenate([m_new] * (TQ // V_DIM), axis=1))
                pv = _dot(p.astype(BF16), v1)
                if j == 0:
                    acc[mp] = pv
                else:
                    alpha = jnp.exp2(m[mp] - m_new)
                    acc[mp] = jnp.concatenate([alpha, alpha], axis=1) * acc[mp] + pv
                m[mp] = m_new
        o = (acc[0][:, :V_DIM] / acc[0][:, V_DIM:]
             - lam * (acc[1][:, :V_DIM] / acc[1][:, V_DIM:]))
        o_ref[q_rows, :] = (_rms(o, subln_ref[...]) * (1.0 - lam_init)).astype(BF16)

    for n_tiles in range(1, k_ref.shape[0] // TQ + 1):
        attend(n_tiles)


def _attention(lamv, subln, q, k, v, bias_tiles, lam_init, layer):
    B, S, _ = q.shape
    head = pl.BlockSpec((None, S, V_DIM), lambda b, h: (b, 0, h))
    return pl.pallas_call(
        functools.partial(_attn_kernel, lam_init=lam_init),
        grid=(B, N_HEADS),
        in_specs=[_per_layer(lamv, layer), _per_layer(subln, layer), head, head, head,
                  pl.BlockSpec((None, 2, TQ, TQ), lambda b, h: (h, 0, 0, 0))],
        out_specs=head,
        out_shape=jax.ShapeDtypeStruct((B, S, ATTN_WIDTH), BF16),
        compiler_params=pltpu.CompilerParams(
            dimension_semantics=("parallel", "parallel"), vmem_limit_bytes=VMEM_LIMIT),
        name="diff_attention",
    )(lamv, subln, q, k, v, bias_tiles)


def _ssm_kernel(u_ref, perm_ref, permt_ref, wb_ref, a_ref, c_ref, dskip_ref, wglu_ref,
                out_ref, bur, bui, xr_s, xi_s):
    @pl.when(pl.program_id(0) == 0)
    def _():
        xr_s[...] = jnp.zeros(xr_s.shape, F32)
        xi_s[...] = jnp.zeros(xi_s.shape, F32)


    def b_project(c):
        u_bt = u_ref[:, c * SCAN_STEPS:(c + 1) * SCAN_STEPS, :]
        u = _dot(perm_ref[...], u_bt.reshape(SCAN_STEPS * SUBLANES, SSM_WIDTH))
        ub = u.astype(BF16)
        for j in range(STATE_LANES // MXU_TILE):
            kt = (j * MXU_TILE // SSM_STATE * SSM_GROUP) // MXU_TILE
            rows = slice(kt * MXU_TILE, (kt + 1) * MXU_TILE)
            cols = slice(j * MXU_TILE, (j + 1) * MXU_TILE)
            bur[c, :, cols] = _dot(ub[:, rows], wb_ref[0, rows, cols])
            bui[c, :, cols] = _dot(ub[:, rows], wb_ref[1, rows, cols])
        return u

    def scan(c):
        for s in range(STATE_LANES // SCAN_STRIP):
            sl = slice(s * SCAN_STRIP, (s + 1) * SCAN_STRIP)
            a_r = a_ref[0, :, sl]
            a_i = a_ref[1, :, sl]
            xr = xr_s[:, sl]
            xi = xi_s[:, sl]
            for t in range(SCAN_STEPS):
                step = slice(t * SUBLANES, (t + 1) * SUBLANES)
                xr, xi = (a_r * xr - a_i * xi + bur[c, step, sl],
                          a_r * xi + a_i * xr + bui[c, step, sl])
                bur[c, step, sl] = xr
                bui[c, step, sl] = xi
            xr_s[:, sl] = xr
            xi_s[:, sl] = xi

    def c_project(c, u):
        halves = []
        k_per_half = STATE_LANES // 2
        n_per_half = SSM_WIDTH // 2
        for m in range(2):
            rows = slice(m * k_per_half, (m + 1) * k_per_half)
            cols = slice(m * n_per_half, (m + 1) * n_per_half)
            halves.append(_dot(bur[c, :, rows].astype(BF16), c_ref[0, rows, cols])
                          + _dot(bui[c, :, rows].astype(BF16), c_ref[1, rows, cols]))
        y = jnp.concatenate(halves, axis=-1) + dskip_ref[...] * u
        s = jax.nn.gelu(y)
        s = s * jax.nn.sigmoid(_dot(s.astype(BF16), wglu_ref[...]))
        s_bt = _dot(permt_ref[...], s.astype(BF16)).astype(BF16)
        out_ref[:, c * SCAN_STEPS:(c + 1) * SCAN_STEPS, :] = s_bt.reshape(
            SUBLANES, SCAN_STEPS, SSM_WIDTH)

    us = [b_project(c) for c in range(SCAN_CHUNKS)]
    for c in range(SCAN_CHUNKS):
        scan(c)
    for c in range(SCAN_CHUNKS):
        c_project(c, us[c])


def _ssm(u, wb, a, c, dskip, wglu, layer):
    rows = SCAN_STEPS * SUBLANES
    B, S, _ = u.shape
    r = jnp.arange(rows)
    perm = (jnp.arange(rows)[None, :] == ((r % B) * SCAN_STEPS + r // B)[:, None]).astype(BF16)
    blk = pl.BlockSpec((B, SCAN_CHUNKS * SCAN_STEPS, SSM_WIDTH), lambda c: (0, c, 0))
    return pl.pallas_call(
        _ssm_kernel,
        grid=(S // (SCAN_CHUNKS * SCAN_STEPS),),
        in_specs=[blk, _resident(perm.shape), _resident(perm.shape),
                  _per_layer(wb, layer), _per_layer(a, layer), _per_layer(c, layer),
                  _per_layer(dskip, layer), _per_layer(wglu, layer)],
        out_specs=blk,
        out_shape=jax.ShapeDtypeStruct((B, S, SSM_WIDTH), BF16),
        scratch_shapes=[pltpu.VMEM((SCAN_CHUNKS, rows, STATE_LANES), F32),
                        pltpu.VMEM((SCAN_CHUNKS, rows, STATE_LANES), F32),
                        pltpu.VMEM((SUBLANES, STATE_LANES), F32),
                        pltpu.VMEM((SUBLANES, STATE_LANES), F32)],
        compiler_params=pltpu.CompilerParams(
            dimension_semantics=("arbitrary",), vmem_limit_bytes=VMEM_LIMIT),
        name="s5_branch",
    )(u, perm, perm.T, wb, a, c, dskip, wglu)


def _merge_ffn_kernel(x_ref, o_ref, s_ref, g1_ref, g2_ref,
                      w_in_hbm, wa_hbm, wb_hbm, wo_hbm, w1_hbm, w3_hbm, w2_hbm, out_ref,
                      wg_ref, wa_ref, wb_ref, wo_ref, w1_ref, w3_ref, w2_ref,
                      acc_s, stage, sems, *, layer):
    @pl.when((pl.program_id(0) == 0) & (pl.program_id(1) == 0))
    def _():
        gate_col0 = w_in_hbm.shape[2] - wg_ref.shape[1]
        _stage_bf16([(w_in_hbm.at[layer], gate_col0, wg_ref), (wa_hbm.at[layer], 0, wa_ref),
                     (wb_hbm.at[layer], 0, wb_ref), (wo_hbm.at[layer], 0, wo_ref),
                     (w1_hbm.at[layer], 0, w1_ref), (w3_hbm.at[layer], 0, w3_ref),
                     (w2_hbm.at[layer], 0, w2_ref)], stage, sems)

    x = x_ref[...]
    h = _rms(x, g1_ref[...]).astype(BF16)
    mixed = jax.nn.sigmoid(_dot(h, wg_ref[:, 0:D_MODEL])) * _dot(o_ref[...], wa_ref[...])
    mixed += jax.nn.sigmoid(_dot(h, wg_ref[:, D_MODEL:])) * _dot(s_ref[...], wb_ref[...])
    x1 = x + _dot(mixed.astype(BF16), wo_ref[...])

    h2 = _rms(x1, g2_ref[...]).astype(BF16)
    acc_s[...] = x1

    for c in range(D_FF // FF_CHUNK):
        cols = slice(c * FF_CHUNK, (c + 1) * FF_CHUNK)
        t = jax.nn.silu(_dot(h2, w1_ref[:, cols])) * _dot(h2, w3_ref[:, cols])
        acc_s[...] += _dot(t.astype(BF16), w2_ref[cols, :])
    out_ref[...] = acc_s[...]


def _merge_ffn(x, o, s2, g1, g2, w_in, w_a, w_b, w_o, w1, w3, w2, layer):
    B, S, D = x.shape
    tok = lambda width: pl.BlockSpec((None, TM_FFN, width), lambda b, i: (b, i, 0))
    bf16_weight = lambda shape: pltpu.VMEM(shape, BF16)
    return pl.pallas_call(
        functools.partial(_merge_ffn_kernel, layer=layer),
        grid=(B, S // TM_FFN),
        in_specs=[tok(D), tok(ATTN_WIDTH), tok(SSM_WIDTH),
                  _per_layer(g1, layer), _per_layer(g2, layer)] + [_HBM] * 7,
        out_specs=tok(D),
        out_shape=jax.ShapeDtypeStruct((B, S, D), F32),
        scratch_shapes=[bf16_weight((D, 2 * D)), bf16_weight(w_a.shape[1:]),
                        bf16_weight(w_b.shape[1:]), bf16_weight(w_o.shape[1:]),
                        bf16_weight(w1.shape[1:]), bf16_weight(w3.shape[1:]),
                        bf16_weight(w2.shape[1:]),
                        pltpu.VMEM((TM_FFN, D), F32)] + _STAGE_SCRATCH,
        compiler_params=pltpu.CompilerParams(
            dimension_semantics=("arbitrary", "arbitrary"), vmem_limit_bytes=VMEM_LIMIT),
        name="merge_ffn",
    )(x, o, s2, g1, g2, w_in, w_a, w_b, w_o, w1, w3, w2)


def _t5_bucket(n):
    max_exact = N_BUCKETS // 2
    is_small = n < max_exact
    nf = jnp.maximum(n, 1).astype(F32)
    large = max_exact + (jnp.log(nf / max_exact) / math.log(MAX_DISTANCE / max_exact)
                         * (N_BUCKETS - max_exact)).astype(jnp.int32)
    large = jnp.minimum(large, N_BUCKETS - 1)
    return jnp.where(is_small, n, large)


def _ssm_params(lam_re, lam_im, b_re, b_im, c_re, c_im, log_step):
    step = jnp.exp(log_step.astype(F32))[..., None]
    lr = lam_re.astype(F32)
    li = lam_im.astype(F32)
    decay = jnp.exp(lr * step)
    ab_re = decay * jnp.cos(li * step)
    ab_im = decay * jnp.sin(li * step)
    nr = ab_re - 1.0
    ni = ab_im
    den = lr * lr + li * li
    f_re = (nr * lr + ni * li) / den
    f_im = (ni * lr - nr * li) / den
    br = b_re.astype(F32)
    bi = b_im.astype(F32)
    bb_re = f_re[..., None] * br - f_im[..., None] * bi
    bb_im = f_re[..., None] * bi + f_im[..., None] * br

    def blockdiag(w):
        *lead, g, n, m = w.shape
        stacked = jnp.swapaxes(w, -1, -2).reshape(*lead, g * m, n)
        same = (jnp.arange(g * m)[:, None] // m) == (jnp.arange(g * n)[None, :] // n)
        return jnp.where(same, jnp.tile(stacked, (1,) * len(lead) + (1, g)), 0.0).astype(BF16)

    depth = lr.shape[0]
    a = jnp.stack([ab_re, ab_im], axis=1).reshape(depth, 2, 1, STATE_LANES)
    return (blockdiag(jnp.stack([bb_re, bb_im], axis=1)),
            jnp.broadcast_to(a, (depth, 2, SUBLANES, STATE_LANES)),
            blockdiag(jnp.stack([c_re.astype(F32), -c_im.astype(F32)], axis=1)))


def kernel(x, rel_bias, norm_mix, w_in, q_gain, k_gain, lambda_q1, lambda_k1, lambda_q2, lambda_k2, subln, w_a, lam_re, lam_im, b_re, b_im, c_re, c_im, d_skip, log_step, w_glu, w_b, w_o, norm_ffn, w1, w3, w2):
    B, S, D = x.shape
    depth = w_in.shape[0]
    assert B == SUBLANES and D == D_MODEL

    table = rel_bias.astype(F32)
    dist = (-jnp.arange(MAX_DISTANCE)) % MAX_DISTANCE
    brev = (table[_t5_bucket(dist)] - table[N_BUCKETS - 1][None]).T * LOG2E
    bias_tiles = _bias_tiles(brev[:, None, :])

    seg = (jnp.arange(MXU_TILE)[:, None] // HEAD_DIM
           == jnp.arange(MXU_TILE)[None, :] // HEAD_DIM).astype(BF16)

    row = lambda p: p.astype(F32)[:, None, :]
    heads = QK_WIDTH // HEAD_DIM
    qg = row(jnp.tile(q_gain, (1, heads))) * (HEAD_DIM ** -0.5 * LOG2E)
    kg = row(jnp.tile(k_gain, (1, heads)))
    lamv = jnp.stack([lambda_q1, lambda_k1, lambda_q2, lambda_k2], axis=1).astype(F32)
    wb, a, c = _ssm_params(lam_re, lam_im, b_re, b_im, c_re, c_im, log_step)
    g_mix, g_ffn, g_sub, dskip = row(norm_mix), row(norm_ffn), row(subln), row(d_skip)
    wglu = w_glu.astype(BF16)

    for l in range(depth):
        lam_init = 0.8 - 0.6 * math.exp(-0.3 * l)
        q, k, v, u = _inproj(x, g_mix, w_in, l, seg, qg, kg)
        o = _attention(lamv, g_sub, q, k, v, bias_tiles, lam_init, l)
        s2 = _ssm(u, wb, a, c, dskip, wglu, l)
        x = _merge_ffn(x, o, s2, g_mix, g_ffn, w_in, w_a, w_b, w_o, w1, w3, w2, l)
    return x
```

```python
import functools
import math

import jax
import jax.numpy as jnp
from jax import lax
from jax.experimental import pallas as pl
from jax.experimental.pallas import tpu as pltpu

D_MODEL = 1024
N_HEADS = 4
HEAD_DIM = 64
V_DIM = 2 * HEAD_DIM
QK_WIDTH = N_HEADS * 2 * HEAD_DIM
ATTN_WIDTH = N_HEADS * V_DIM
SSM_WIDTH = D_MODEL // 2
SSM_GROUP = 16
SSM_GROUPS = SSM_WIDTH // SSM_GROUP
SSM_STATE = 64
STATE_LANES = SSM_GROUPS * SSM_STATE
D_FF = 2816
N_BUCKETS = 32
MAX_DISTANCE = 128
EPS = 1e-6
NEG = -1e30

MXU_TILE = 256
SUBLANES = 8

TM_PROJ = 1024
TM_FFN = 1024
TQ = 256
ONES_ROWS = 16
SCORE_LOOKAHEAD = 8
LOG2E = math.log2(math.e)
SCAN_STEPS = 64
SCAN_CHUNKS = 2
SCAN_STRIP = 512
FF_CHUNK = 256
VMEM_LIMIT = 62 * 1024 * 1024

BF16 = jnp.bfloat16
F32 = jnp.float32


def _dot(a, b):
    return jnp.dot(a, b, preferred_element_type=F32)


def _dot_nt(a, b):
    return lax.dot_general(a, b, (((1,), (1,)), ((), ())), preferred_element_type=F32)


def _rms(x, g):
    return x * lax.rsqrt(jnp.mean(x * x, axis=-1, keepdims=True) + EPS) * g


def _resident(shape):
    zeros = (0,) * len(shape)
    return pl.BlockSpec(shape, lambda *_: zeros, pipeline_mode=pl.Buffered(1))


def _per_layer(arr, layer):
    zeros = (0,) * (arr.ndim - 1)
    return pl.BlockSpec((None,) + arr.shape[1:], lambda *_: (layer,) + zeros,
                        pipeline_mode=pl.Buffered(1))


STAGE_ROWS = 128
STAGE_COLS = 1024
STAGE_DEPTH = 8
_STAGE_SCRATCH = [pltpu.VMEM((STAGE_DEPTH, STAGE_ROWS, STAGE_COLS), F32),
                  pltpu.SemaphoreType.DMA((STAGE_DEPTH,))]
_HBM = pl.BlockSpec(memory_space=pl.ANY)


def _stage_bf16(jobs, stage, sems):
    pieces = []
    for src, col0, dst in jobs:
        n_rows, n_cols = dst.shape
        for r0 in range(0, n_rows, STAGE_ROWS):
            for c0 in range(0, n_cols, STAGE_COLS):
                pieces.append((src, col0, dst, r0, c0, min(STAGE_COLS, n_cols - c0)))

    def copy(i):
        src, col0, _, r0, c0, nc = pieces[i]
        slot = i % STAGE_DEPTH
        return pltpu.make_async_copy(
            src.at[pl.ds(r0, STAGE_ROWS), pl.ds(col0 + c0, nc)],
            stage.at[slot, :, pl.ds(0, nc)], sems.at[slot])

    for i in range(min(STAGE_DEPTH, len(pieces))):
        copy(i).start()
    for i, (_, _, dst, r0, c0, nc) in enumerate(pieces):
        copy(i).wait()
        dst[r0:r0 + STAGE_ROWS, c0:c0 + nc] = stage[i % STAGE_DEPTH, :, 0:nc].astype(BF16)
        if i + STAGE_DEPTH < len(pieces):
            copy(i + STAGE_DEPTH).start()


def _bias_tiles_kernel(brev_ref, out_ref):
    sub = MAX_DISTANCE
    r = lax.broadcasted_iota(jnp.int32, (sub, sub), 0)
    c = lax.broadcasted_iota(jnp.int32, (sub, sub), 1)
    y = pltpu.roll(jnp.broadcast_to(brev_ref[...], (sub, sub)), 0, 1, stride=1, stride_axis=0)
    by_diag = {0: jnp.where(c <= r, y, NEG).T, 1: jnp.where(c > r, y, 0.0).T}
    for t, off in enumerate((TQ, 0)):
        for a in range(TQ // sub):
            for b in range(TQ // sub):
                d = a - b + off // sub
                fill = NEG if d < 0 else 0.0
                out_ref[t, b * sub:(b + 1) * sub, a * sub:(a + 1) * sub] = by_diag.get(
                    d, jnp.full((sub, sub), fill, F32))


def _bias_tiles(brev):
    return pl.pallas_call(
        _bias_tiles_kernel,
        grid=(N_HEADS,),
        in_specs=[pl.BlockSpec((None, 1, MAX_DISTANCE), lambda h: (h, 0, 0))],
        out_specs=pl.BlockSpec((None, 2, TQ, TQ), lambda h: (h, 0, 0, 0)),
        out_shape=jax.ShapeDtypeStruct((N_HEADS, 2, TQ, TQ), F32),
        name="bias_tiles",
    )(brev)


def _inproj_kernel(x_ref, g_ref, w_hbm, seg_ref, qg_ref, kg_ref,
                   q_ref, k_ref, vt_ref, u_ref, w_ref, stage, sems, *, layer):
    @pl.when((pl.program_id(0) == 0) & (pl.program_id(1) == 0))
    def _():
        _stage_bf16([(w_hbm.at[layer], 0, w_ref)], stage, sems)

    h = _rms(x_ref[...], g_ref[...]).astype(BF16)

    def qk_norm(z, gain):
        z2 = (z * z).astype(BF16)
        ss = jnp.concatenate(
            [_dot(z2[:, t * MXU_TILE:(t + 1) * MXU_TILE], seg_ref[...])
             for t in range(QK_WIDTH // MXU_TILE)], axis=1)
        return z * lax.rsqrt(ss * (1.0 / HEAD_DIM) + EPS) * gain

    zq = _dot(h, w_ref[:, 0:QK_WIDTH])
    q_ref[...] = qk_norm(zq, qg_ref[...]).astype(BF16)
    zk = _dot(h, w_ref[:, QK_WIDTH:2 * QK_WIDTH])
    k_ref[...] = qk_norm(zk, kg_ref[...]).astype(BF16)
    v = _dot(h, w_ref[:, 2 * QK_WIDTH:2 * QK_WIDTH + ATTN_WIDTH])
    vt_ref[...] = v.T.astype(BF16)
    u_ref[...] = _dot(h, w_ref[:, 2 * QK_WIDTH + ATTN_WIDTH:]).astype(BF16)


def _inproj(x, g, w_in, layer, seg, qg, kg):
    B, S, D = x.shape
    n_cols = 2 * QK_WIDTH + ATTN_WIDTH + SSM_WIDTH
    tok = lambda width: pl.BlockSpec((None, TM_PROJ, width), lambda b, i: (b, i, 0))
    return pl.pallas_call(
        functools.partial(_inproj_kernel, layer=layer),
        grid=(B, S // TM_PROJ),
        in_specs=[tok(D), _per_layer(g, layer), _HBM, _resident(seg.shape),
                  _per_layer(qg, layer), _per_layer(kg, layer)],
        out_specs=[tok(QK_WIDTH), tok(QK_WIDTH),
                   pl.BlockSpec((None, ATTN_WIDTH, TM_PROJ), lambda b, i: (b, 0, i)),
                   tok(SSM_WIDTH)],
        out_shape=[jax.ShapeDtypeStruct((B, S, QK_WIDTH), BF16),
                   jax.ShapeDtypeStruct((B, S, QK_WIDTH), BF16),
                   jax.ShapeDtypeStruct((B, ATTN_WIDTH, S), BF16),
                   jax.ShapeDtypeStruct((B, S, SSM_WIDTH), BF16)],
        scratch_shapes=[pltpu.VMEM((D, n_cols), BF16)] + _STAGE_SCRATCH,
        compiler_params=pltpu.CompilerParams(
            dimension_semantics=("arbitrary", "arbitrary"), vmem_limit_bytes=VMEM_LIMIT),
        name="inproj",
    )(x, g, w_in, seg, qg, kg)


def _attn_kernel(lamv_ref, subln_ref, q_ref, k_ref, vt_ref, bias_ref, o_ref,
                 *, lam_init):
    lane = lax.broadcasted_iota(jnp.int32, (TQ, V_DIM), 1)
    ones = jnp.ones((ONES_ROWS, TQ), BF16)
    lv = lamv_ref[...]
    lam = (jnp.exp(jnp.sum(lv[0:1] * lv[1:2], axis=-1, keepdims=True))
           - jnp.exp(jnp.sum(lv[2:3] * lv[3:4], axis=-1, keepdims=True)) + lam_init)

    n_q = k_ref.shape[0] // TQ
    tasks = [(qi, j, mp) for qi in range(n_q) for j in range(qi + 1) for mp in range(2)]
    qms, m, acc = {}, {}, {}

    def scores(qi, j, mp):
        if qi not in qms:
            q = q_ref[qi * TQ:(qi + 1) * TQ, :]
            zero = jnp.zeros_like(q)
            qms[qi] = (jnp.where(lane < HEAD_DIM, q, zero), jnp.where(lane >= HEAD_DIM, q, zero))
        st = _dot_nt(k_ref[j * TQ:(j + 1) * TQ, :], qms[qi][mp])
        bias = {qi: 1, qi - 1: 0}.get(j)
        return st if bias is None else st + bias_ref[bias]

    def softmax_pv(qi, j, mp, st):
        v1t = jnp.concatenate([vt_ref[:, j * TQ:(j + 1) * TQ], ones], axis=0)
        m_cur = jnp.max(st, axis=0, keepdims=True)
        m_new = m_cur if j == 0 else jnp.maximum(m[qi, mp], m_cur)
        p = jnp.exp2(st - m_new)
        pv = _dot(v1t, p.astype(BF16))
        acc[qi, mp] = pv if j == 0 else jnp.exp2(m[qi, mp] - m_new) * acc[qi, mp] + pv
        m[qi, mp] = m_new
        if j == qi and mp == 1:
            a0, a1 = acc.pop((qi, 0)), acc.pop((qi, 1))
            o_t = a0[:V_DIM] / a0[V_DIM:V_DIM + 1] - lam * (a1[:V_DIM] / a1[V_DIM:V_DIM + 1])
            o_n = (o_t * lax.rsqrt(jnp.mean(o_t * o_t, axis=0, keepdims=True) + EPS)).T
            o_ref[qi * TQ:(qi + 1) * TQ, :] = (
                o_n * subln_ref[...] * (1.0 - lam_init)).astype(BF16)

    pending = {}
    for i in range(len(tasks) + SCORE_LOOKAHEAD):
        if i < len(tasks):
            pending[i] = scores(*tasks[i])
        if i >= SCORE_LOOKAHEAD:
            softmax_pv(*tasks[i - SCORE_LOOKAHEAD], pending.pop(i - SCORE_LOOKAHEAD))


def _attention(lamv, subln, q, k, vt, bias_tiles, lam_init, layer):
    B, S, _ = q.shape
    head = pl.BlockSpec((None, S, V_DIM), lambda b, h: (b, 0, h))
    head_t = pl.BlockSpec((None, V_DIM, S), lambda b, h: (b, h, 0))
    return pl.pallas_call(
        functools.partial(_attn_kernel, lam_init=lam_init),
        grid=(B, N_HEADS),
        in_specs=[_per_layer(lamv, layer), _per_layer(subln, layer), head, head, head_t,
                  pl.BlockSpec((None, 2, TQ, TQ), lambda b, h: (h, 0, 0, 0))],
        out_specs=head,
        out_shape=jax.ShapeDtypeStruct((B, S, ATTN_WIDTH), BF16),
        compiler_params=pltpu.CompilerParams(
            dimension_semantics=("parallel", "parallel"), vmem_limit_bytes=VMEM_LIMIT),
        name="diff_attention",
    )(lamv, subln, q, k, vt, bias_tiles)


def _ssm_kernel(u_ref, perm_ref, permt_ref, wb_ref, a_ref, c_ref, dskip_ref, wglu_ref,
                out_ref, bur, bui, xr_s, xi_s):
    @pl.when(pl.program_id(0) == 0)
    def _():
        xr_s[...] = jnp.zeros(xr_s.shape, F32)
        xi_s[...] = jnp.zeros(xi_s.shape, F32)


    def b_project(c):
        u_bt = u_ref[:, c * SCAN_STEPS:(c + 1) * SCAN_STEPS, :]
        u = _dot(perm_ref[...], u_bt.reshape(SCAN_STEPS * SUBLANES, SSM_WIDTH))
        ub = u.astype(BF16)
        for j in range(STATE_LANES // MXU_TILE):
            kt = (j * MXU_TILE // SSM_STATE * SSM_GROUP) // MXU_TILE
            rows = slice(kt * MXU_TILE, (kt + 1) * MXU_TILE)
            cols = slice(j * MXU_TILE, (j + 1) * MXU_TILE)
            bur[c, :, cols] = _dot(ub[:, rows], wb_ref[0, rows, cols])
            bui[c, :, cols] = _dot(ub[:, rows], wb_ref[1, rows, cols])
        return u

    def scan(c):
        for s in range(STATE_LANES // SCAN_STRIP):
            sl = slice(s * SCAN_STRIP, (s + 1) * SCAN_STRIP)
            a_r = a_ref[0, :, sl]
            a_i = a_ref[1, :, sl]
            xr = xr_s[:, sl]
            xi = xi_s[:, sl]
            for t in range(SCAN_STEPS):
                step = slice(t * SUBLANES, (t + 1) * SUBLANES)
                xr, xi = (a_r * xr - a_i * xi + bur[c, step, sl],
                          a_r * xi + a_i * xr + bui[c, step, sl])
                bur[c, step, sl] = xr
                bui[c, step, sl] = xi
            xr_s[:, sl] = xr
            xi_s[:, sl] = xi

    def c_project(c, u):
        halves = []
        k_per_half = STATE_LANES // 2
        n_per_half = SSM_WIDTH // 2
        for m in range(2):
            rows = slice(m * k_per_half, (m + 1) * k_per_half)
            cols = slice(m * n_per_half, (m + 1) * n_per_half)
            halves.append(_dot(bur[c, :, rows].astype(BF16), c_ref[0, rows, cols])
                          + _dot(bui[c, :, rows].astype(BF16), c_ref[1, rows, cols]))
        y = jnp.concatenate(halves, axis=-1) + dskip_ref[...] * u
        s = jax.nn.gelu(y)
        s = s * jax.nn.sigmoid(_dot(s.astype(BF16), wglu_ref[...]))
        s_bt = _dot(permt_ref[...], s.astype(BF16)).astype(BF16)
        out_ref[:, c * SCAN_STEPS:(c + 1) * SCAN_STEPS, :] = s_bt.reshape(
            SUBLANES, SCAN_STEPS, SSM_WIDTH)

    us = [b_project(c) for c in range(SCAN_CHUNKS)]
    for c in range(SCAN_CHUNKS):
        scan(c)
    for c in range(SCAN_CHUNKS):
        c_project(c, us[c])


def _ssm(u, wb, a, c, dskip, wglu, layer):
    rows = SCAN_STEPS * SUBLANES
    B, S, _ = u.shape
    r = jnp.arange(rows)
    perm = (jnp.arange(rows)[None, :] == ((r % B) * SCAN_STEPS + r // B)[:, None]).astype(BF16)
    blk = pl.BlockSpec((B, SCAN_CHUNKS * SCAN_STEPS, SSM_WIDTH), lambda c: (0, c, 0))
    return pl.pallas_call(
        _ssm_kernel,
        grid=(S // (SCAN_CHUNKS * SCAN_STEPS),),
        in_specs=[blk, _resident(perm.shape), _resident(perm.shape),
                  _per_layer(wb, layer), _per_layer(a, layer), _per_layer(c, layer),
                  _per_layer(dskip, layer), _per_layer(wglu, layer)],
        out_specs=blk,
        out_shape=jax.ShapeDtypeStruct((B, S, SSM_WIDTH), BF16),
        scratch_shapes=[pltpu.VMEM((SCAN_CHUNKS, rows, STATE_LANES), F32),
                        pltpu.VMEM((SCAN_CHUNKS, rows, STATE_LANES), F32),
                        pltpu.VMEM((SUBLANES, STATE_LANES), F32),
                        pltpu.VMEM((SUBLANES, STATE_LANES), F32)],
        compiler_params=pltpu.CompilerParams(
            dimension_semantics=("arbitrary",), vmem_limit_bytes=VMEM_LIMIT),
        name="s5_branch",
    )(u, perm, perm.T, wb, a, c, dskip, wglu)


def _merge_ffn_kernel(x_ref, o_ref, s_ref, g1_ref, g2_ref,
                      w_in_hbm, wa_hbm, wb_hbm, wo_hbm, w1_hbm, w3_hbm, w2_hbm, out_ref,
                      wg_ref, wa_ref, wb_ref, wo_ref, w1_ref, w3_ref, w2_ref,
                      acc_s, stage, sems, *, layer):
    @pl.when((pl.program_id(0) == 0) & (pl.program_id(1) == 0))
    def _():
        gate_col0 = w_in_hbm.shape[2] - wg_ref.shape[1]
        _stage_bf16([(w_in_hbm.at[layer], gate_col0, wg_ref), (wa_hbm.at[layer], 0, wa_ref),
                     (wb_hbm.at[layer], 0, wb_ref), (wo_hbm.at[layer], 0, wo_ref),
                     (w1_hbm.at[layer], 0, w1_ref), (w3_hbm.at[layer], 0, w3_ref),
                     (w2_hbm.at[layer], 0, w2_ref)], stage, sems)

    x = x_ref[...]
    h = _rms(x, g1_ref[...]).astype(BF16)
    mixed = jax.nn.sigmoid(_dot(h, wg_ref[:, 0:D_MODEL])) * _dot(o_ref[...], wa_ref[...])
    mixed += jax.nn.sigmoid(_dot(h, wg_ref[:, D_MODEL:])) * _dot(s_ref[...], wb_ref[...])
    x1 = x + _dot(mixed.astype(BF16), wo_ref[...])

    h2 = _rms(x1, g2_ref[...]).astype(BF16)
    acc_s[...] = x1

    for c in range(D_FF // FF_CHUNK):
        cols = slice(c * FF_CHUNK, (c + 1) * FF_CHUNK)
        t = jax.nn.silu(_dot(h2, w1_ref[:, cols])) * _dot(h2, w3_ref[:, cols])
        acc_s[...] += _dot(t.astype(BF16), w2_ref[cols, :])
    out_ref[...] = acc_s[...]


def _merge_ffn(x, o, s2, g1, g2, w_in, w_a, w_b, w_o, w1, w3, w2, layer):
    B, S, D = x.shape
    tok = lambda width: pl.BlockSpec((None, TM_FFN, width), lambda b, i: (b, i, 0))
    bf16_weight = lambda shape: pltpu.VMEM(shape, BF16)
    return pl.pallas_call(
        functools.partial(_merge_ffn_kernel, layer=layer),
        grid=(B, S // TM_FFN),
        in_specs=[tok(D), tok(ATTN_WIDTH), tok(SSM_WIDTH),
                  _per_layer(g1, layer), _per_layer(g2, layer)] + [_HBM] * 7,
        out_specs=tok(D),
        out_shape=jax.ShapeDtypeStruct((B, S, D), F32),
        scratch_shapes=[bf16_weight((D, 2 * D)), bf16_weight(w_a.shape[1:]),
                        bf16_weight(w_b.shape[1:]), bf16_weight(w_o.shape[1:]),
                        bf16_weight(w1.shape[1:]), bf16_weight(w3.shape[1:]),
                        bf16_weight(w2.shape[1:]),
                        pltpu.VMEM((TM_FFN, D), F32)] + _STAGE_SCRATCH,
        compiler_params=pltpu.CompilerParams(
            dimension_semantics=("arbitrary", "arbitrary"), vmem_limit_bytes=VMEM_LIMIT),
        name="merge_ffn",
    )(x, o, s2, g1, g2, w_in, w_a, w_b, w_o, w1, w3, w2)


def _t5_bucket(n):
    max_exact = N_BUCKETS // 2
    is_small = n < max_exact
    nf = jnp.maximum(n, 1).astype(F32)
    large = max_exact + (jnp.log(nf / max_exact) / math.log(MAX_DISTANCE / max_exact)
                         * (N_BUCKETS - max_exact)).astype(jnp.int32)
    large = jnp.minimum(large, N_BUCKETS - 1)
    return jnp.where(is_small, n, large)


def _ssm_params(lam_re, lam_im, b_re, b_im, c_re, c_im, log_step):
    step = jnp.exp(log_step.astype(F32))[..., None]
    lr = lam_re.astype(F32)
    li = lam_im.astype(F32)
    decay = jnp.exp(lr * step)
    ab_re = decay * jnp.cos(li * step)
    ab_im = decay * jnp.sin(li * step)
    nr = ab_re - 1.0
    ni = ab_im
    den = lr * lr + li * li
    f_re = (nr * lr + ni * li) / den
    f_im = (ni * lr - nr * li) / den
    br = b_re.astype(F32)
    bi = b_im.astype(F32)
    bb_re = f_re[..., None] * br - f_im[..., None] * bi
    bb_im = f_re[..., None] * bi + f_im[..., None] * br

    def blockdiag(w):
        *lead, g, n, m = w.shape
        stacked = jnp.swapaxes(w, -1, -2).reshape(*lead, g * m, n)
        same = (jnp.arange(g * m)[:, None] // m) == (jnp.arange(g * n)[None, :] // n)
        return jnp.where(same, jnp.tile(stacked, (1,) * len(lead) + (1, g)), 0.0).astype(BF16)

    depth = lr.shape[0]
    a = jnp.stack([ab_re, ab_im], axis=1).reshape(depth, 2, 1, STATE_LANES)
    return (blockdiag(jnp.stack([bb_re, bb_im], axis=1)),
            jnp.broadcast_to(a, (depth, 2, SUBLANES, STATE_LANES)),
            blockdiag(jnp.stack([c_re.astype(F32), -c_im.astype(F32)], axis=1)))


def kernel(x, rel_bias, norm_mix, w_in, q_gain, k_gain, lambda_q1, lambda_k1, lambda_q2, lambda_k2, subln, w_a, lam_re, lam_im, b_re, b_im, c_re, c_im, d_skip, log_step, w_glu, w_b, w_o, norm_ffn, w1, w3, w2):
    B, S, D = x.shape
    depth = w_in.shape[0]
    assert B == SUBLANES and D == D_MODEL

    table = rel_bias.astype(F32)
    dist = (-jnp.arange(MAX_DISTANCE)) % MAX_DISTANCE
    brev = (table[_t5_bucket(dist)] - table[N_BUCKETS - 1][None]).T * LOG2E
    bias_tiles = _bias_tiles(brev[:, None, :])

    seg = (jnp.arange(MXU_TILE)[:, None] // HEAD_DIM
           == jnp.arange(MXU_TILE)[None, :] // HEAD_DIM).astype(BF16)

    row = lambda p: p.astype(F32)[:, None, :]
    heads = QK_WIDTH // HEAD_DIM
    qg = row(jnp.tile(q_gain, (1, heads))) * (HEAD_DIM ** -0.5 * LOG2E)
    kg = row(jnp.tile(k_gain, (1, heads)))
    lamv = jnp.stack([lambda_q1, lambda_k1, lambda_q2, lambda_k2], axis=1).astype(F32)
    wb, a, c = _ssm_params(lam_re, lam_im, b_re, b_im, c_re, c_im, log_step)
    g_mix, g_ffn, g_sub, dskip = row(norm_mix), row(norm_ffn), row(subln), row(d_skip)
    wglu = w_glu.astype(BF16)

    for l in range(depth):
        lam_init = 0.8 - 0.6 * math.exp(-0.3 * l)
        q, k, vt, u = _inproj(x, g_mix, w_in, l, seg, qg, kg)
        o = _attention(lamv, g_sub, q, k, vt, bias_tiles, lam_init, l)
        s2 = _ssm(u, wb, a, c, dskip, wglu, l)
        x = _merge_ffn(x, o, s2, g_mix, g_ffn, w_in, w_a, w_b, w_o, w1, w3, w2, l)
    return x
```

```python
import functools
import math

import jax
import jax.numpy as jnp
from jax import lax
from jax.experimental import pallas as pl
from jax.experimental.pallas import tpu as pltpu

D_MODEL = 1024
N_HEADS = 4
HEAD_DIM = 64
V_DIM = 2 * HEAD_DIM
QK_WIDTH = N_HEADS * 2 * HEAD_DIM
ATTN_WIDTH = N_HEADS * V_DIM
SSM_WIDTH = D_MODEL // 2
SSM_GROUP = 16
SSM_GROUPS = SSM_WIDTH // SSM_GROUP
SSM_STATE = 64
STATE_LANES = SSM_GROUPS * SSM_STATE
D_FF = 2816
N_BUCKETS = 32
MAX_DISTANCE = 128
EPS = 1e-6
NEG = -1e30

MXU_TILE = 256
SUBLANES = 8

TM_PROJ = 1024
TM_FFN = 1024
TQ = 256
ONES_ROWS = 16
SCORE_LOOKAHEAD = 8
LOG2E = math.log2(math.e)
SCAN_STEPS = 64
SCAN_CHUNKS = 2
SCAN_STRIP = 512
FF_CHUNK = 256
VMEM_LIMIT = 62 * 1024 * 1024

BF16 = jnp.bfloat16
F32 = jnp.float32


def _dot(a, b):
    return jnp.dot(a, b, preferred_element_type=F32)


def _dot_nt(a, b):
    return lax.dot_general(a, b, (((1,), (1,)), ((), ())), preferred_element_type=F32)


def _rms(x, g):
    return x * lax.rsqrt(jnp.mean(x * x, axis=-1, keepdims=True) + EPS) * g


def _resident(shape):
    zeros = (0,) * len(shape)
    return pl.BlockSpec(shape, lambda *_: zeros, pipeline_mode=pl.Buffered(1))


def _per_layer(arr, layer):
    zeros = (0,) * (arr.ndim - 1)
    return pl.BlockSpec((None,) + arr.shape[1:], lambda *_: (layer,) + zeros,
                        pipeline_mode=pl.Buffered(1))


STAGE_ROWS = 128
STAGE_COLS = 1024
STAGE_DEPTH = 8
_STAGE_SCRATCH = [pltpu.VMEM((STAGE_DEPTH, STAGE_ROWS, STAGE_COLS), F32),
                  pltpu.SemaphoreType.DMA((STAGE_DEPTH,))]
_HBM = pl.BlockSpec(memory_space=pl.ANY)


def _stage_bf16(jobs, stage, sems):
    pieces = []
    for src, col0, dst in jobs:
        n_rows, n_cols = dst.shape
        for r0 in range(0, n_rows, STAGE_ROWS):
            for c0 in range(0, n_cols, STAGE_COLS):
                pieces.append((src, col0, dst, r0, c0, min(STAGE_COLS, n_cols - c0)))

    def copy(i):
        src, col0, _, r0, c0, nc = pieces[i]
        slot = i % STAGE_DEPTH
        return pltpu.make_async_copy(
            src.at[pl.ds(r0, STAGE_ROWS), pl.ds(col0 + c0, nc)],
            stage.at[slot, :, pl.ds(0, nc)], sems.at[slot])

    for i in range(min(STAGE_DEPTH, len(pieces))):
        copy(i).start()
    for i, (_, _, dst, r0, c0, nc) in enumerate(pieces):
        copy(i).wait()
        dst[r0:r0 + STAGE_ROWS, c0:c0 + nc] = stage[i % STAGE_DEPTH, :, 0:nc].astype(BF16)
        if i + STAGE_DEPTH < len(pieces):
            copy(i + STAGE_DEPTH).start()


def _bias_tiles_kernel(brev_ref, out_ref):
    sub = MAX_DISTANCE
    r = lax.broadcasted_iota(jnp.int32, (sub, sub), 0)
    c = lax.broadcasted_iota(jnp.int32, (sub, sub), 1)
    y = pltpu.roll(jnp.broadcast_to(brev_ref[...], (sub, sub)), 0, 1, stride=1, stride_axis=0)
    by_diag = {0: jnp.where(c <= r, y, NEG).T, 1: jnp.where(c > r, y, 0.0).T}
    for t, off in enumerate((TQ, 0)):
        for a in range(TQ // sub):
            for b in range(TQ // sub):
                d = a - b + off // sub
                fill = NEG if d < 0 else 0.0
                out_ref[t, b * sub:(b + 1) * sub, a * sub:(a + 1) * sub] = by_diag.get(
                    d, jnp.full((sub, sub), fill, F32))


def _bias_tiles(brev):
    return pl.pallas_call(
        _bias_tiles_kernel,
        grid=(N_HEADS,),
        in_specs=[pl.BlockSpec((None, 1, MAX_DISTANCE), lambda h: (h, 0, 0))],
        out_specs=pl.BlockSpec((None, 2, TQ, TQ), lambda h: (h, 0, 0, 0)),
        out_shape=jax.ShapeDtypeStruct((N_HEADS, 2, TQ, TQ), F32),
        name="bias_tiles",
    )(brev)


def _inproj_kernel(x_ref, g_ref, w_hbm, seg_ref, qg_ref, kg_ref,
                   q_ref, k_ref, vt_ref, u_ref, w_ref, stage, sems, *, layer):
    @pl.when((pl.program_id(0) == 0) & (pl.program_id(1) == 0))
    def _():
        _stage_bf16([(w_hbm.at[layer], 0, w_ref)], stage, sems)

    h = _rms(x_ref[...], g_ref[...]).astype(BF16)

    def qk_norm(z, gain):
        z2 = (z * z).astype(BF16)
        ss = jnp.concatenate(
            [_dot(z2[:, t * MXU_TILE:(t + 1) * MXU_TILE], seg_ref[...])
             for t in range(QK_WIDTH // MXU_TILE)], axis=1)
        return z * lax.rsqrt(ss * (1.0 / HEAD_DIM) + EPS) * gain

    zq = _dot(h, w_ref[:, 0:QK_WIDTH])
    zk = _dot(h, w_ref[:, QK_WIDTH:2 * QK_WIDTH])
    q_ref[...] = qk_norm(zq, qg_ref[...]).astype(BF16)
    v = _dot(h, w_ref[:, 2 * QK_WIDTH:2 * QK_WIDTH + ATTN_WIDTH])
    k_ref[...] = qk_norm(zk, kg_ref[...]).astype(BF16)
    u_ref[...] = _dot(h, w_ref[:, 2 * QK_WIDTH + ATTN_WIDTH:]).astype(BF16)
    vt_ref[...] = v.T.astype(BF16)


def _inproj(x, g, w_in, layer, seg, qg, kg):
    B, S, D = x.shape
    n_cols = 2 * QK_WIDTH + ATTN_WIDTH + SSM_WIDTH
    tok = lambda width: pl.BlockSpec((None, TM_PROJ, width), lambda b, i: (b, i, 0))
    return pl.pallas_call(
        functools.partial(_inproj_kernel, layer=layer),
        grid=(B, S // TM_PROJ),
        in_specs=[tok(D), _per_layer(g, layer), _HBM, _resident(seg.shape),
                  _per_layer(qg, layer), _per_layer(kg, layer)],
        out_specs=[tok(QK_WIDTH), tok(QK_WIDTH),
                   pl.BlockSpec((None, ATTN_WIDTH, TM_PROJ), lambda b, i: (b, 0, i)),
                   tok(SSM_WIDTH)],
        out_shape=[jax.ShapeDtypeStruct((B, S, QK_WIDTH), BF16),
                   jax.ShapeDtypeStruct((B, S, QK_WIDTH), BF16),
                   jax.ShapeDtypeStruct((B, ATTN_WIDTH, S), BF16),
                   jax.ShapeDtypeStruct((B, S, SSM_WIDTH), BF16)],
        scratch_shapes=[pltpu.VMEM((D, n_cols), BF16)] + _STAGE_SCRATCH,
        compiler_params=pltpu.CompilerParams(
            dimension_semantics=("arbitrary", "arbitrary"), vmem_limit_bytes=VMEM_LIMIT),
        name="inproj",
    )(x, g, w_in, seg, qg, kg)


def _attn_kernel(lamv_ref, subln_ref, q_ref, k_ref, vt_ref, bias_ref, o_ref,
                 *, lam_init):
    lane = lax.broadcasted_iota(jnp.int32, (TQ, V_DIM), 1)
    ones = jnp.ones((ONES_ROWS, TQ), BF16)
    lv = lamv_ref[...]
    lam = (jnp.exp(jnp.sum(lv[0:1] * lv[1:2], axis=-1, keepdims=True))
           - jnp.exp(jnp.sum(lv[2:3] * lv[3:4], axis=-1, keepdims=True)) + lam_init)

    n_q = k_ref.shape[0] // TQ
    tasks = [(qi, j, mp) for qi in range(n_q) for j in range(qi + 1) for mp in range(2)]
    qms, m, acc = {}, {}, {}

    def scores(qi, j, mp):
        if qi not in qms:
            q = q_ref[qi * TQ:(qi + 1) * TQ, :]
            zero = jnp.zeros_like(q)
            qms[qi] = (jnp.where(lane < HEAD_DIM, q, zero), jnp.where(lane >= HEAD_DIM, q, zero))
        st = _dot_nt(k_ref[j * TQ:(j + 1) * TQ, :], qms[qi][mp])
        bias = {qi: 1, qi - 1: 0}.get(j)
        return st if bias is None else st + bias_ref[bias]

    def softmax_pv(qi, j, mp, st):
        v1t = jnp.concatenate([vt_ref[:, j * TQ:(j + 1) * TQ], ones], axis=0)
        m_cur = jnp.max(st, axis=0, keepdims=True)
        m_new = m_cur if j == 0 else jnp.maximum(m[qi, mp], m_cur)
        p = jnp.exp2(st - m_new)
        pv = _dot(v1t, p.astype(BF16))
        acc[qi, mp] = pv if j == 0 else jnp.exp2(m[qi, mp] - m_new) * acc[qi, mp] + pv
        m[qi, mp] = m_new
        if j == qi and mp == 1:
            a0, a1 = acc.pop((qi, 0)), acc.pop((qi, 1))
            o_t = a0[:V_DIM] / a0[V_DIM:V_DIM + 1] - lam * (a1[:V_DIM] / a1[V_DIM:V_DIM + 1])
            o_n = (o_t * lax.rsqrt(jnp.mean(o_t * o_t, axis=0, keepdims=True) + EPS)).T
            o_ref[qi * TQ:(qi + 1) * TQ, :] = (
                o_n * subln_ref[...] * (1.0 - lam_init)).astype(BF16)

    pending = {}
    for i in range(len(tasks) + SCORE_LOOKAHEAD):
        if i < len(tasks):
            pending[i] = scores(*tasks[i])
        if i >= SCORE_LOOKAHEAD:
            softmax_pv(*tasks[i - SCORE_LOOKAHEAD], pending.pop(i - SCORE_LOOKAHEAD))


def _attention(lamv, subln, q, k, vt, bias_tiles, lam_init, layer):
    B, S, _ = q.shape
    head = pl.BlockSpec((None, S, V_DIM), lambda b, h: (b, 0, h))
    head_t = pl.BlockSpec((None, V_DIM, S), lambda b, h: (b, h, 0))
    return pl.pallas_call(
        functools.partial(_attn_kernel, lam_init=lam_init),
        grid=(B, N_HEADS),
        in_specs=[_per_layer(lamv, layer), _per_layer(subln, layer), head, head, head_t,
                  pl.BlockSpec((None, 2, TQ, TQ), lambda b, h: (h, 0, 0, 0))],
        out_specs=head,
        out_shape=jax.ShapeDtypeStruct((B, S, ATTN_WIDTH), BF16),
        compiler_params=pltpu.CompilerParams(
            dimension_semantics=("parallel", "parallel"), vmem_limit_bytes=VMEM_LIMIT),
        name="diff_attention",
    )(lamv, subln, q, k, vt, bias_tiles)


def _ssm_kernel(u_ref, perm_ref, permt_ref, wb_ref, a_ref, c_ref, dskip_ref, wglu_ref,
                out_ref, bur, bui, xr_s, xi_s):
    @pl.when(pl.program_id(0) == 0)
    def _():
        xr_s[...] = jnp.zeros(xr_s.shape, F32)
        xi_s[...] = jnp.zeros(xi_s.shape, F32)


    def to_time_major(c):
        u_bt = u_ref[:, c * SCAN_STEPS:(c + 1) * SCAN_STEPS, :]
        return _dot(perm_ref[...], u_bt.reshape(SCAN_STEPS * SUBLANES, SSM_WIDTH))

    def b_project(c, u):
        ub = u.astype(BF16)
        for j in range(STATE_LANES // MXU_TILE):
            kt = (j * MXU_TILE // SSM_STATE * SSM_GROUP) // MXU_TILE
            rows = slice(kt * MXU_TILE, (kt + 1) * MXU_TILE)
            cols = slice(j * MXU_TILE, (j + 1) * MXU_TILE)
            bur[c, :, cols] = _dot(ub[:, rows], wb_ref[0, rows, cols])
            bui[c, :, cols] = _dot(ub[:, rows], wb_ref[1, rows, cols])

    def scan(c):
        for s in range(STATE_LANES // SCAN_STRIP):
            sl = slice(s * SCAN_STRIP, (s + 1) * SCAN_STRIP)
            a_r = a_ref[0, :, sl]
            a_i = a_ref[1, :, sl]
            xr = xr_s[:, sl]
            xi = xi_s[:, sl]
            for t in range(SCAN_STEPS):
                step = slice(t * SUBLANES, (t + 1) * SUBLANES)
                xr, xi = (a_r * xr - a_i * xi + bur[c, step, sl],
                          a_r * xi + a_i * xr + bui[c, step, sl])
                bur[c, step, sl] = xr
                bui[c, step, sl] = xi
            xr_s[:, sl] = xr
            xi_s[:, sl] = xi

    def c_project(c, u):
        halves = []
        k_per_half = STATE_LANES // 2
        n_per_half = SSM_WIDTH // 2
        for m in range(2):
            rows = slice(m * k_per_half, (m + 1) * k_per_half)
            cols = slice(m * n_per_half, (m + 1) * n_per_half)
            halves.append(_dot(bur[c, :, rows].astype(BF16), c_ref[0, rows, cols])
                          + _dot(bui[c, :, rows].astype(BF16), c_ref[1, rows, cols]))
        return jax.nn.gelu(jnp.concatenate(halves, axis=-1) + dskip_ref[...] * u)

    def to_batch_major(c, s):
        s_bt = _dot(permt_ref[...], s.astype(BF16)).astype(BF16)
        out_ref[:, c * SCAN_STEPS:(c + 1) * SCAN_STEPS, :] = s_bt.reshape(
            SUBLANES, SCAN_STEPS, SSM_WIDTH)

    chunks = range(SCAN_CHUNKS)
    us = [to_time_major(c) for c in chunks]
    for c in chunks:
        b_project(c, us[c])
    for c in chunks:
        scan(c)
    ss = [c_project(c, us[c]) for c in chunks]
    ss = [s * jax.nn.sigmoid(_dot(s.astype(BF16), wglu_ref[...])) for s in ss]
    for c in chunks:
        to_batch_major(c, ss[c])


def _ssm(u, wb, a, c, dskip, wglu, layer):
    rows = SCAN_STEPS * SUBLANES
    B, S, _ = u.shape
    r = jnp.arange(rows)
    perm = (jnp.arange(rows)[None, :] == ((r % B) * SCAN_STEPS + r // B)[:, None]).astype(BF16)
    blk = pl.BlockSpec((B, SCAN_CHUNKS * SCAN_STEPS, SSM_WIDTH), lambda c: (0, c, 0))
    return pl.pallas_call(
        _ssm_kernel,
        grid=(S // (SCAN_CHUNKS * SCAN_STEPS),),
        in_specs=[blk, _resident(perm.shape), _resident(perm.shape),
                  _per_layer(wb, layer), _per_layer(a, layer), _per_layer(c, layer),
                  _per_layer(dskip, layer), _per_layer(wglu, layer)],
        out_specs=blk,
        out_shape=jax.ShapeDtypeStruct((B, S, SSM_WIDTH), BF16),
        scratch_shapes=[pltpu.VMEM((SCAN_CHUNKS, rows, STATE_LANES), F32),
                        pltpu.VMEM((SCAN_CHUNKS, rows, STATE_LANES), F32),
                        pltpu.VMEM((SUBLANES, STATE_LANES), F32),
                        pltpu.VMEM((SUBLANES, STATE_LANES), F32)],
        compiler_params=pltpu.CompilerParams(
            dimension_semantics=("arbitrary",), vmem_limit_bytes=VMEM_LIMIT),
        name="s5_branch",
    )(u, perm, perm.T, wb, a, c, dskip, wglu)


def _merge_ffn_kernel(x_ref, o_ref, s_ref, g1_ref, g2_ref,
                      w_in_hbm, wa_hbm, wb_hbm, wo_hbm, w1_hbm, w3_hbm, w2_hbm, out_ref,
                      wg_ref, wa_ref, wb_ref, wo_ref, w1_ref, w3_ref, w2_ref,
                      acc_s, stage, sems, *, layer):
    @pl.when((pl.program_id(0) == 0) & (pl.program_id(1) == 0))
    def _():
        gate_col0 = w_in_hbm.shape[2] - wg_ref.shape[1]
        _stage_bf16([(w_in_hbm.at[layer], gate_col0, wg_ref), (wa_hbm.at[layer], 0, wa_ref),
                     (wb_hbm.at[layer], 0, wb_ref), (wo_hbm.at[layer], 0, wo_ref),
                     (w1_hbm.at[layer], 0, w1_ref), (w3_hbm.at[layer], 0, w3_ref),
                     (w2_hbm.at[layer], 0, w2_ref)], stage, sems)

    x = x_ref[...]
    h = _rms(x, g1_ref[...]).astype(BF16)
    mixed = jax.nn.sigmoid(_dot(h, wg_ref[:, 0:D_MODEL])) * _dot(o_ref[...], wa_ref[...])
    mixed += jax.nn.sigmoid(_dot(h, wg_ref[:, D_MODEL:])) * _dot(s_ref[...], wb_ref[...])
    x1 = x + _dot(mixed.astype(BF16), wo_ref[...])

    h2 = _rms(x1, g2_ref[...]).astype(BF16)
    acc_s[...] = x1

    for c in range(D_FF // FF_CHUNK):
        cols = slice(c * FF_CHUNK, (c + 1) * FF_CHUNK)
        t = jax.nn.silu(_dot(h2, w1_ref[:, cols])) * _dot(h2, w3_ref[:, cols])
        acc_s[...] += _dot(t.astype(BF16), w2_ref[cols, :])
    out_ref[...] = acc_s[...]


def _merge_ffn(x, o, s2, g1, g2, w_in, w_a, w_b, w_o, w1, w3, w2, layer):
    B, S, D = x.shape
    tok = lambda width: pl.BlockSpec((None, TM_FFN, width), lambda b, i: (b, i, 0))
    bf16_weight = lambda shape: pltpu.VMEM(shape, BF16)
    return pl.pallas_call(
        functools.partial(_merge_ffn_kernel, layer=layer),
        grid=(B, S // TM_FFN),
        in_specs=[tok(D), tok(ATTN_WIDTH), tok(SSM_WIDTH),
                  _per_layer(g1, layer), _per_layer(g2, layer)] + [_HBM] * 7,
        out_specs=tok(D),
        out_shape=jax.ShapeDtypeStruct((B, S, D), F32),
        scratch_shapes=[bf16_weight((D, 2 * D)), bf16_weight(w_a.shape[1:]),
                        bf16_weight(w_b.shape[1:]), bf16_weight(w_o.shape[1:]),
                        bf16_weight(w1.shape[1:]), bf16_weight(w3.shape[1:]),
                        bf16_weight(w2.shape[1:]),
                        pltpu.VMEM((TM_FFN, D), F32)] + _STAGE_SCRATCH,
        compiler_params=pltpu.CompilerParams(
            dimension_semantics=("arbitrary", "arbitrary"), vmem_limit_bytes=VMEM_LIMIT),
        name="merge_ffn",
    )(x, o, s2, g1, g2, w_in, w_a, w_b, w_o, w1, w3, w2)


def _t5_bucket(n):
    max_exact = N_BUCKETS // 2
    is_small = n < max_exact
    nf = jnp.maximum(n, 1).astype(F32)
    large = max_exact + (jnp.log(nf / max_exact) / math.log(MAX_DISTANCE / max_exact)
                         * (N_BUCKETS - max_exact)).astype(jnp.int32)
    large = jnp.minimum(large, N_BUCKETS - 1)
    return jnp.where(is_small, n, large)


def _ssm_params(lam_re, lam_im, b_re, b_im, c_re, c_im, log_step):
    step = jnp.exp(log_step.astype(F32))[..., None]
    lr = lam_re.astype(F32)
    li = lam_im.astype(F32)
    decay = jnp.exp(lr * step)
    ab_re = decay * jnp.cos(li * step)
    ab_im = decay * jnp.sin(li * step)
    nr = ab_re - 1.0
    ni = ab_im
    den = lr * lr + li * li
    f_re = (nr * lr + ni * li) / den
    f_im = (ni * lr - nr * li) / den
    br = b_re.astype(F32)
    bi = b_im.astype(F32)
    bb_re = f_re[..., None] * br - f_im[..., None] * bi
    bb_im = f_re[..., None] * bi + f_im[..., None] * br

    def blockdiag(w):
        *lead, g, n, m = w.shape
        stacked = jnp.swapaxes(w, -1, -2).reshape(*lead, g * m, n)
        same = (jnp.arange(g * m)[:, None] // m) == (jnp.arange(g * n)[None, :] // n)
        return jnp.where(same, jnp.tile(stacked, (1,) * len(lead) + (1, g)), 0.0).astype(BF16)

    depth = lr.shape[0]
    a = jnp.stack([ab_re, ab_im], axis=1).reshape(depth, 2, 1, STATE_LANES)
    return (blockdiag(jnp.stack([bb_re, bb_im], axis=1)),
            jnp.broadcast_to(a, (depth, 2, SUBLANES, STATE_LANES)),
            blockdiag(jnp.stack([c_re.astype(F32), -c_im.astype(F32)], axis=1)))


def kernel(x, rel_bias, norm_mix, w_in, q_gain, k_gain, lambda_q1, lambda_k1, lambda_q2, lambda_k2, subln, w_a, lam_re, lam_im, b_re, b_im, c_re, c_im, d_skip, log_step, w_glu, w_b, w_o, norm_ffn, w1, w3, w2):
    B, S, D = x.shape
    depth = w_in.shape[0]
    assert B == SUBLANES and D == D_MODEL

    table = rel_bias.astype(F32)
    dist = (-jnp.arange(MAX_DISTANCE)) % MAX_DISTANCE
    brev = (table[_t5_bucket(dist)] - table[N_BUCKETS - 1][None]).T * LOG2E
    bias_tiles = _bias_tiles(brev[:, None, :])

    seg = (jnp.arange(MXU_TILE)[:, None] // HEAD_DIM
           == jnp.arange(MXU_TILE)[None, :] // HEAD_DIM).astype(BF16)

    row = lambda p: p.astype(F32)[:, None, :]
    heads = QK_WIDTH // HEAD_DIM
    qg = row(jnp.tile(q_gain, (1, heads))) * (HEAD_DIM ** -0.5 * LOG2E)
    kg = row(jnp.tile(k_gain, (1, heads)))
    lamv = jnp.stack([lambda_q1, lambda_k1, lambda_q2, lambda_k2], axis=1).astype(F32)
    wb, a, c = _ssm_params(lam_re, lam_im, b_re, b_im, c_re, c_im, log_step)
    g_mix, g_ffn, g_sub, dskip = row(norm_mix), row(norm_ffn), row(subln), row(d_skip)
    wglu = w_glu.astype(BF16)

    for l in range(depth):
        lam_init = 0.8 - 0.6 * math.exp(-0.3 * l)
        q, k, vt, u = _inproj(x, g_mix, w_in, l, seg, qg, kg)
        o = _attention(lamv, g_sub, q, k, vt, bias_tiles, lam_init, l)
        s2 = _ssm(u, wb, a, c, dskip, wglu, l)
        x = _merge_ffn(x, o, s2, g_mix, g_ffn, w_in, w_a, w_b, w_o, w1, w3, w2, l)
    return x
```

```python
import functools
import math

import jax
import jax.numpy as jnp
from jax import lax
from jax.experimental import pallas as pl
from jax.experimental.pallas import tpu as pltpu

D_MODEL = 1024
N_HEADS = 4
HEAD_DIM = 64
V_DIM = 2 * HEAD_DIM
QK_WIDTH = N_HEADS * 2 * HEAD_DIM
ATTN_WIDTH = N_HEADS * V_DIM
SSM_WIDTH = D_MODEL // 2
SSM_GROUP = 16
SSM_GROUPS = SSM_WIDTH // SSM_GROUP
SSM_STATE = 64
STATE_LANES = SSM_GROUPS * SSM_STATE
D_FF = 2816
N_BUCKETS = 32
MAX_DISTANCE = 128
EPS = 1e-6
NEG = -1e30

MXU_TILE = 256
SUBLANES = 8

TM_PROJ = 1024
TM_FFN = 1024
TQ = 256
ONES_ROWS = 16
SCORE_LOOKAHEAD = 8
LOG2E = math.log2(math.e)
SCAN_STEPS = 64
SCAN_CHUNKS = 2
PERM_STEPS = 16
SCAN_STRIP = 512
FF_CHUNK = 256
VMEM_LIMIT = 62 * 1024 * 1024

BF16 = jnp.bfloat16
F32 = jnp.float32


def _dot(a, b):
    return jnp.dot(a, b, preferred_element_type=F32)


def _dot_nt(a, b):
    return lax.dot_general(a, b, (((1,), (1,)), ((), ())), preferred_element_type=F32)


def _rms(x, g):
    return x * lax.rsqrt(jnp.mean(x * x, axis=-1, keepdims=True) + EPS) * g


def _resident(shape):
    zeros = (0,) * len(shape)
    return pl.BlockSpec(shape, lambda *_: zeros, pipeline_mode=pl.Buffered(1))


def _per_layer(arr, layer):
    zeros = (0,) * (arr.ndim - 1)
    return pl.BlockSpec((None,) + arr.shape[1:], lambda *_: (layer,) + zeros,
                        pipeline_mode=pl.Buffered(1))


STAGE_ROWS = 128
STAGE_COLS = 1024
STAGE_DEPTH = 8
_STAGE_SCRATCH = [pltpu.VMEM((STAGE_DEPTH, STAGE_ROWS, STAGE_COLS), F32),
                  pltpu.SemaphoreType.DMA((STAGE_DEPTH,))]
_HBM = pl.BlockSpec(memory_space=pl.ANY)


def _stage_bf16(jobs, stage, sems):
    pieces = []
    for src, col0, dst in jobs:
        n_rows, n_cols = dst.shape
        for r0 in range(0, n_rows, STAGE_ROWS):
            for c0 in range(0, n_cols, STAGE_COLS):
                pieces.append((src, col0, dst, r0, c0, min(STAGE_COLS, n_cols - c0)))

    def copy(i):
        src, col0, _, r0, c0, nc = pieces[i]
        slot = i % STAGE_DEPTH
        return pltpu.make_async_copy(
            src.at[pl.ds(r0, STAGE_ROWS), pl.ds(col0 + c0, nc)],
            stage.at[slot, :, pl.ds(0, nc)], sems.at[slot])

    for i in range(min(STAGE_DEPTH, len(pieces))):
        copy(i).start()
    for i, (_, _, dst, r0, c0, nc) in enumerate(pieces):
        copy(i).wait()
        dst[r0:r0 + STAGE_ROWS, c0:c0 + nc] = stage[i % STAGE_DEPTH, :, 0:nc].astype(BF16)
        if i + STAGE_DEPTH < len(pieces):
            copy(i + STAGE_DEPTH).start()


def _bias_tiles_kernel(brev_ref, out_ref):
    sub = MAX_DISTANCE
    r = lax.broadcasted_iota(jnp.int32, (sub, sub), 0)
    c = lax.broadcasted_iota(jnp.int32, (sub, sub), 1)
    y = pltpu.roll(jnp.broadcast_to(brev_ref[...], (sub, sub)), 0, 1, stride=1, stride_axis=0)
    by_diag = {0: jnp.where(c <= r, y, NEG).T, 1: jnp.where(c > r, y, 0.0).T}
    for t, off in enumerate((TQ, 0)):
        for a in range(TQ // sub):
            for b in range(TQ // sub):
                d = a - b + off // sub
                fill = NEG if d < 0 else 0.0
                out_ref[t, b * sub:(b + 1) * sub, a * sub:(a + 1) * sub] = by_diag.get(
                    d, jnp.full((sub, sub), fill, F32))


def _bias_tiles(brev):
    return pl.pallas_call(
        _bias_tiles_kernel,
        grid=(N_HEADS,),
        in_specs=[pl.BlockSpec((None, 1, MAX_DISTANCE), lambda h: (h, 0, 0))],
        out_specs=pl.BlockSpec((None, 2, TQ, TQ), lambda h: (h, 0, 0, 0)),
        out_shape=jax.ShapeDtypeStruct((N_HEADS, 2, TQ, TQ), F32),
        name="bias_tiles",
    )(brev)


def _inproj_kernel(x_ref, g_ref, w_hbm, seg_ref, qg_ref, kg_ref,
                   q_ref, k_ref, vt_ref, u_ref, w_ref, stage, sems, *, layer):
    @pl.when((pl.program_id(0) == 0) & (pl.program_id(1) == 0))
    def _():
        _stage_bf16([(w_hbm.at[layer], 0, w_ref)], stage, sems)

    h = _rms(x_ref[...], g_ref[...]).astype(BF16)

    def qk_norm(z, gain):
        z2 = (z * z).astype(BF16)
        ss = jnp.concatenate(
            [_dot(z2[:, t * MXU_TILE:(t + 1) * MXU_TILE], seg_ref[...])
             for t in range(QK_WIDTH // MXU_TILE)], axis=1)
        return z * lax.rsqrt(ss * (1.0 / HEAD_DIM) + EPS) * gain

    zq = _dot(h, w_ref[:, 0:QK_WIDTH])
    zk = _dot(h, w_ref[:, QK_WIDTH:2 * QK_WIDTH])
    q_ref[...] = qk_norm(zq, qg_ref[...]).astype(BF16)
    v = _dot(h, w_ref[:, 2 * QK_WIDTH:2 * QK_WIDTH + ATTN_WIDTH])
    k_ref[...] = qk_norm(zk, kg_ref[...]).astype(BF16)
    u_ref[...] = _dot(h, w_ref[:, 2 * QK_WIDTH + ATTN_WIDTH:]).astype(BF16)
    vt_ref[...] = v.T.astype(BF16)


def _inproj(x, g, w_in, layer, seg, qg, kg):
    B, S, D = x.shape
    n_cols = 2 * QK_WIDTH + ATTN_WIDTH + SSM_WIDTH
    tok = lambda width: pl.BlockSpec((None, TM_PROJ, width), lambda b, i: (b, i, 0))
    return pl.pallas_call(
        functools.partial(_inproj_kernel, layer=layer),
        grid=(B, S // TM_PROJ),
        in_specs=[tok(D), _per_layer(g, layer), _HBM, _resident(seg.shape),
                  _per_layer(qg, layer), _per_layer(kg, layer)],
        out_specs=[tok(QK_WIDTH), tok(QK_WIDTH),
                   pl.BlockSpec((None, ATTN_WIDTH, TM_PROJ), lambda b, i: (b, 0, i)),
                   tok(SSM_WIDTH)],
        out_shape=[jax.ShapeDtypeStruct((B, S, QK_WIDTH), BF16),
                   jax.ShapeDtypeStruct((B, S, QK_WIDTH), BF16),
                   jax.ShapeDtypeStruct((B, ATTN_WIDTH, S), BF16),
                   jax.ShapeDtypeStruct((B, S, SSM_WIDTH), BF16)],
        scratch_shapes=[pltpu.VMEM((D, n_cols), BF16)] + _STAGE_SCRATCH,
        compiler_params=pltpu.CompilerParams(
            dimension_semantics=("arbitrary", "arbitrary"), vmem_limit_bytes=VMEM_LIMIT),
        name="inproj",
    )(x, g, w_in, seg, qg, kg)


def _attn_kernel(lamv_ref, subln_ref, q_ref, k_ref, vt_ref, bias_ref, o_ref,
                 *, lam_init):
    lane = lax.broadcasted_iota(jnp.int32, (TQ, V_DIM), 1)
    ones = jnp.ones((ONES_ROWS, TQ), BF16)
    lv = lamv_ref[...]
    lam = (jnp.exp(jnp.sum(lv[0:1] * lv[1:2], axis=-1, keepdims=True))
           - jnp.exp(jnp.sum(lv[2:3] * lv[3:4], axis=-1, keepdims=True)) + lam_init)

    n_q = k_ref.shape[0] // TQ
    tasks = [(qi, j, mp) for qi in range(n_q) for j in range(qi + 1) for mp in range(2)]
    qms, m, acc = {}, {}, {}

    def scores(qi, j, mp):
        if qi not in qms:
            q = q_ref[qi * TQ:(qi + 1) * TQ, :]
            zero = jnp.zeros_like(q)
            qms[qi] = (jnp.where(lane < HEAD_DIM, q, zero), jnp.where(lane >= HEAD_DIM, q, zero))
        st = _dot_nt(k_ref[j * TQ:(j + 1) * TQ, :], qms[qi][mp])
        bias = {qi: 1, qi - 1: 0}.get(j)
        return st if bias is None else st + bias_ref[bias]

    def softmax_pv(qi, j, mp, st):
        v1t = jnp.concatenate([vt_ref[:, j * TQ:(j + 1) * TQ], ones], axis=0)
        m_cur = jnp.max(st, axis=0, keepdims=True)
        m_new = m_cur if j == 0 else jnp.maximum(m[qi, mp], m_cur)
        p = jnp.exp2(st - m_new)
        pv = _dot(v1t, p.astype(BF16))
        acc[qi, mp] = pv if j == 0 else jnp.exp2(m[qi, mp] - m_new) * acc[qi, mp] + pv
        m[qi, mp] = m_new
        if j == qi and mp == 1:
            a0, a1 = acc.pop((qi, 0)), acc.pop((qi, 1))
            o_t = a0[:V_DIM] / a0[V_DIM:V_DIM + 1] - lam * (a1[:V_DIM] / a1[V_DIM:V_DIM + 1])
            o_n = (o_t * lax.rsqrt(jnp.mean(o_t * o_t, axis=0, keepdims=True) + EPS)).T
            o_ref[qi * TQ:(qi + 1) * TQ, :] = (
                o_n * subln_ref[...] * (1.0 - lam_init)).astype(BF16)

    pending = {}
    for i in range(len(tasks) + SCORE_LOOKAHEAD):
        if i < len(tasks):
            pending[i] = scores(*tasks[i])
        if i >= SCORE_LOOKAHEAD:
            softmax_pv(*tasks[i - SCORE_LOOKAHEAD], pending.pop(i - SCORE_LOOKAHEAD))


def _attention(lamv, subln, q, k, vt, bias_tiles, lam_init, layer):
    B, S, _ = q.shape
    head = pl.BlockSpec((None, S, V_DIM), lambda b, h: (b, 0, h))
    head_t = pl.BlockSpec((None, V_DIM, S), lambda b, h: (b, h, 0))
    return pl.pallas_call(
        functools.partial(_attn_kernel, lam_init=lam_init),
        grid=(B, N_HEADS),
        in_specs=[_per_layer(lamv, layer), _per_layer(subln, layer), head, head, head_t,
                  pl.BlockSpec((None, 2, TQ, TQ), lambda b, h: (h, 0, 0, 0))],
        out_specs=head,
        out_shape=jax.ShapeDtypeStruct((B, S, ATTN_WIDTH), BF16),
        compiler_params=pltpu.CompilerParams(
            dimension_semantics=("parallel", "parallel"), vmem_limit_bytes=VMEM_LIMIT),
        name="diff_attention",
    )(lamv, subln, q, k, vt, bias_tiles)


def _ssm_kernel(u_ref, perm_ref, permt_ref, wb_ref, a_ref, c_ref, dskip_ref, wglu_ref,
                out_ref, bur, bui, xr_s, xi_s):
    @pl.when(pl.program_id(0) == 0)
    def _():
        xr_s[...] = jnp.zeros(xr_s.shape, F32)
        xi_s[...] = jnp.zeros(xi_s.shape, F32)


    def to_time_major(c):
        blocks = []
        for t0 in range(c * SCAN_STEPS, (c + 1) * SCAN_STEPS, PERM_STEPS):
            rows_bt = jnp.concatenate(
                [u_ref[b, t0:t0 + PERM_STEPS, :] for b in range(SUBLANES)], axis=0)
            blocks.append(_dot(perm_ref[...], rows_bt))
        return jnp.concatenate(blocks, axis=0)

    def b_project(c, u):
        ub = u.astype(BF16)
        for j in range(STATE_LANES // MXU_TILE):
            kt = (j * MXU_TILE // SSM_STATE * SSM_GROUP) // MXU_TILE
            rows = slice(kt * MXU_TILE, (kt + 1) * MXU_TILE)
            cols = slice(j * MXU_TILE, (j + 1) * MXU_TILE)
            bur[c, :, cols] = _dot(ub[:, rows], wb_ref[0, rows, cols])
            bui[c, :, cols] = _dot(ub[:, rows], wb_ref[1, rows, cols])

    def scan(c):
        for s in range(STATE_LANES // SCAN_STRIP):
            sl = slice(s * SCAN_STRIP, (s + 1) * SCAN_STRIP)
            a_r = a_ref[0, :, sl]
            a_i = a_ref[1, :, sl]
            xr = xr_s[:, sl]
            xi = xi_s[:, sl]
            for t in range(SCAN_STEPS):
                step = slice(t * SUBLANES, (t + 1) * SUBLANES)
                xr, xi = (a_r * xr - a_i * xi + bur[c, step, sl],
                          a_r * xi + a_i * xr + bui[c, step, sl])
                bur[c, step, sl] = xr
                bui[c, step, sl] = xi
            xr_s[:, sl] = xr
            xi_s[:, sl] = xi

    def c_project(c, u):
        halves = []
        k_per_half = STATE_LANES // 2
        n_per_half = SSM_WIDTH // 2
        for m in range(2):
            rows = slice(m * k_per_half, (m + 1) * k_per_half)
            cols = slice(m * n_per_half, (m + 1) * n_per_half)
            halves.append(_dot(bur[c, :, rows].astype(BF16), c_ref[0, rows, cols])
                          + _dot(bui[c, :, rows].astype(BF16), c_ref[1, rows, cols]))
        return jax.nn.gelu(jnp.concatenate(halves, axis=-1) + dskip_ref[...] * u)

    def to_batch_major(c, s):
        sb = s.astype(BF16)
        rows_per_block = PERM_STEPS * SUBLANES
        for i, t0 in enumerate(range(c * SCAN_STEPS, (c + 1) * SCAN_STEPS, PERM_STEPS)):
            rows_bt = _dot(permt_ref[...], sb[i * rows_per_block:(i + 1) * rows_per_block])
            for b in range(SUBLANES):
                out_ref[b, t0:t0 + PERM_STEPS, :] = rows_bt[
                    b * PERM_STEPS:(b + 1) * PERM_STEPS].astype(BF16)

    chunks = range(SCAN_CHUNKS)
    us = [to_time_major(c) for c in chunks]
    for c in chunks:
        b_project(c, us[c])
    for c in chunks:
        scan(c)
    ss = [c_project(c, us[c]) for c in chunks]
    ss = [s * jax.nn.sigmoid(_dot(s.astype(BF16), wglu_ref[...])) for s in ss]
    for c in chunks:
        to_batch_major(c, ss[c])


def _ssm(u, wb, a, c, dskip, wglu, layer):
    rows = SCAN_STEPS * SUBLANES
    B, S, _ = u.shape
    r = jnp.arange(PERM_STEPS * B)
    perm = (r[None, :] == ((r % B) * PERM_STEPS + r // B)[:, None]).astype(BF16)
    blk = pl.BlockSpec((B, SCAN_CHUNKS * SCAN_STEPS, SSM_WIDTH), lambda c: (0, c, 0))
    return pl.pallas_call(
        _ssm_kernel,
        grid=(S // (SCAN_CHUNKS * SCAN_STEPS),),
        in_specs=[blk, _resident(perm.shape), _resident(perm.shape),
                  _per_layer(wb, layer), _per_layer(a, layer), _per_layer(c, layer),
                  _per_layer(dskip, layer), _per_layer(wglu, layer)],
        out_specs=blk,
        out_shape=jax.ShapeDtypeStruct((B, S, SSM_WIDTH), BF16),
        scratch_shapes=[pltpu.VMEM((SCAN_CHUNKS, rows, STATE_LANES), F32),
                        pltpu.VMEM((SCAN_CHUNKS, rows, STATE_LANES), F32),
                        pltpu.VMEM((SUBLANES, STATE_LANES), F32),
                        pltpu.VMEM((SUBLANES, STATE_LANES), F32)],
        compiler_params=pltpu.CompilerParams(
            dimension_semantics=("arbitrary",), vmem_limit_bytes=VMEM_LIMIT),
        name="s5_branch",
    )(u, perm, perm.T, wb, a, c, dskip, wglu)


def _merge_ffn_kernel(x_ref, o_ref, s_ref, g1_ref, g2_ref,
                      w_in_hbm, wa_hbm, wb_hbm, wo_hbm, w1_hbm, w3_hbm, w2_hbm, out_ref,
                      wg_ref, wa_ref, wb_ref, wo_ref, w1_ref, w3_ref, w2_ref,
                      acc_s, stage, sems, *, layer):
    @pl.when((pl.program_id(0) == 0) & (pl.program_id(1) == 0))
    def _():
        gate_col0 = w_in_hbm.shape[2] - wg_ref.shape[1]
        _stage_bf16([(w_in_hbm.at[layer], gate_col0, wg_ref), (wa_hbm.at[layer], 0, wa_ref),
                     (wb_hbm.at[layer], 0, wb_ref), (wo_hbm.at[layer], 0, wo_ref),
                     (w1_hbm.at[layer], 0, w1_ref), (w3_hbm.at[layer], 0, w3_ref),
                     (w2_hbm.at[layer], 0, w2_ref)], stage, sems)

    x = x_ref[...]
    h = _rms(x, g1_ref[...]).astype(BF16)
    mixed = jax.nn.sigmoid(_dot(h, wg_ref[:, 0:D_MODEL])) * _dot(o_ref[...], wa_ref[...])
    mixed += jax.nn.sigmoid(_dot(h, wg_ref[:, D_MODEL:])) * _dot(s_ref[...], wb_ref[...])
    x1 = x + _dot(mixed.astype(BF16), wo_ref[...])

    h2 = _rms(x1, g2_ref[...]).astype(BF16)
    acc_s[...] = x1

    for c in range(D_FF // FF_CHUNK):
        cols = slice(c * FF_CHUNK, (c + 1) * FF_CHUNK)
        t = jax.nn.silu(_dot(h2, w1_ref[:, cols])) * _dot(h2, w3_ref[:, cols])
        acc_s[...] += _dot(t.astype(BF16), w2_ref[cols, :])
    out_ref[...] = acc_s[...]


def _merge_ffn(x, o, s2, g1, g2, w_in, w_a, w_b, w_o, w1, w3, w2, layer):
    B, S, D = x.shape
    tok = lambda width: pl.BlockSpec((None, TM_FFN, width), lambda b, i: (b, i, 0))
    bf16_weight = lambda shape: pltpu.VMEM(shape, BF16)
    return pl.pallas_call(
        functools.partial(_merge_ffn_kernel, layer=layer),
        grid=(B, S // TM_FFN),
        in_specs=[tok(D), tok(ATTN_WIDTH), tok(SSM_WIDTH),
                  _per_layer(g1, layer), _per_layer(g2, layer)] + [_HBM] * 7,
        out_specs=tok(D),
        out_shape=jax.ShapeDtypeStruct((B, S, D), F32),
        scratch_shapes=[bf16_weight((D, 2 * D)), bf16_weight(w_a.shape[1:]),
                        bf16_weight(w_b.shape[1:]), bf16_weight(w_o.shape[1:]),
                        bf16_weight(w1.shape[1:]), bf16_weight(w3.shape[1:]),
                        bf16_weight(w2.shape[1:]),
                        pltpu.VMEM((TM_FFN, D), F32)] + _STAGE_SCRATCH,
        compiler_params=pltpu.CompilerParams(
            dimension_semantics=("arbitrary", "arbitrary"), vmem_limit_bytes=VMEM_LIMIT),
        name="merge_ffn",
    )(x, o, s2, g1, g2, w_in, w_a, w_b, w_o, w1, w3, w2)


def _t5_bucket(n):
    max_exact = N_BUCKETS // 2
    is_small = n < max_exact
    nf = jnp.maximum(n, 1).astype(F32)
    large = max_exact + (jnp.log(nf / max_exact) / math.log(MAX_DISTANCE / max_exact)
                         * (N_BUCKETS - max_exact)).astype(jnp.int32)
    large = jnp.minimum(large, N_BUCKETS - 1)
    return jnp.where(is_small, n, large)


def _ssm_params(lam_re, lam_im, b_re, b_im, c_re, c_im, log_step):
    step = jnp.exp(log_step.astype(F32))[..., None]
    lr = lam_re.astype(F32)
    li = lam_im.astype(F32)
    decay = jnp.exp(lr * step)
    ab_re = decay * jnp.cos(li * step)
    ab_im = decay * jnp.sin(li * step)
    nr = ab_re - 1.0
    ni = ab_im
    den = lr * lr + li * li
    f_re = (nr * lr + ni * li) / den
    f_im = (ni * lr - nr * li) / den
    br = b_re.astype(F32)
    bi = b_im.astype(F32)
    bb_re = f_re[..., None] * br - f_im[..., None] * bi
    bb_im = f_re[..., None] * bi + f_im[..., None] * br

    def blockdiag(w):
        *lead, g, n, m = w.shape
        stacked = jnp.swapaxes(w, -1, -2).reshape(*lead, g * m, n)
        same = (jnp.arange(g * m)[:, None] // m) == (jnp.arange(g * n)[None, :] // n)
        return jnp.where(same, jnp.tile(stacked, (1,) * len(lead) + (1, g)), 0.0).astype(BF16)

    depth = lr.shape[0]
    a = jnp.stack([ab_re, ab_im], axis=1).reshape(depth, 2, 1, STATE_LANES)
    return (blockdiag(jnp.stack([bb_re, bb_im], axis=1)),
            jnp.broadcast_to(a, (depth, 2, SUBLANES, STATE_LANES)),
            blockdiag(jnp.stack([c_re.astype(F32), -c_im.astype(F32)], axis=1)))


def kernel(x, rel_bias, norm_mix, w_in, q_gain, k_gain, lambda_q1, lambda_k1, lambda_q2, lambda_k2, subln, w_a, lam_re, lam_im, b_re, b_im, c_re, c_im, d_skip, log_step, w_glu, w_b, w_o, norm_ffn, w1, w3, w2):
    B, S, D = x.shape
    depth = w_in.shape[0]
    assert B == SUBLANES and D == D_MODEL

    table = rel_bias.astype(F32)
    dist = (-jnp.arange(MAX_DISTANCE)) % MAX_DISTANCE
    brev = (table[_t5_bucket(dist)] - table[N_BUCKETS - 1][None]).T * LOG2E
    bias_tiles = _bias_tiles(brev[:, None, :])

    seg = (jnp.arange(MXU_TILE)[:, None] // HEAD_DIM
           == jnp.arange(MXU_TILE)[None, :] // HEAD_DIM).astype(BF16)

    row = lambda p: p.astype(F32)[:, None, :]
    heads = QK_WIDTH // HEAD_DIM
    qg = row(jnp.tile(q_gain, (1, heads))) * (HEAD_DIM ** -0.5 * LOG2E)
    kg = row(jnp.tile(k_gain, (1, heads)))
    lamv = jnp.stack([lambda_q1, lambda_k1, lambda_q2, lambda_k2], axis=1).astype(F32)
    wb, a, c = _ssm_params(lam_re, lam_im, b_re, b_im, c_re, c_im, log_step)
    g_mix, g_ffn, g_sub, dskip = row(norm_mix), row(norm_ffn), row(subln), row(d_skip)
    wglu = w_glu.astype(BF16)

    for l in range(depth):
        lam_init = 0.8 - 0.6 * math.exp(-0.3 * l)
        q, k, vt, u = _inproj(x, g_mix, w_in, l, seg, qg, kg)
        o = _attention(lamv, g_sub, q, k, vt, bias_tiles, lam_init, l)
        s2 = _ssm(u, wb, a, c, dskip, wglu, l)
        x = _merge_ffn(x, o, s2, g_mix, g_ffn, w_in, w_a, w_b, w_o, w1, w3, w2, l)
    return x
```

```python
import functools
import math

import jax
import jax.numpy as jnp
from jax import lax
from jax.experimental import pallas as pl
from jax.experimental.pallas import tpu as pltpu

D_MODEL = 1024
N_HEADS = 4
HEAD_DIM = 64
V_DIM = 2 * HEAD_DIM
QK_WIDTH = N_HEADS * 2 * HEAD_DIM
ATTN_WIDTH = N_HEADS * V_DIM
SSM_WIDTH = D_MODEL // 2
SSM_GROUP = 16
SSM_GROUPS = SSM_WIDTH // SSM_GROUP
SSM_STATE = 64
STATE_LANES = SSM_GROUPS * SSM_STATE
D_FF = 2816
N_BUCKETS = 32
MAX_DISTANCE = 128
EPS = 1e-6
NEG = -1e30

MXU_TILE = 256
SUBLANES = 8

TM_PROJ = 1024
TM_FFN = 1024
TQ = 256
ONES_ROWS = 16
SCORE_LOOKAHEAD = 8
LOG2E = math.log2(math.e)
SCAN_STEPS = 64
SCAN_CHUNKS = 2
PERM_STEPS = 16
SCAN_STRIP = 512
FF_CHUNK = 256
VMEM_LIMIT = 62 * 1024 * 1024

BF16 = jnp.bfloat16
F32 = jnp.float32


def _dot(a, b):
    return jnp.dot(a, b, preferred_element_type=F32)


def _dot_nt(a, b):
    return lax.dot_general(a, b, (((1,), (1,)), ((), ())), preferred_element_type=F32)


def _rms(x, g):
    return x * lax.rsqrt(jnp.mean(x * x, axis=-1, keepdims=True) + EPS) * g


def _resident(shape):
    zeros = (0,) * len(shape)
    return pl.BlockSpec(shape, lambda *_: zeros, pipeline_mode=pl.Buffered(1))


def _per_layer(arr, layer):
    zeros = (0,) * (arr.ndim - 1)
    return pl.BlockSpec((None,) + arr.shape[1:], lambda *_: (layer,) + zeros,
                        pipeline_mode=pl.Buffered(1))


STAGE_ROWS = 128
STAGE_COLS = 1024
STAGE_DEPTH = 8
_STAGE_SCRATCH = [pltpu.VMEM((STAGE_DEPTH, STAGE_ROWS, STAGE_COLS), F32),
                  pltpu.SemaphoreType.DMA((STAGE_DEPTH,))]
_HBM = pl.BlockSpec(memory_space=pl.ANY)


def _stage_bf16(jobs, stage, sems):
    pieces = []
    for src, col0, dst in jobs:
        n_rows, n_cols = dst.shape
        for r0 in range(0, n_rows, STAGE_ROWS):
            for c0 in range(0, n_cols, STAGE_COLS):
                pieces.append((src, col0, dst, r0, c0, min(STAGE_COLS, n_cols - c0)))

    def copy(i):
        src, col0, _, r0, c0, nc = pieces[i]
        slot = i % STAGE_DEPTH
        return pltpu.make_async_copy(
            src.at[pl.ds(r0, STAGE_ROWS), pl.ds(col0 + c0, nc)],
            stage.at[slot, :, pl.ds(0, nc)], sems.at[slot])

    for i in range(min(STAGE_DEPTH, len(pieces))):
        copy(i).start()
    for i, (_, _, dst, r0, c0, nc) in enumerate(pieces):
        copy(i).wait()
        dst[r0:r0 + STAGE_ROWS, c0:c0 + nc] = stage[i % STAGE_DEPTH, :, 0:nc].astype(BF16)
        if i + STAGE_DEPTH < len(pieces):
            copy(i + STAGE_DEPTH).start()


def _bias_tiles_kernel(brev_ref, out_ref):
    sub = MAX_DISTANCE
    r = lax.broadcasted_iota(jnp.int32, (sub, sub), 0)
    c = lax.broadcasted_iota(jnp.int32, (sub, sub), 1)
    y = pltpu.roll(jnp.broadcast_to(brev_ref[...], (sub, sub)), 0, 1, stride=1, stride_axis=0)
    by_diag = {0: jnp.where(c <= r, y, NEG).T, 1: jnp.where(c > r, y, 0.0).T}
    for t, off in enumerate((TQ, 0)):
        for a in range(TQ // sub):
            for b in range(TQ // sub):
                d = a - b + off // sub
                fill = NEG if d < 0 else 0.0
                out_ref[t, b * sub:(b + 1) * sub, a * sub:(a + 1) * sub] = by_diag.get(
                    d, jnp.full((sub, sub), fill, F32))


def _bias_tiles(brev):
    return pl.pallas_call(
        _bias_tiles_kernel,
        grid=(N_HEADS,),
        in_specs=[pl.BlockSpec((None, 1, MAX_DISTANCE), lambda h: (h, 0, 0))],
        out_specs=pl.BlockSpec((None, 2, TQ, TQ), lambda h: (h, 0, 0, 0)),
        out_shape=jax.ShapeDtypeStruct((N_HEADS, 2, TQ, TQ), F32),
        name="bias_tiles",
    )(brev)


def _inproj_kernel(x_ref, g_ref, w_hbm, seg_ref, qg_ref, kg_ref,
                   q_ref, k_ref, vt_ref, u_ref, w_ref, stage, sems, *, layer):
    @pl.when((pl.program_id(0) == 0) & (pl.program_id(1) == 0))
    def _():
        _stage_bf16([(w_hbm.at[layer], 0, w_ref)], stage, sems)

    h = _rms(x_ref[...], g_ref[...]).astype(BF16)

    def qk_norm(z, gain):
        z2 = (z * z).astype(BF16)
        ss = jnp.concatenate(
            [_dot(z2[:, t * MXU_TILE:(t + 1) * MXU_TILE], seg_ref[...])
             for t in range(QK_WIDTH // MXU_TILE)], axis=1)
        return z * lax.rsqrt(ss * (1.0 / HEAD_DIM) + EPS) * gain

    zq = _dot(h, w_ref[:, 0:QK_WIDTH])
    zk = _dot(h, w_ref[:, QK_WIDTH:2 * QK_WIDTH])
    q_ref[...] = qk_norm(zq, qg_ref[...]).astype(BF16)
    v = _dot(h, w_ref[:, 2 * QK_WIDTH:2 * QK_WIDTH + ATTN_WIDTH])
    k_ref[...] = qk_norm(zk, kg_ref[...]).astype(BF16)
    u_ref[...] = _dot(h, w_ref[:, 2 * QK_WIDTH + ATTN_WIDTH:]).astype(BF16)
    vt_ref[...] = v.T.astype(BF16)


def _inproj(x, g, w_in, layer, seg, qg, kg):
    B, S, D = x.shape
    n_cols = 2 * QK_WIDTH + ATTN_WIDTH + SSM_WIDTH
    tok = lambda width: pl.BlockSpec((None, TM_PROJ, width), lambda b, i: (b, i, 0))
    return pl.pallas_call(
        functools.partial(_inproj_kernel, layer=layer),
        grid=(B, S // TM_PROJ),
        in_specs=[tok(D), _per_layer(g, layer), _HBM, _resident(seg.shape),
                  _per_layer(qg, layer), _per_layer(kg, layer)],
        out_specs=[tok(QK_WIDTH), tok(QK_WIDTH),
                   pl.BlockSpec((None, ATTN_WIDTH, TM_PROJ), lambda b, i: (b, 0, i)),
                   tok(SSM_WIDTH)],
        out_shape=[jax.ShapeDtypeStruct((B, S, QK_WIDTH), BF16),
                   jax.ShapeDtypeStruct((B, S, QK_WIDTH), BF16),
                   jax.ShapeDtypeStruct((B, ATTN_WIDTH, S), BF16),
                   jax.ShapeDtypeStruct((B, S, SSM_WIDTH), BF16)],
        scratch_shapes=[pltpu.VMEM((D, n_cols), BF16)] + _STAGE_SCRATCH,
        compiler_params=pltpu.CompilerParams(
            dimension_semantics=("arbitrary", "arbitrary"), vmem_limit_bytes=VMEM_LIMIT),
        name="inproj",
    )(x, g, w_in, seg, qg, kg)


def _attn_kernel(lamv_ref, subln_ref, q_ref, k_ref, vt_ref, bias_ref, o_ref,
                 *, lam_init):
    lane = lax.broadcasted_iota(jnp.int32, (TQ, V_DIM), 1)
    ones = jnp.ones((ONES_ROWS, TQ), BF16)
    lv = lamv_ref[...]
    lam = (jnp.exp(jnp.sum(lv[0:1] * lv[1:2], axis=-1, keepdims=True))
           - jnp.exp(jnp.sum(lv[2:3] * lv[3:4], axis=-1, keepdims=True)) + lam_init)

    n_q = k_ref.shape[0] // TQ
    tasks = [(qi, j, mp) for qi in range(n_q) for j in range(qi + 1) for mp in range(2)]
    qms, m, acc = {}, {}, {}

    def scores(qi, j, mp):
        if qi not in qms:
            q = q_ref[qi * TQ:(qi + 1) * TQ, :]
            zero = jnp.zeros_like(q)
            qms[qi] = (jnp.where(lane < HEAD_DIM, q, zero), jnp.where(lane >= HEAD_DIM, q, zero))
        st = _dot_nt(k_ref[j * TQ:(j + 1) * TQ, :], qms[qi][mp])
        bias = {qi: 1, qi - 1: 0}.get(j)
        return st if bias is None else st + bias_ref[bias]

    def softmax_pv(qi, j, mp, st):
        v1t = jnp.concatenate([vt_ref[:, j * TQ:(j + 1) * TQ], ones], axis=0)
        m_cur = jnp.max(st, axis=0, keepdims=True)
        m_new = m_cur if j == 0 else jnp.maximum(m[qi, mp], m_cur)
        p = jnp.exp2(st - m_new)
        pv = _dot(v1t, p.astype(BF16))
        acc[qi, mp] = pv if j == 0 else jnp.exp2(m[qi, mp] - m_new) * acc[qi, mp] + pv
        m[qi, mp] = m_new
        if j == qi and mp == 1:
            a0, a1 = acc.pop((qi, 0)), acc.pop((qi, 1))
            o_t = a0[:V_DIM] / a0[V_DIM:V_DIM + 1] - lam * (a1[:V_DIM] / a1[V_DIM:V_DIM + 1])
            o_n = (o_t * lax.rsqrt(jnp.mean(o_t * o_t, axis=0, keepdims=True) + EPS)).T
            o_ref[qi * TQ:(qi + 1) * TQ, :] = (
                o_n * subln_ref[...] * (1.0 - lam_init)).astype(BF16)

    pending = {}
    for i in range(len(tasks) + SCORE_LOOKAHEAD):
        if i < len(tasks):
            pending[i] = scores(*tasks[i])
        if i >= SCORE_LOOKAHEAD:
            softmax_pv(*tasks[i - SCORE_LOOKAHEAD], pending.pop(i - SCORE_LOOKAHEAD))


def _attention(lamv, subln, q, k, vt, bias_tiles, lam_init, layer):
    B, S, _ = q.shape
    head = pl.BlockSpec((None, S, V_DIM), lambda b, h: (b, 0, h))
    head_t = pl.BlockSpec((None, V_DIM, S), lambda b, h: (b, h, 0))
    return pl.pallas_call(
        functools.partial(_attn_kernel, lam_init=lam_init),
        grid=(B, N_HEADS),
        in_specs=[_per_layer(lamv, layer), _per_layer(subln, layer), head, head, head_t,
                  pl.BlockSpec((None, 2, TQ, TQ), lambda b, h: (h, 0, 0, 0))],
        out_specs=head,
        out_shape=jax.ShapeDtypeStruct((B, S, ATTN_WIDTH), BF16),
        compiler_params=pltpu.CompilerParams(
            dimension_semantics=("parallel", "parallel"), vmem_limit_bytes=VMEM_LIMIT),
        name="diff_attention",
    )(lamv, subln, q, k, vt, bias_tiles)


def _blockdiag_into(dst, stacked):
    (rows, n), cols = stacked.shape, dst.shape[1]
    m = rows // (cols // n)
    lane_in_block = lax.broadcasted_iota(jnp.int32, (n, cols), 1) % n
    tile_cols = (lane_in_block == lax.broadcasted_iota(jnp.int32, (n, cols), 0)).astype(BF16)
    tiled = _dot(stacked.astype(BF16), tile_cols)
    same_group = (lax.broadcasted_iota(jnp.int32, (rows, cols), 0) // m
                  == lax.broadcasted_iota(jnp.int32, (rows, cols), 1) // n)
    dst[...] = jnp.where(same_group, tiled, 0.0).astype(BF16)


def _ssm_kernel(u_ref, perm_ref, permt_ref, bb_ref, a_ref, cc_ref, dskip_ref, wglu_ref,
                out_ref, wb_ref, c_ref, bur, bui, xr_s, xi_s):
    @pl.when(pl.program_id(0) == 0)
    def _():
        xr_s[...] = jnp.zeros(xr_s.shape, F32)
        xi_s[...] = jnp.zeros(xi_s.shape, F32)
        for part in range(2):
            _blockdiag_into(wb_ref.at[part], bb_ref[part])
            _blockdiag_into(c_ref.at[part], cc_ref[part])


    def to_time_major(c):
        blocks = []
        for t0 in range(c * SCAN_STEPS, (c + 1) * SCAN_STEPS, PERM_STEPS):
            rows_bt = jnp.concatenate(
                [u_ref[b, t0:t0 + PERM_STEPS, :] for b in range(SUBLANES)], axis=0)
            blocks.append(_dot(perm_ref[...], rows_bt))
        return jnp.concatenate(blocks, axis=0)

    def b_project(c, u):
        ub = u.astype(BF16)
        for j in range(STATE_LANES // MXU_TILE):
            kt = (j * MXU_TILE // SSM_STATE * SSM_GROUP) // MXU_TILE
            rows = slice(kt * MXU_TILE, (kt + 1) * MXU_TILE)
            cols = slice(j * MXU_TILE, (j + 1) * MXU_TILE)
            bur[c, :, cols] = _dot(ub[:, rows], wb_ref[0, rows, cols])
            bui[c, :, cols] = _dot(ub[:, rows], wb_ref[1, rows, cols])

    def scan(c):
        for s in range(STATE_LANES // SCAN_STRIP):
            sl = slice(s * SCAN_STRIP, (s + 1) * SCAN_STRIP)
            a_r = a_ref[0, :, sl]
            a_i = a_ref[1, :, sl]
            xr = xr_s[:, sl]
            xi = xi_s[:, sl]
            for t in range(SCAN_STEPS):
                step = slice(t * SUBLANES, (t + 1) * SUBLANES)
                xr, xi = (a_r * xr - a_i * xi + bur[c, step, sl],
                          a_r * xi + a_i * xr + bui[c, step, sl])
                bur[c, step, sl] = xr
                bui[c, step, sl] = xi
            xr_s[:, sl] = xr
            xi_s[:, sl] = xi

    def c_project(c, u):
        halves = []
        k_per_half = STATE_LANES // 2
        n_per_half = SSM_WIDTH // 2
        for m in range(2):
            rows = slice(m * k_per_half, (m + 1) * k_per_half)
            cols = slice(m * n_per_half, (m + 1) * n_per_half)
            halves.append(_dot(bur[c, :, rows].astype(BF16), c_ref[0, rows, cols])
                          + _dot(bui[c, :, rows].astype(BF16), c_ref[1, rows, cols]))
        return jax.nn.gelu(jnp.concatenate(halves, axis=-1) + dskip_ref[...] * u)

    def to_batch_major(c, s):
        sb = s.astype(BF16)
        rows_per_block = PERM_STEPS * SUBLANES
        for i, t0 in enumerate(range(c * SCAN_STEPS, (c + 1) * SCAN_STEPS, PERM_STEPS)):
            rows_bt = _dot(permt_ref[...], sb[i * rows_per_block:(i + 1) * rows_per_block])
            for b in range(SUBLANES):
                out_ref[b, t0:t0 + PERM_STEPS, :] = rows_bt[
                    b * PERM_STEPS:(b + 1) * PERM_STEPS].astype(BF16)

    chunks = range(SCAN_CHUNKS)
    us = [to_time_major(c) for c in chunks]
    for c in chunks:
        b_project(c, us[c])
    for c in chunks:
        scan(c)
    ss = [c_project(c, us[c]) for c in chunks]
    ss = [s * jax.nn.sigmoid(_dot(s.astype(BF16), wglu_ref[...])) for s in ss]
    for c in chunks:
        to_batch_major(c, ss[c])


def _ssm(u, bb, a, cc, dskip, wglu, layer):
    rows = SCAN_STEPS * SUBLANES
    B, S, _ = u.shape
    r = jnp.arange(PERM_STEPS * B)
    perm = (r[None, :] == ((r % B) * PERM_STEPS + r // B)[:, None]).astype(BF16)
    blk = pl.BlockSpec((B, SCAN_CHUNKS * SCAN_STEPS, SSM_WIDTH), lambda c: (0, c, 0))
    return pl.pallas_call(
        _ssm_kernel,
        grid=(S // (SCAN_CHUNKS * SCAN_STEPS),),
        in_specs=[blk, _resident(perm.shape), _resident(perm.shape),
                  _per_layer(bb, layer), _per_layer(a, layer), _per_layer(cc, layer),
                  _per_layer(dskip, layer), _per_layer(wglu, layer)],
        out_specs=blk,
        out_shape=jax.ShapeDtypeStruct((B, S, SSM_WIDTH), BF16),
        scratch_shapes=[pltpu.VMEM((2, SSM_WIDTH, STATE_LANES), BF16),
                        pltpu.VMEM((2, STATE_LANES, SSM_WIDTH), BF16),
                        pltpu.VMEM((SCAN_CHUNKS, rows, STATE_LANES), F32),
                        pltpu.VMEM((SCAN_CHUNKS, rows, STATE_LANES), F32),
                        pltpu.VMEM((SUBLANES, STATE_LANES), F32),
                        pltpu.VMEM((SUBLANES, STATE_LANES), F32)],
        compiler_params=pltpu.CompilerParams(
            dimension_semantics=("arbitrary",), vmem_limit_bytes=VMEM_LIMIT),
        name="s5_branch",
    )(u, perm, perm.T, bb, a, cc, dskip, wglu)


def _merge_ffn_kernel(x_ref, o_ref, s_ref, g1_ref, g2_ref,
                      w_in_hbm, wa_hbm, wb_hbm, wo_hbm, w1_hbm, w3_hbm, w2_hbm, out_ref,
                      wg_ref, wa_ref, wb_ref, wo_ref, w1_ref, w3_ref, w2_ref,
                      acc_s, stage, sems, *, layer):
    @pl.when((pl.program_id(0) == 0) & (pl.program_id(1) == 0))
    def _():
        gate_col0 = w_in_hbm.shape[2] - wg_ref.shape[1]
        _stage_bf16([(w_in_hbm.at[layer], gate_col0, wg_ref), (wa_hbm.at[layer], 0, wa_ref),
                     (wb_hbm.at[layer], 0, wb_ref), (wo_hbm.at[layer], 0, wo_ref),
                     (w1_hbm.at[layer], 0, w1_ref), (w3_hbm.at[layer], 0, w3_ref),
                     (w2_hbm.at[layer], 0, w2_ref)], stage, sems)

    x = x_ref[...]
    h = _rms(x, g1_ref[...]).astype(BF16)
    mixed = jax.nn.sigmoid(_dot(h, wg_ref[:, 0:D_MODEL])) * _dot(o_ref[...], wa_ref[...])
    mixed += jax.nn.sigmoid(_dot(h, wg_ref[:, D_MODEL:])) * _dot(s_ref[...], wb_ref[...])
    x1 = x + _dot(mixed.astype(BF16), wo_ref[...])

    h2 = _rms(x1, g2_ref[...]).astype(BF16)
    acc_s[...] = x1

    for c in range(D_FF // FF_CHUNK):
        cols = slice(c * FF_CHUNK, (c + 1) * FF_CHUNK)
        t = jax.nn.silu(_dot(h2, w1_ref[:, cols])) * _dot(h2, w3_ref[:, cols])
        acc_s[...] += _dot(t.astype(BF16), w2_ref[cols, :])
    out_ref[...] = acc_s[...]


def _merge_ffn(x, o, s2, g1, g2, w_in, w_a, w_b, w_o, w1, w3, w2, layer):
    B, S, D = x.shape
    tok = lambda width: pl.BlockSpec((None, TM_FFN, width), lambda b, i: (b, i, 0))
    bf16_weight = lambda shape: pltpu.VMEM(shape, BF16)
    return pl.pallas_call(
        functools.partial(_merge_ffn_kernel, layer=layer),
        grid=(B, S // TM_FFN),
        in_specs=[tok(D), tok(ATTN_WIDTH), tok(SSM_WIDTH),
                  _per_layer(g1, layer), _per_layer(g2, layer)] + [_HBM] * 7,
        out_specs=tok(D),
        out_shape=jax.ShapeDtypeStruct((B, S, D), F32),
        scratch_shapes=[bf16_weight((D, 2 * D)), bf16_weight(w_a.shape[1:]),
                        bf16_weight(w_b.shape[1:]), bf16_weight(w_o.shape[1:]),
                        bf16_weight(w1.shape[1:]), bf16_weight(w3.shape[1:]),
                        bf16_weight(w2.shape[1:]),
                        pltpu.VMEM((TM_FFN, D), F32)] + _STAGE_SCRATCH,
        compiler_params=pltpu.CompilerParams(
            dimension_semantics=("arbitrary", "arbitrary"), vmem_limit_bytes=VMEM_LIMIT),
        name="merge_ffn",
    )(x, o, s2, g1, g2, w_in, w_a, w_b, w_o, w1, w3, w2)


def _t5_bucket(n):
    max_exact = N_BUCKETS // 2
    is_small = n < max_exact
    nf = jnp.maximum(n, 1).astype(F32)
    large = max_exact + (jnp.log(nf / max_exact) / math.log(MAX_DISTANCE / max_exact)
                         * (N_BUCKETS - max_exact)).astype(jnp.int32)
    large = jnp.minimum(large, N_BUCKETS - 1)
    return jnp.where(is_small, n, large)


def _ssm_params(lam_re, lam_im, b_re, b_im, c_re, c_im, log_step):
    step = jnp.exp(log_step.astype(F32))[..., None]
    lr = lam_re.astype(F32)
    li = lam_im.astype(F32)
    decay = jnp.exp(lr * step)
    ab_re = decay * jnp.cos(li * step)
    ab_im = decay * jnp.sin(li * step)
    nr = ab_re - 1.0
    ni = ab_im
    den = lr * lr + li * li
    f_re = (nr * lr + ni * li) / den
    f_im = (ni * lr - nr * li) / den
    br = b_re.astype(F32)
    bi = b_im.astype(F32)
    bb_re = f_re[..., None] * br - f_im[..., None] * bi
    bb_im = f_re[..., None] * bi + f_im[..., None] * br

    def stack_transposed(w):
        *lead, g, n, m = w.shape
        return jnp.swapaxes(w, -1, -2).reshape(*lead, g * m, n)

    depth = lr.shape[0]
    a = jnp.stack([ab_re, ab_im], axis=1).reshape(depth, 2, 1, STATE_LANES)
    return (stack_transposed(jnp.stack([bb_re, bb_im], axis=1)),
            jnp.broadcast_to(a, (depth, 2, SUBLANES, STATE_LANES)),
            stack_transposed(jnp.stack([c_re.astype(F32), -c_im.astype(F32)], axis=1)))


def kernel(x, rel_bias, norm_mix, w_in, q_gain, k_gain, lambda_q1, lambda_k1, lambda_q2, lambda_k2, subln, w_a, lam_re, lam_im, b_re, b_im, c_re, c_im, d_skip, log_step, w_glu, w_b, w_o, norm_ffn, w1, w3, w2):
    B, S, D = x.shape
    depth = w_in.shape[0]
    assert B == SUBLANES and D == D_MODEL

    table = rel_bias.astype(F32)
    dist = (-jnp.arange(MAX_DISTANCE)) % MAX_DISTANCE
    brev = (table[_t5_bucket(dist)] - table[N_BUCKETS - 1][None]).T * LOG2E
    bias_tiles = _bias_tiles(brev[:, None, :])

    seg = (jnp.arange(MXU_TILE)[:, None] // HEAD_DIM
           == jnp.arange(MXU_TILE)[None, :] // HEAD_DIM).astype(BF16)

    row = lambda p: p.astype(F32)[:, None, :]
    heads = QK_WIDTH // HEAD_DIM
    qg = row(jnp.tile(q_gain, (1, heads))) * (HEAD_DIM ** -0.5 * LOG2E)
    kg = row(jnp.tile(k_gain, (1, heads)))
    lamv = jnp.stack([lambda_q1, lambda_k1, lambda_q2, lambda_k2], axis=1).astype(F32)
    bb, a, cc = _ssm_params(lam_re, lam_im, b_re, b_im, c_re, c_im, log_step)
    g_mix, g_ffn, g_sub, dskip = row(norm_mix), row(norm_ffn), row(subln), row(d_skip)
    wglu = w_glu.astype(BF16)

    for l in range(depth):
        lam_init = 0.8 - 0.6 * math.exp(-0.3 * l)
        q, k, vt, u = _inproj(x, g_mix, w_in, l, seg, qg, kg)
        o = _attention(lamv, g_sub, q, k, vt, bias_tiles, lam_init, l)
        s2 = _ssm(u, bb, a, cc, dskip, wglu, l)
        x = _merge_ffn(x, o, s2, g_mix, g_ffn, w_in, w_a, w_b, w_o, w1, w3, w2, l)
    return x
```

```python
import functools
import math

import jax
import jax.numpy as jnp
from jax import lax
from jax.experimental import pallas as pl
from jax.experimental.pallas import tpu as pltpu

D_MODEL = 1024
N_HEADS = 4
HEAD_DIM = 64
V_DIM = 2 * HEAD_DIM
QK_WIDTH = N_HEADS * 2 * HEAD_DIM
ATTN_WIDTH = N_HEADS * V_DIM
SSM_WIDTH = D_MODEL // 2
SSM_GROUP = 16
SSM_GROUPS = SSM_WIDTH // SSM_GROUP
SSM_STATE = 64
STATE_LANES = SSM_GROUPS * SSM_STATE
D_FF = 2816
N_BUCKETS = 32
MAX_DISTANCE = 128
EPS = 1e-6
NEG = -1e30

MXU_TILE = 256
SUBLANES = 8

TM_PROJ = 1024
TM_FFN = 1024
TQ = 256
ONES_ROWS = 16
SCORE_LOOKAHEAD = 8
LOG2E = math.log2(math.e)
SCAN_STEPS = 64
SCAN_CHUNKS = 2
PERM_STEPS = 16
SCAN_STRIP = 512
FF_CHUNK = 256
VMEM_LIMIT = 62 * 1024 * 1024

BF16 = jnp.bfloat16
F32 = jnp.float32


def _dot(a, b):
    return jnp.dot(a, b, preferred_element_type=F32)


def _dot_nt(a, b):
    return lax.dot_general(a, b, (((1,), (1,)), ((), ())), preferred_element_type=F32)


def _rms(x, g):
    return x * lax.rsqrt(jnp.mean(x * x, axis=-1, keepdims=True) + EPS) * g


def _resident(shape):
    zeros = (0,) * len(shape)
    return pl.BlockSpec(shape, lambda *_: zeros, pipeline_mode=pl.Buffered(1))


def _per_layer(arr, layer):
    zeros = (0,) * (arr.ndim - 1)
    return pl.BlockSpec((None,) + arr.shape[1:], lambda *_: (layer,) + zeros,
                        pipeline_mode=pl.Buffered(1))


STAGE_ROWS = 128
STAGE_COLS = 1024
STAGE_DEPTH = 8
_STAGE_SCRATCH = [pltpu.VMEM((STAGE_DEPTH, STAGE_ROWS, STAGE_COLS), F32),
                  pltpu.SemaphoreType.DMA((STAGE_DEPTH,))]
_HBM = pl.BlockSpec(memory_space=pl.ANY)


def _stage_bf16(jobs, stage, sems):
    pieces = []
    for src, col0, dst in jobs:
        n_rows, n_cols = dst.shape
        for r0 in range(0, n_rows, STAGE_ROWS):
            for c0 in range(0, n_cols, STAGE_COLS):
                pieces.append((src, col0, dst, r0, c0, min(STAGE_COLS, n_cols - c0)))

    def copy(i):
        src, col0, _, r0, c0, nc = pieces[i]
        slot = i % STAGE_DEPTH
        return pltpu.make_async_copy(
            src.at[pl.ds(r0, STAGE_ROWS), pl.ds(col0 + c0, nc)],
            stage.at[slot, :, pl.ds(0, nc)], sems.at[slot])

    for i in range(min(STAGE_DEPTH, len(pieces))):
        copy(i).start()
    for i, (_, _, dst, r0, c0, nc) in enumerate(pieces):
        copy(i).wait()
        dst[r0:r0 + STAGE_ROWS, c0:c0 + nc] = stage[i % STAGE_DEPTH, :, 0:nc].astype(BF16)
        if i + STAGE_DEPTH < len(pieces):
            copy(i + STAGE_DEPTH).start()


def _bias_tiles_kernel(brev_ref, out_ref):
    sub = MAX_DISTANCE
    r = lax.broadcasted_iota(jnp.int32, (sub, sub), 0)
    c = lax.broadcasted_iota(jnp.int32, (sub, sub), 1)
    y = pltpu.roll(jnp.broadcast_to(brev_ref[...], (sub, sub)), 0, 1, stride=1, stride_axis=0)
    by_diag = {0: jnp.where(c <= r, y, NEG).T, 1: jnp.where(c > r, y, 0.0).T}
    for t, off in enumerate((TQ, 0)):
        for a in range(TQ // sub):
            for b in range(TQ // sub):
                d = a - b + off // sub
                fill = NEG if d < 0 else 0.0
                out_ref[t, b * sub:(b + 1) * sub, a * sub:(a + 1) * sub] = by_diag.get(
                    d, jnp.full((sub, sub), fill, F32))


def _bias_tiles(brev):
    return pl.pallas_call(
        _bias_tiles_kernel,
        grid=(N_HEADS,),
        in_specs=[pl.BlockSpec((None, 1, MAX_DISTANCE), lambda h: (h, 0, 0))],
        out_specs=pl.BlockSpec((None, 2, TQ, TQ), lambda h: (h, 0, 0, 0)),
        out_shape=jax.ShapeDtypeStruct((N_HEADS, 2, TQ, TQ), F32),
        name="bias_tiles",
    )(brev)


def _store_heads(ref, z):
    for hd in range(ref.shape[0]):
        ref[hd] = z[:, hd * V_DIM:(hd + 1) * V_DIM]


def _inproj_kernel(x_ref, g_ref, w_hbm, seg_ref, qg_ref, kg_ref,
                   q_ref, k_ref, vt_ref, u_ref, w_ref, stage, sems, *, layer):
    @pl.when((pl.program_id(0) == 0) & (pl.program_id(1) == 0))
    def _():
        _stage_bf16([(w_hbm.at[layer], 0, w_ref)], stage, sems)

    h = _rms(x_ref[...], g_ref[...]).astype(BF16)

    def qk_norm(z, gain):
        z2 = (z * z).astype(BF16)
        ss = jnp.concatenate(
            [_dot(z2[:, t * MXU_TILE:(t + 1) * MXU_TILE], seg_ref[...])
             for t in range(QK_WIDTH // MXU_TILE)], axis=1)
        return z * lax.rsqrt(ss * (1.0 / HEAD_DIM) + EPS) * gain

    zq = _dot(h, w_ref[:, 0:QK_WIDTH])
    zk = _dot(h, w_ref[:, QK_WIDTH:2 * QK_WIDTH])
    _store_heads(q_ref, qk_norm(zq, qg_ref[...]).astype(BF16))
    v = _dot(h, w_ref[:, 2 * QK_WIDTH:2 * QK_WIDTH + ATTN_WIDTH])
    _store_heads(k_ref, qk_norm(zk, kg_ref[...]).astype(BF16))
    u_ref[...] = _dot(h, w_ref[:, 2 * QK_WIDTH + ATTN_WIDTH:]).astype(BF16)
    vt_ref[...] = v.T.astype(BF16)


def _inproj(x, g, w_in, layer, seg, qg, kg):
    B, S, D = x.shape
    n_cols = 2 * QK_WIDTH + ATTN_WIDTH + SSM_WIDTH
    tok = lambda width: pl.BlockSpec((None, TM_PROJ, width), lambda b, i: (b, i, 0))
    heads = pl.BlockSpec((None, N_HEADS, TM_PROJ, V_DIM), lambda b, i: (b, 0, i, 0))
    return pl.pallas_call(
        functools.partial(_inproj_kernel, layer=layer),
        grid=(B, S // TM_PROJ),
        in_specs=[tok(D), _per_layer(g, layer), _HBM, _resident(seg.shape),
                  _per_layer(qg, layer), _per_layer(kg, layer)],
        out_specs=[heads, heads,
                   pl.BlockSpec((None, ATTN_WIDTH, TM_PROJ), lambda b, i: (b, 0, i)),
                   tok(SSM_WIDTH)],
        out_shape=[jax.ShapeDtypeStruct((B, N_HEADS, S, V_DIM), BF16),
                   jax.ShapeDtypeStruct((B, N_HEADS, S, V_DIM), BF16),
                   jax.ShapeDtypeStruct((B, ATTN_WIDTH, S), BF16),
                   jax.ShapeDtypeStruct((B, S, SSM_WIDTH), BF16)],
        scratch_shapes=[pltpu.VMEM((D, n_cols), BF16)] + _STAGE_SCRATCH,
        compiler_params=pltpu.CompilerParams(
            dimension_semantics=("arbitrary", "arbitrary"), vmem_limit_bytes=VMEM_LIMIT),
        name="inproj",
    )(x, g, w_in, seg, qg, kg)


def _attn_kernel(lamv_ref, subln_ref, q_ref, k_ref, vt_ref, bias_ref, o_ref,
                 *, lam_init):
    lane = lax.broadcasted_iota(jnp.int32, (TQ, V_DIM), 1)
    ones = jnp.ones((ONES_ROWS, TQ), BF16)
    lv = lamv_ref[...]
    lam = (jnp.exp(jnp.sum(lv[0:1] * lv[1:2], axis=-1, keepdims=True))
           - jnp.exp(jnp.sum(lv[2:3] * lv[3:4], axis=-1, keepdims=True)) + lam_init)

    n_q = k_ref.shape[0] // TQ
    tasks = [(qi, j, mp) for qi in range(n_q) for j in range(qi + 1) for mp in range(2)]
    qms, m, acc = {}, {}, {}

    def scores(qi, j, mp):
        if qi not in qms:
            q = q_ref[qi * TQ:(qi + 1) * TQ, :]
            zero = jnp.zeros_like(q)
            qms[qi] = (jnp.where(lane < HEAD_DIM, q, zero), jnp.where(lane >= HEAD_DIM, q, zero))
        st = _dot_nt(k_ref[j * TQ:(j + 1) * TQ, :], qms[qi][mp])
        bias = {qi: 1, qi - 1: 0}.get(j)
        return st if bias is None else st + bias_ref[bias]

    def softmax_pv(qi, j, mp, st):
        v1t = jnp.concatenate([vt_ref[:, j * TQ:(j + 1) * TQ], ones], axis=0)
        m_cur = jnp.max(st, axis=0, keepdims=True)
        m_new = m_cur if j == 0 else jnp.maximum(m[qi, mp], m_cur)
        p = jnp.exp2(st - m_new)
        pv = _dot(v1t, p.astype(BF16))
        acc[qi, mp] = pv if j == 0 else jnp.exp2(m[qi, mp] - m_new) * acc[qi, mp] + pv
        m[qi, mp] = m_new
        if j == qi and mp == 1:
            a0, a1 = acc.pop((qi, 0)), acc.pop((qi, 1))
            o_t = a0[:V_DIM] / a0[V_DIM:V_DIM + 1] - lam * (a1[:V_DIM] / a1[V_DIM:V_DIM + 1])
            o_n = (o_t * lax.rsqrt(jnp.mean(o_t * o_t, axis=0, keepdims=True) + EPS)).T
            o_ref[qi * TQ:(qi + 1) * TQ, :] = (
                o_n * subln_ref[...] * (1.0 - lam_init)).astype(BF16)

    pending = {}
    for i in range(len(tasks) + SCORE_LOOKAHEAD):
        if i < len(tasks):
            pending[i] = scores(*tasks[i])
        if i >= SCORE_LOOKAHEAD:
            softmax_pv(*tasks[i - SCORE_LOOKAHEAD], pending.pop(i - SCORE_LOOKAHEAD))


def _attention(lamv, subln, q, k, vt, bias_tiles, lam_init, layer):
    B, _, S, _ = q.shape
    head = pl.BlockSpec((None, None, S, V_DIM), lambda b, h: (b, h, 0, 0))
    head_t = pl.BlockSpec((None, V_DIM, S), lambda b, h: (b, h, 0))
    return pl.pallas_call(
        functools.partial(_attn_kernel, lam_init=lam_init),
        grid=(B, N_HEADS),
        in_specs=[_per_layer(lamv, layer), _per_layer(subln, layer), head, head, head_t,
                  pl.BlockSpec((None, 2, TQ, TQ), lambda b, h: (h, 0, 0, 0))],
        out_specs=head,
        out_shape=jax.ShapeDtypeStruct((B, N_HEADS, S, V_DIM), BF16),
        compiler_params=pltpu.CompilerParams(
            dimension_semantics=("parallel", "parallel"), vmem_limit_bytes=VMEM_LIMIT),
        name="diff_attention",
    )(lamv, subln, q, k, vt, bias_tiles)


def _blockdiag_into(dst, stacked):
    (rows, n), cols = stacked.shape, dst.shape[1]
    m = rows // (cols // n)
    lane_in_block = lax.broadcasted_iota(jnp.int32, (n, cols), 1) % n
    tile_cols = (lane_in_block == lax.broadcasted_iota(jnp.int32, (n, cols), 0)).astype(BF16)
    tiled = _dot(stacked.astype(BF16), tile_cols)
    same_group = (lax.broadcasted_iota(jnp.int32, (rows, cols), 0) // m
                  == lax.broadcasted_iota(jnp.int32, (rows, cols), 1) // n)
    dst[...] = jnp.where(same_group, tiled, 0.0).astype(BF16)


def _ssm_kernel(u_ref, perm_ref, permt_ref, bb_ref, a_ref, cc_ref, dskip_ref, wglu_ref,
                out_ref, wb_ref, c_ref, bur, bui, xr_s, xi_s):
    @pl.when(pl.program_id(0) == 0)
    def _():
        xr_s[...] = jnp.zeros(xr_s.shape, F32)
        xi_s[...] = jnp.zeros(xi_s.shape, F32)
        for part in range(2):
            _blockdiag_into(wb_ref.at[part], bb_ref[part])
            _blockdiag_into(c_ref.at[part], cc_ref[part])


    def to_time_major(c):
        blocks = []
        for t0 in range(c * SCAN_STEPS, (c + 1) * SCAN_STEPS, PERM_STEPS):
            rows_bt = jnp.concatenate(
                [u_ref[b, t0:t0 + PERM_STEPS, :] for b in range(SUBLANES)], axis=0)
            blocks.append(_dot(perm_ref[...], rows_bt))
        return jnp.concatenate(blocks, axis=0)

    def b_project(c, u):
        ub = u.astype(BF16)
        for j in range(STATE_LANES // MXU_TILE):
            kt = (j * MXU_TILE // SSM_STATE * SSM_GROUP) // MXU_TILE
            rows = slice(kt * MXU_TILE, (kt + 1) * MXU_TILE)
            cols = slice(j * MXU_TILE, (j + 1) * MXU_TILE)
            bur[c, :, cols] = _dot(ub[:, rows], wb_ref[0, rows, cols])
            bui[c, :, cols] = _dot(ub[:, rows], wb_ref[1, rows, cols])

    def scan(c):
        for s in range(STATE_LANES // SCAN_STRIP):
            sl = slice(s * SCAN_STRIP, (s + 1) * SCAN_STRIP)
            a_r = a_ref[0, :, sl]
            a_i = a_ref[1, :, sl]
            xr = xr_s[:, sl]
            xi = xi_s[:, sl]
            for t in range(SCAN_STEPS):
                step = slice(t * SUBLANES, (t + 1) * SUBLANES)
                xr, xi = (a_r * xr - a_i * xi + bur[c, step, sl],
                          a_r * xi + a_i * xr + bui[c, step, sl])
                bur[c, step, sl] = xr
                bui[c, step, sl] = xi
            xr_s[:, sl] = xr
            xi_s[:, sl] = xi

    def c_project(c, u):
        halves = []
        k_per_half = STATE_LANES // 2
        n_per_half = SSM_WIDTH // 2
        for m in range(2):
            rows = slice(m * k_per_half, (m + 1) * k_per_half)
            cols = slice(m * n_per_half, (m + 1) * n_per_half)
            halves.append(_dot(bur[c, :, rows].astype(BF16), c_ref[0, rows, cols])
                          + _dot(bui[c, :, rows].astype(BF16), c_ref[1, rows, cols]))
        return jax.nn.gelu(jnp.concatenate(halves, axis=-1) + dskip_ref[...] * u)

    def to_batch_major(c, s):
        sb = s.astype(BF16)
        rows_per_block = PERM_STEPS * SUBLANES
        for i, t0 in enumerate(range(c * SCAN_STEPS, (c + 1) * SCAN_STEPS, PERM_STEPS)):
            rows_bt = _dot(permt_ref[...], sb[i * rows_per_block:(i + 1) * rows_per_block])
            for b in range(SUBLANES):
                out_ref[b, t0:t0 + PERM_STEPS, :] = rows_bt[
                    b * PERM_STEPS:(b + 1) * PERM_STEPS].astype(BF16)

    chunks = range(SCAN_CHUNKS)
    us = [to_time_major(c) for c in chunks]
    for c in chunks:
        b_project(c, us[c])
    for c in chunks:
        scan(c)
    ss = [c_project(c, us[c]) for c in chunks]
    ss = [s * jax.nn.sigmoid(_dot(s.astype(BF16), wglu_ref[...])) for s in ss]
    for c in chunks:
        to_batch_major(c, ss[c])


def _ssm(u, bb, a, cc, dskip, wglu, layer):
    rows = SCAN_STEPS * SUBLANES
    B, S, _ = u.shape
    r = jnp.arange(PERM_STEPS * B)
    perm = (r[None, :] == ((r % B) * PERM_STEPS + r // B)[:, None]).astype(BF16)
    blk = pl.BlockSpec((B, SCAN_CHUNKS * SCAN_STEPS, SSM_WIDTH), lambda c: (0, c, 0))
    return pl.pallas_call(
        _ssm_kernel,
        grid=(S // (SCAN_CHUNKS * SCAN_STEPS),),
        in_specs=[blk, _resident(perm.shape), _resident(perm.shape),
                  _per_layer(bb, layer), _per_layer(a, layer), _per_layer(cc, layer),
                  _per_layer(dskip, layer), _per_layer(wglu, layer)],
        out_specs=blk,
        out_shape=jax.ShapeDtypeStruct((B, S, SSM_WIDTH), BF16),
        scratch_shapes=[pltpu.VMEM((2, SSM_WIDTH, STATE_LANES), BF16),
                        pltpu.VMEM((2, STATE_LANES, SSM_WIDTH), BF16),
                        pltpu.VMEM((SCAN_CHUNKS, rows, STATE_LANES), F32),
                        pltpu.VMEM((SCAN_CHUNKS, rows, STATE_LANES), F32),
                        pltpu.VMEM((SUBLANES, STATE_LANES), F32),
                        pltpu.VMEM((SUBLANES, STATE_LANES), F32)],
        compiler_params=pltpu.CompilerParams(
            dimension_semantics=("arbitrary",), vmem_limit_bytes=VMEM_LIMIT),
        name="s5_branch",
    )(u, perm, perm.T, bb, a, cc, dskip, wglu)


def _merge_ffn_kernel(x_ref, o_ref, s_ref, g1_ref, g2_ref,
                      w_in_hbm, wa_hbm, wb_hbm, wo_hbm, w1_hbm, w3_hbm, w2_hbm, out_ref,
                      wg_ref, wa_ref, wb_ref, wo_ref, w1_ref, w3_ref, w2_ref,
                      acc_s, stage, sems, *, layer):
    @pl.when((pl.program_id(0) == 0) & (pl.program_id(1) == 0))
    def _():
        gate_col0 = w_in_hbm.shape[2] - wg_ref.shape[1]
        _stage_bf16([(w_in_hbm.at[layer], gate_col0, wg_ref), (wa_hbm.at[layer], 0, wa_ref),
                     (wb_hbm.at[layer], 0, wb_ref), (wo_hbm.at[layer], 0, wo_ref),
                     (w1_hbm.at[layer], 0, w1_ref), (w3_hbm.at[layer], 0, w3_ref),
                     (w2_hbm.at[layer], 0, w2_ref)], stage, sems)

    x = x_ref[...]
    h = _rms(x, g1_ref[...]).astype(BF16)
    o = jnp.concatenate([o_ref[hd] for hd in range(N_HEADS)], axis=1)
    mixed = jax.nn.sigmoid(_dot(h, wg_ref[:, 0:D_MODEL])) * _dot(o, wa_ref[...])
    mixed += jax.nn.sigmoid(_dot(h, wg_ref[:, D_MODEL:])) * _dot(s_ref[...], wb_ref[...])
    x1 = x + _dot(mixed.astype(BF16), wo_ref[...])

    h2 = _rms(x1, g2_ref[...]).astype(BF16)
    acc_s[...] = x1

    for c in range(D_FF // FF_CHUNK):
        cols = slice(c * FF_CHUNK, (c + 1) * FF_CHUNK)
        t = jax.nn.silu(_dot(h2, w1_ref[:, cols])) * _dot(h2, w3_ref[:, cols])
        acc_s[...] += _dot(t.astype(BF16), w2_ref[cols, :])
    out_ref[...] = acc_s[...]


def _merge_ffn(x, o, s2, g1, g2, w_in, w_a, w_b, w_o, w1, w3, w2, layer):
    B, S, D = x.shape
    tok = lambda width: pl.BlockSpec((None, TM_FFN, width), lambda b, i: (b, i, 0))
    bf16_weight = lambda shape: pltpu.VMEM(shape, BF16)
    return pl.pallas_call(
        functools.partial(_merge_ffn_kernel, layer=layer),
        grid=(B, S // TM_FFN),
        in_specs=[tok(D),
                  pl.BlockSpec((None, N_HEADS, TM_FFN, V_DIM), lambda b, i: (b, 0, i, 0)),
                  tok(SSM_WIDTH),
                  _per_layer(g1, layer), _per_layer(g2, layer)] + [_HBM] * 7,
        out_specs=tok(D),
        out_shape=jax.ShapeDtypeStruct((B, S, D), F32),
        scratch_shapes=[bf16_weight((D, 2 * D)), bf16_weight(w_a.shape[1:]),
                        bf16_weight(w_b.shape[1:]), bf16_weight(w_o.shape[1:]),
                        bf16_weight(w1.shape[1:]), bf16_weight(w3.shape[1:]),
                        bf16_weight(w2.shape[1:]),
                        pltpu.VMEM((TM_FFN, D), F32)] + _STAGE_SCRATCH,
        compiler_params=pltpu.CompilerParams(
            dimension_semantics=("arbitrary", "arbitrary"), vmem_limit_bytes=VMEM_LIMIT),
        name="merge_ffn",
    )(x, o, s2, g1, g2, w_in, w_a, w_b, w_o, w1, w3, w2)


def _t5_bucket(n):
    max_exact = N_BUCKETS // 2
    is_small = n < max_exact
    nf = jnp.maximum(n, 1).astype(F32)
    large = max_exact + (jnp.log(nf / max_exact) / math.log(MAX_DISTANCE / max_exact)
                         * (N_BUCKETS - max_exact)).astype(jnp.int32)
    large = jnp.minimum(large, N_BUCKETS - 1)
    return jnp.where(is_small, n, large)


def _ssm_params(lam_re, lam_im, b_re, b_im, c_re, c_im, log_step):
    step = jnp.exp(log_step.astype(F32))[..., None]
    lr = lam_re.astype(F32)
    li = lam_im.astype(F32)
    decay = jnp.exp(lr * step)
    ab_re = decay * jnp.cos(li * step)
    ab_im = decay * jnp.sin(li * step)
    nr = ab_re - 1.0
    ni = ab_im
    den = lr * lr + li * li
    f_re = (nr * lr + ni * li) / den
    f_im = (ni * lr - nr * li) / den
    br = b_re.astype(F32)
    bi = b_im.astype(F32)
    bb_re = f_re[..., None] * br - f_im[..., None] * bi
    bb_im = f_re[..., None] * bi + f_im[..., None] * br

    def stack_transposed(w):
        *lead, g, n, m = w.shape
        return jnp.swapaxes(w, -1, -2).reshape(*lead, g * m, n)

    depth = lr.shape[0]
    a = jnp.stack([ab_re, ab_im], axis=1).reshape(depth, 2, 1, STATE_LANES)
    return (stack_transposed(jnp.stack([bb_re, bb_im], axis=1)),
            jnp.broadcast_to(a, (depth, 2, SUBLANES, STATE_LANES)),
            stack_transposed(jnp.stack([c_re.astype(F32), -c_im.astype(F32)], axis=1)))


def kernel(x, rel_bias, norm_mix, w_in, q_gain, k_gain, lambda_q1, lambda_k1, lambda_q2, lambda_k2, subln, w_a, lam_re, lam_im, b_re, b_im, c_re, c_im, d_skip, log_step, w_glu, w_b, w_o, norm_ffn, w1, w3, w2):
    B, S, D = x.shape
    depth = w_in.shape[0]
    assert B == SUBLANES and D == D_MODEL

    table = rel_bias.astype(F32)
    dist = (-jnp.arange(MAX_DISTANCE)) % MAX_DISTANCE
    brev = (table[_t5_bucket(dist)] - table[N_BUCKETS - 1][None]).T * LOG2E
    bias_tiles = _bias_tiles(brev[:, None, :])

    seg = (jnp.arange(MXU_TILE)[:, None] // HEAD_DIM
           == jnp.arange(MXU_TILE)[None, :] // HEAD_DIM).astype(BF16)

    row = lambda p: p.astype(F32)[:, None, :]
    heads = QK_WIDTH // HEAD_DIM
    qg = row(jnp.tile(q_gain, (1, heads))) * (HEAD_DIM ** -0.5 * LOG2E)
    kg = row(jnp.tile(k_gain, (1, heads)))
    lamv = jnp.stack([lambda_q1, lambda_k1, lambda_q2, lambda_k2], axis=1).astype(F32)
    bb, a, cc = _ssm_params(lam_re, lam_im, b_re, b_im, c_re, c_im, log_step)
    g_mix, g_ffn, g_sub, dskip = row(norm_mix), row(norm_ffn), row(subln), row(d_skip)
    wglu = w_glu.astype(BF16)

    for l in range(depth):
        lam_init = 0.8 - 0.6 * math.exp(-0.3 * l)
        q, k, vt, u = _inproj(x, g_mix, w_in, l, seg, qg, kg)
        o = _attention(lamv, g_sub, q, k, vt, bias_tiles, lam_init, l)
        s2 = _ssm(u, bb, a, cc, dskip, wglu, l)
        x = _merge_ffn(x, o, s2, g_mix, g_ffn, w_in, w_a, w_b, w_o, w1, w3, w2, l)
    return x
```

```python
import functools
import math

import jax
import jax.numpy as jnp
from jax import lax
from jax.experimental import pallas as pl
from jax.experimental.pallas import tpu as pltpu

D_MODEL = 1024
N_HEADS = 4
HEAD_DIM = 64
V_DIM = 2 * HEAD_DIM
QK_WIDTH = N_HEADS * 2 * HEAD_DIM
ATTN_WIDTH = N_HEADS * V_DIM
SSM_WIDTH = D_MODEL // 2
SSM_GROUP = 16
SSM_GROUPS = SSM_WIDTH // SSM_GROUP
SSM_STATE = 64
STATE_LANES = SSM_GROUPS * SSM_STATE
D_FF = 2816
N_BUCKETS = 32
MAX_DISTANCE = 128
EPS = 1e-6
NEG = -1e30

MXU_TILE = 256
SUBLANES = 8

TM_PROJ = 1024
TM_FFN = 1024
TQ = 256
ONES_ROWS = 16
SCORE_LOOKAHEAD = 8
LOG2E = math.log2(math.e)
SCAN_STEPS = 64
SCAN_CHUNKS = 2
PERM_STEPS = 16
SCAN_STRIP = 512
FF_CHUNK = 256
VMEM_LIMIT = 62 * 1024 * 1024

BF16 = jnp.bfloat16
F32 = jnp.float32


def _dot(a, b):
    return jnp.dot(a, b, preferred_element_type=F32)


def _dot_nt(a, b):
    return lax.dot_general(a, b, (((1,), (1,)), ((), ())), preferred_element_type=F32)


def _rms(x, g):
    return x * lax.rsqrt(jnp.mean(x * x, axis=-1, keepdims=True) + EPS) * g


def _resident(shape):
    zeros = (0,) * len(shape)
    return pl.BlockSpec(shape, lambda *_: zeros, pipeline_mode=pl.Buffered(1))


def _per_layer(arr, layer):
    zeros = (0,) * (arr.ndim - 1)
    return pl.BlockSpec((None,) + arr.shape[1:], lambda *_: (layer,) + zeros,
                        pipeline_mode=pl.Buffered(1))


STAGE_ROWS = 128
STAGE_COLS = 1024
STAGE_DEPTH = 8
_STAGE_SCRATCH = [pltpu.VMEM((STAGE_DEPTH, STAGE_ROWS, STAGE_COLS), F32),
                  pltpu.SemaphoreType.DMA((STAGE_DEPTH,))]
_HBM = pl.BlockSpec(memory_space=pl.ANY)


def _stage_bf16(jobs, stage, sems):
    pieces = []
    for src, col0, dst in jobs:
        n_rows, n_cols = dst.shape
        for r0 in range(0, n_rows, STAGE_ROWS):
            for c0 in range(0, n_cols, STAGE_COLS):
                pieces.append((src, col0, dst, r0, c0, min(STAGE_COLS, n_cols - c0)))

    def copy(i):
        src, col0, _, r0, c0, nc = pieces[i]
        slot = i % STAGE_DEPTH
        return pltpu.make_async_copy(
            src.at[pl.ds(r0, STAGE_ROWS), pl.ds(col0 + c0, nc)],
            stage.at[slot, :, pl.ds(0, nc)], sems.at[slot])

    for i in range(min(STAGE_DEPTH, len(pieces))):
        copy(i).start()
    for i, (_, _, dst, r0, c0, nc) in enumerate(pieces):
        copy(i).wait()
        dst[r0:r0 + STAGE_ROWS, c0:c0 + nc] = stage[i % STAGE_DEPTH, :, 0:nc].astype(BF16)
        if i + STAGE_DEPTH < len(pieces):
            copy(i + STAGE_DEPTH).start()


def _bias_tiles_kernel(brev_ref, out_ref):
    sub = MAX_DISTANCE
    r = lax.broadcasted_iota(jnp.int32, (sub, sub), 0)
    c = lax.broadcasted_iota(jnp.int32, (sub, sub), 1)
    y = pltpu.roll(jnp.broadcast_to(brev_ref[...], (sub, sub)), 0, 1, stride=1, stride_axis=0)
    by_diag = {0: jnp.where(c <= r, y, NEG).T, 1: jnp.where(c > r, y, 0.0).T}
    for t, off in enumerate((TQ, 0)):
        for a in range(TQ // sub):
            for b in range(TQ // sub):
                d = a - b + off // sub
                fill = NEG if d < 0 else 0.0
                out_ref[t, b * sub:(b + 1) * sub, a * sub:(a + 1) * sub] = by_diag.get(
                    d, jnp.full((sub, sub), fill, F32))


def _bias_tiles(brev):
    return pl.pallas_call(
        _bias_tiles_kernel,
        grid=(N_HEADS,),
        in_specs=[pl.BlockSpec((None, 1, MAX_DISTANCE), lambda h: (h, 0, 0))],
        out_specs=pl.BlockSpec((None, 2, TQ, TQ), lambda h: (h, 0, 0, 0)),
        out_shape=jax.ShapeDtypeStruct((N_HEADS, 2, TQ, TQ), F32),
        name="bias_tiles",
    )(brev)


def _store_heads(ref, z):
    for hd in range(ref.shape[0]):
        ref[hd] = z[:, hd * V_DIM:(hd + 1) * V_DIM]


def _inproj_kernel(x_ref, g_ref, w_hbm, seg_ref, qg_ref, kg_ref,
                   q_ref, k_ref, vt_ref, u_ref, w_ref, stage, sems, *, layer):
    @pl.when((pl.program_id(0) == 0) & (pl.program_id(1) == 0))
    def _():
        _stage_bf16([(w_hbm.at[layer], 0, w_ref)], stage, sems)

    h = _rms(x_ref[...], g_ref[...]).astype(BF16)

    def qk_norm(z, gain):
        z2 = (z * z).astype(BF16)
        ss = jnp.concatenate(
            [_dot(z2[:, t * MXU_TILE:(t + 1) * MXU_TILE], seg_ref[...])
             for t in range(QK_WIDTH // MXU_TILE)], axis=1)
        return z * lax.rsqrt(ss * (1.0 / HEAD_DIM) + EPS) * gain

    zq = _dot(h, w_ref[:, 0:QK_WIDTH])
    zk = _dot(h, w_ref[:, QK_WIDTH:2 * QK_WIDTH])
    _store_heads(q_ref, qk_norm(zq, qg_ref[...]).astype(BF16))
    v = _dot(h, w_ref[:, 2 * QK_WIDTH:2 * QK_WIDTH + ATTN_WIDTH])
    _store_heads(k_ref, qk_norm(zk, kg_ref[...]).astype(BF16))
    u_ref[...] = _dot(h, w_ref[:, 2 * QK_WIDTH + ATTN_WIDTH:]).astype(BF16)
    vt_ref[...] = v.T.astype(BF16)


def _inproj(x, g, w_in, layer, seg, qg, kg):
    B, S, D = x.shape
    n_cols = 2 * QK_WIDTH + ATTN_WIDTH + SSM_WIDTH
    tok = lambda width: pl.BlockSpec((None, TM_PROJ, width), lambda b, i: (b, i, 0))
    heads = pl.BlockSpec((None, N_HEADS, TM_PROJ, V_DIM), lambda b, i: (b, 0, i, 0))
    return pl.pallas_call(
        functools.partial(_inproj_kernel, layer=layer),
        grid=(B, S // TM_PROJ),
        in_specs=[tok(D), _per_layer(g, layer), _HBM, _resident(seg.shape),
                  _per_layer(qg, layer), _per_layer(kg, layer)],
        out_specs=[heads, heads,
                   pl.BlockSpec((None, ATTN_WIDTH, TM_PROJ), lambda b, i: (b, 0, i)),
                   tok(SSM_WIDTH)],
        out_shape=[jax.ShapeDtypeStruct((B, N_HEADS, S, V_DIM), BF16),
                   jax.ShapeDtypeStruct((B, N_HEADS, S, V_DIM), BF16),
                   jax.ShapeDtypeStruct((B, ATTN_WIDTH, S), BF16),
                   jax.ShapeDtypeStruct((B, S, SSM_WIDTH), BF16)],
        scratch_shapes=[pltpu.VMEM((D, n_cols), BF16)] + _STAGE_SCRATCH,
        compiler_params=pltpu.CompilerParams(
            dimension_semantics=("arbitrary", "arbitrary"), vmem_limit_bytes=VMEM_LIMIT),
        name="inproj",
    )(x, g, w_in, seg, qg, kg)


def _attn_kernel(lamv_ref, subln_ref, q_ref, k_ref, vt_ref, bias_ref, o_ref,
                 *, lam_init):
    lane = lax.broadcasted_iota(jnp.int32, (TQ, V_DIM), 1)
    ones = jnp.ones((ONES_ROWS, TQ), BF16)
    lv = lamv_ref[...]
    lam = (jnp.exp(jnp.sum(lv[0:1] * lv[1:2], axis=-1, keepdims=True))
           - jnp.exp(jnp.sum(lv[2:3] * lv[3:4], axis=-1, keepdims=True)) + lam_init)

    n_q = k_ref.shape[0] // TQ
    tasks = [(qi, j, mp) for qi in range(n_q) for j in range(qi + 1) for mp in range(2)]
    qms, m, acc = {}, {}, {}

    def scores(qi, j, mp):
        if qi not in qms:
            q = q_ref[qi * TQ:(qi + 1) * TQ, :]
            zero = jnp.zeros_like(q)
            qms[qi] = (jnp.where(lane < HEAD_DIM, q, zero), jnp.where(lane >= HEAD_DIM, q, zero))
        st = _dot_nt(k_ref[j * TQ:(j + 1) * TQ, :], qms[qi][mp])
        bias = {qi: 1, qi - 1: 0}.get(j)
        return st if bias is None else st + bias_ref[bias]

    def softmax_pv(qi, j, mp, st):
        v1t = jnp.concatenate([vt_ref[:, j * TQ:(j + 1) * TQ], ones], axis=0)
        m_cur = jnp.max(st, axis=0, keepdims=True)
        m_new = m_cur if j == 0 else jnp.maximum(m[qi, mp], m_cur)
        p = jnp.exp2(st - m_new)
        pv = _dot(v1t, p.astype(BF16))
        acc[qi, mp] = pv if j == 0 else jnp.exp2(m[qi, mp] - m_new) * acc[qi, mp] + pv
        m[qi, mp] = m_new
        if j == qi and mp == 1:
            a0, a1 = acc.pop((qi, 0)), acc.pop((qi, 1))
            o_t = a0[:V_DIM] / a0[V_DIM:V_DIM + 1] - lam * (a1[:V_DIM] / a1[V_DIM:V_DIM + 1])
            o_n = (o_t * lax.rsqrt(jnp.mean(o_t * o_t, axis=0, keepdims=True) + EPS)).T
            o_ref[qi * TQ:(qi + 1) * TQ, :] = (
                o_n * subln_ref[...] * (1.0 - lam_init)).astype(BF16)

    pending = {}
    for i in range(len(tasks) + SCORE_LOOKAHEAD):
        if i < len(tasks):
            pending[i] = scores(*tasks[i])
        if i >= SCORE_LOOKAHEAD:
            softmax_pv(*tasks[i - SCORE_LOOKAHEAD], pending.pop(i - SCORE_LOOKAHEAD))


def _attention(lamv, subln, q, k, vt, bias_tiles, lam_init, layer):
    B, _, S, _ = q.shape
    head = pl.BlockSpec((None, None, S, V_DIM), lambda b, h: (b, h, 0, 0))
    head_t = pl.BlockSpec((None, V_DIM, S), lambda b, h: (b, h, 0))
    return pl.pallas_call(
        functools.partial(_attn_kernel, lam_init=lam_init),
        grid=(B, N_HEADS),
        in_specs=[_per_layer(lamv, layer), _per_layer(subln, layer), head, head, head_t,
                  pl.BlockSpec((None, 2, TQ, TQ), lambda b, h: (h, 0, 0, 0))],
        out_specs=head,
        out_shape=jax.ShapeDtypeStruct((B, N_HEADS, S, V_DIM), BF16),
        compiler_params=pltpu.CompilerParams(
            dimension_semantics=("parallel", "parallel"), vmem_limit_bytes=VMEM_LIMIT),
        name="diff_attention",
    )(lamv, subln, q, k, vt, bias_tiles)


def _blockdiag_into(dst, stacked):
    (rows, n), cols = stacked.shape, dst.shape[0]
    m = rows // (cols // n)
    lane_in_block = lax.broadcasted_iota(jnp.int32, (n, cols), 1) % n
    tile_cols = (lane_in_block == lax.broadcasted_iota(jnp.int32, (n, cols), 0)).astype(BF16)
    tiled = _dot(stacked.astype(BF16), tile_cols)
    same_group = (lax.broadcasted_iota(jnp.int32, (rows, cols), 0) // m
                  == lax.broadcasted_iota(jnp.int32, (rows, cols), 1) // n)
    dst[...] = jnp.where(same_group, tiled, 0.0).T.astype(BF16)


def _ssm_kernel(u_ref, perm_ref, permt_ref, bb_ref, a_ref, cc_ref, dskip_ref, wglu_ref,
                out_ref, wb_ref, c_ref, bur, bui, xr_s, xi_s):
    @pl.when(pl.program_id(0) == 0)
    def _():
        xr_s[...] = jnp.zeros(xr_s.shape, F32)
        xi_s[...] = jnp.zeros(xi_s.shape, F32)
        for part in range(2):
            _blockdiag_into(wb_ref.at[part], bb_ref[part])
            _blockdiag_into(c_ref.at[part], cc_ref[part])


    def to_time_major(c):
        blocks = []
        for t0 in range(c * SCAN_STEPS, (c + 1) * SCAN_STEPS, PERM_STEPS):
            rows_bt = jnp.concatenate(
                [u_ref[b, t0:t0 + PERM_STEPS, :] for b in range(SUBLANES)], axis=0)
            blocks.append(_dot(perm_ref[...], rows_bt))
        return jnp.concatenate(blocks, axis=0)

    def b_project(c, u):
        ub = u.astype(BF16)
        for j in range(STATE_LANES // MXU_TILE):
            kt = (j * MXU_TILE // SSM_STATE * SSM_GROUP) // MXU_TILE
            rows = slice(kt * MXU_TILE, (kt + 1) * MXU_TILE)
            cols = slice(j * MXU_TILE, (j + 1) * MXU_TILE)
            bur[c, :, cols] = _dot(ub[:, rows], wb_ref[0, rows, cols])
            bui[c, :, cols] = _dot(ub[:, rows], wb_ref[1, rows, cols])

    def scan(c):
        for s in range(STATE_LANES // SCAN_STRIP):
            sl = slice(s * SCAN_STRIP, (s + 1) * SCAN_STRIP)
            a_r = a_ref[0, :, sl]
            a_i = a_ref[1, :, sl]
            xr = xr_s[:, sl]
            xi = xi_s[:, sl]
            for t in range(SCAN_STEPS):
                step = slice(t * SUBLANES, (t + 1) * SUBLANES)
                xr, xi = (a_r * xr - a_i * xi + bur[c, step, sl],
                          a_r * xi + a_i * xr + bui[c, step, sl])
                bur[c, step, sl] = xr
                bui[c, step, sl] = xi
            xr_s[:, sl] = xr
            xi_s[:, sl] = xi

    def c_project(c, u):
        halves = []
        k_per_half = STATE_LANES // 2
        n_per_half = SSM_WIDTH // 2
        for m in range(2):
            rows = slice(m * k_per_half, (m + 1) * k_per_half)
            cols = slice(m * n_per_half, (m + 1) * n_per_half)
            halves.append(_dot(bur[c, :, rows].astype(BF16), c_ref[0, rows, cols])
                          + _dot(bui[c, :, rows].astype(BF16), c_ref[1, rows, cols]))
        return jax.nn.gelu(jnp.concatenate(halves, axis=-1) + dskip_ref[...] * u)

    def to_batch_major(c, s):
        sb = s.astype(BF16)
        rows_per_block = PERM_STEPS * SUBLANES
        for i, t0 in enumerate(range(c * SCAN_STEPS, (c + 1) * SCAN_STEPS, PERM_STEPS)):
            rows_bt = _dot(permt_ref[...], sb[i * rows_per_block:(i + 1) * rows_per_block])
            for b in range(SUBLANES):
                out_ref[b, t0:t0 + PERM_STEPS, :] = rows_bt[
                    b * PERM_STEPS:(b + 1) * PERM_STEPS].astype(BF16)

    chunks = range(SCAN_CHUNKS)
    us = [to_time_major(c) for c in chunks]
    for c in chunks:
        b_project(c, us[c])
    for c in chunks:
        scan(c)
    ss = [c_project(c, us[c]) for c in chunks]
    ss = [s * jax.nn.sigmoid(_dot(s.astype(BF16), wglu_ref[...])) for s in ss]
    for c in chunks:
        to_batch_major(c, ss[c])


def _ssm(u, bb, a, cc, dskip, wglu, layer):
    rows = SCAN_STEPS * SUBLANES
    B, S, _ = u.shape
    r = jnp.arange(PERM_STEPS * B)
    perm = (r[None, :] == ((r % B) * PERM_STEPS + r // B)[:, None]).astype(BF16)
    blk = pl.BlockSpec((B, SCAN_CHUNKS * SCAN_STEPS, SSM_WIDTH), lambda c: (0, c, 0))
    return pl.pallas_call(
        _ssm_kernel,
        grid=(S // (SCAN_CHUNKS * SCAN_STEPS),),
        in_specs=[blk, _resident(perm.shape), _resident(perm.shape),
                  _per_layer(bb, layer), _per_layer(a, layer), _per_layer(cc, layer),
                  _per_layer(dskip, layer), _per_layer(wglu, layer)],
        out_specs=blk,
        out_shape=jax.ShapeDtypeStruct((B, S, SSM_WIDTH), BF16),
        scratch_shapes=[pltpu.VMEM((2, SSM_WIDTH, STATE_LANES), BF16),
                        pltpu.VMEM((2, STATE_LANES, SSM_WIDTH), BF16),
                        pltpu.VMEM((SCAN_CHUNKS, rows, STATE_LANES), F32),
                        pltpu.VMEM((SCAN_CHUNKS, rows, STATE_LANES), F32),
                        pltpu.VMEM((SUBLANES, STATE_LANES), F32),
                        pltpu.VMEM((SUBLANES, STATE_LANES), F32)],
        compiler_params=pltpu.CompilerParams(
            dimension_semantics=("arbitrary",), vmem_limit_bytes=VMEM_LIMIT),
        name="s5_branch",
    )(u, perm, perm.T, bb, a, cc, dskip, wglu)


def _merge_ffn_kernel(x_ref, o_ref, s_ref, g1_ref, g2_ref,
                      w_in_hbm, wa_hbm, wb_hbm, wo_hbm, w1_hbm, w3_hbm, w2_hbm, out_ref,
                      wg_ref, wa_ref, wb_ref, wo_ref, w1_ref, w3_ref, w2_ref,
                      acc_s, stage, sems, *, layer):
    @pl.when((pl.program_id(0) == 0) & (pl.program_id(1) == 0))
    def _():
        gate_col0 = w_in_hbm.shape[2] - wg_ref.shape[1]
        _stage_bf16([(w_in_hbm.at[layer], gate_col0, wg_ref), (wa_hbm.at[layer], 0, wa_ref),
                     (wb_hbm.at[layer], 0, wb_ref), (wo_hbm.at[layer], 0, wo_ref),
                     (w1_hbm.at[layer], 0, w1_ref), (w3_hbm.at[layer], 0, w3_ref),
                     (w2_hbm.at[layer], 0, w2_ref)], stage, sems)

    x = x_ref[...]
    h = _rms(x, g1_ref[...]).astype(BF16)
    o = jnp.concatenate([o_ref[hd] for hd in range(N_HEADS)], axis=1)
    mixed = jax.nn.sigmoid(_dot(h, wg_ref[:, 0:D_MODEL])) * _dot(o, wa_ref[...])
    mixed += jax.nn.sigmoid(_dot(h, wg_ref[:, D_MODEL:])) * _dot(s_ref[...], wb_ref[...])
    x1 = x + _dot(mixed.astype(BF16), wo_ref[...])

    h2 = _rms(x1, g2_ref[...]).astype(BF16)
    acc_s[...] = x1

    for c in range(D_FF // FF_CHUNK):
        cols = slice(c * FF_CHUNK, (c + 1) * FF_CHUNK)
        t = jax.nn.silu(_dot(h2, w1_ref[:, cols])) * _dot(h2, w3_ref[:, cols])
        acc_s[...] += _dot(t.astype(BF16), w2_ref[cols, :])
    out_ref[...] = acc_s[...]


def _merge_ffn(x, o, s2, g1, g2, w_in, w_a, w_b, w_o, w1, w3, w2, layer):
    B, S, D = x.shape
    tok = lambda width: pl.BlockSpec((None, TM_FFN, width), lambda b, i: (b, i, 0))
    bf16_weight = lambda shape: pltpu.VMEM(shape, BF16)
    return pl.pallas_call(
        functools.partial(_merge_ffn_kernel, layer=layer),
        grid=(B, S // TM_FFN),
        in_specs=[tok(D),
                  pl.BlockSpec((None, N_HEADS, TM_FFN, V_DIM), lambda b, i: (b, 0, i, 0)),
                  tok(SSM_WIDTH),
                  _per_layer(g1, layer), _per_layer(g2, layer)] + [_HBM] * 7,
        out_specs=tok(D),
        out_shape=jax.ShapeDtypeStruct((B, S, D), F32),
        scratch_shapes=[bf16_weight((D, 2 * D)), bf16_weight(w_a.shape[1:]),
                        bf16_weight(w_b.shape[1:]), bf16_weight(w_o.shape[1:]),
                        bf16_weight(w1.shape[1:]), bf16_weight(w3.shape[1:]),
                        bf16_weight(w2.shape[1:]),
                        pltpu.VMEM((TM_FFN, D), F32)] + _STAGE_SCRATCH,
        compiler_params=pltpu.CompilerParams(
            dimension_semantics=("arbitrary", "arbitrary"), vmem_limit_bytes=VMEM_LIMIT),
        name="merge_ffn",
    )(x, o, s2, g1, g2, w_in, w_a, w_b, w_o, w1, w3, w2)


def _t5_bucket(n):
    max_exact = N_BUCKETS // 2
    is_small = n < max_exact
    nf = jnp.maximum(n, 1).astype(F32)
    large = max_exact + (jnp.log(nf / max_exact) / math.log(MAX_DISTANCE / max_exact)
                         * (N_BUCKETS - max_exact)).astype(jnp.int32)
    large = jnp.minimum(large, N_BUCKETS - 1)
    return jnp.where(is_small, n, large)


def _ssm_params(lam_re, lam_im, b_re, b_im, c_re, c_im, log_step):
    step = jnp.exp(log_step.astype(F32))[..., None]
    lr = lam_re.astype(F32)
    li = lam_im.astype(F32)
    decay = jnp.exp(lr * step)
    ab_re = decay * jnp.cos(li * step)
    ab_im = decay * jnp.sin(li * step)
    nr = ab_re - 1.0
    ni = ab_im
    den = lr * lr + li * li
    f_re = (nr * lr + ni * li) / den
    f_im = (ni * lr - nr * li) / den
    br = b_re.astype(F32)
    bi = b_im.astype(F32)
    bb_re = f_re[..., None] * br - f_im[..., None] * bi
    bb_im = f_re[..., None] * bi + f_im[..., None] * br

    def stack_blocks(w):
        *lead, g, n, m = w.shape
        return w.reshape(*lead, g * n, m)

    depth = lr.shape[0]
    a = jnp.stack([ab_re, ab_im], axis=1).reshape(depth, 2, 1, STATE_LANES)
    return (stack_blocks(jnp.stack([bb_re, bb_im], axis=1)),
            jnp.broadcast_to(a, (depth, 2, SUBLANES, STATE_LANES)),
            stack_blocks(jnp.stack([c_re.astype(F32), -c_im.astype(F32)], axis=1)))


def kernel(x, rel_bias, norm_mix, w_in, q_gain, k_gain, lambda_q1, lambda_k1, lambda_q2, lambda_k2, subln, w_a, lam_re, lam_im, b_re, b_im, c_re, c_im, d_skip, log_step, w_glu, w_b, w_o, norm_ffn, w1, w3, w2):
    B, S, D = x.shape
    depth = w_in.shape[0]
    assert B == SUBLANES and D == D_MODEL

    table = rel_bias.astype(F32)
    dist = (-jnp.arange(MAX_DISTANCE)) % MAX_DISTANCE
    brev = (table[_t5_bucket(dist)] - table[N_BUCKETS - 1][None]).T * LOG2E
    bias_tiles = _bias_tiles(brev[:, None, :])

    seg = (jnp.arange(MXU_TILE)[:, None] // HEAD_DIM
           == jnp.arange(MXU_TILE)[None, :] // HEAD_DIM).astype(BF16)

    row = lambda p: p.astype(F32)[:, None, :]
    heads = QK_WIDTH // HEAD_DIM
    qg = row(jnp.tile(q_gain, (1, heads))) * (HEAD_DIM ** -0.5 * LOG2E)
    kg = row(jnp.tile(k_gain, (1, heads)))
    lamv = jnp.stack([lambda_q1, lambda_k1, lambda_q2, lambda_k2], axis=1).astype(F32)
    bb, a, cc = _ssm_params(lam_re, lam_im, b_re, b_im, c_re, c_im, log_step)
    g_mix, g_ffn, g_sub, dskip = row(norm_mix), row(norm_ffn), row(subln), row(d_skip)
    wglu = w_glu.astype(BF16)

    for l in range(depth):
        lam_init = 0.8 - 0.6 * math.exp(-0.3 * l)
        q, k, vt, u = _inproj(x, g_mix, w_in, l, seg, qg, kg)
        o = _attention(lamv, g_sub, q, k, vt, bias_tiles, lam_init, l)
        s2 = _ssm(u, bb, a, cc, dskip, wglu, l)
        x = _merge_ffn(x, o, s2, g_mix, g_ffn, w_in, w_a, w_b, w_o, w1, w3, w2, l)
    return x
```

```python
import functools
import math

import jax
import jax.numpy as jnp
from jax import lax
from jax.experimental import pallas as pl
from jax.experimental.pallas import tpu as pltpu

D_MODEL = 1024
N_HEADS = 4
HEAD_DIM = 64
V_DIM = 2 * HEAD_DIM
QK_WIDTH = N_HEADS * 2 * HEAD_DIM
ATTN_WIDTH = N_HEADS * V_DIM
SSM_WIDTH = D_MODEL // 2
SSM_GROUP = 16
SSM_GROUPS = SSM_WIDTH // SSM_GROUP
SSM_STATE = 64
STATE_LANES = SSM_GROUPS * SSM_STATE
D_FF = 2816
N_BUCKETS = 32
MAX_DISTANCE = 128
EPS = 1e-6
NEG = -1e30

MXU_TILE = 256
SUBLANES = 8

TM_PROJ = 1024
TM_FFN = 1024
TQ = 256
ONES_ROWS = 16
HEADS_PER_STEP = 2
SCORE_LOOKAHEAD = 8
LOG2E = math.log2(math.e)
SCAN_STEPS = 64
SCAN_CHUNKS = 2
PERM_STEPS = 16
SCAN_STRIP = 512
FF_CHUNK = 256
VMEM_LIMIT = 62 * 1024 * 1024

BF16 = jnp.bfloat16
F32 = jnp.float32


def _dot(a, b):
    return jnp.dot(a, b, preferred_element_type=F32)


def _dot_nt(a, b):
    return lax.dot_general(a, b, (((1,), (1,)), ((), ())), preferred_element_type=F32)


def _rms(x, g):
    return x * lax.rsqrt(jnp.mean(x * x, axis=-1, keepdims=True) + EPS) * g


def _resident(shape):
    zeros = (0,) * len(shape)
    return pl.BlockSpec(shape, lambda *_: zeros, pipeline_mode=pl.Buffered(1))


def _per_layer(arr, layer):
    zeros = (0,) * (arr.ndim - 1)
    return pl.BlockSpec((None,) + arr.shape[1:], lambda *_: (layer,) + zeros,
                        pipeline_mode=pl.Buffered(1))


STAGE_ROWS = 128
STAGE_COLS = 1024
STAGE_DEPTH = 8
_STAGE_SCRATCH = [pltpu.VMEM((STAGE_DEPTH, STAGE_ROWS, STAGE_COLS), F32),
                  pltpu.SemaphoreType.DMA((STAGE_DEPTH,))]
_HBM = pl.BlockSpec(memory_space=pl.ANY)


def _stage_bf16(jobs, stage, sems):
    pieces = []
    for src, col0, dst in jobs:
        n_rows, n_cols = dst.shape
        for r0 in range(0, n_rows, STAGE_ROWS):
            for c0 in range(0, n_cols, STAGE_COLS):
                pieces.append((src, col0, dst, r0, c0, min(STAGE_COLS, n_cols - c0)))

    def copy(i):
        src, col0, _, r0, c0, nc = pieces[i]
        slot = i % STAGE_DEPTH
        return pltpu.make_async_copy(
            src.at[pl.ds(r0, STAGE_ROWS), pl.ds(col0 + c0, nc)],
            stage.at[slot, :, pl.ds(0, nc)], sems.at[slot])

    for i in range(min(STAGE_DEPTH, len(pieces))):
        copy(i).start()
    for i, (_, _, dst, r0, c0, nc) in enumerate(pieces):
        copy(i).wait()
        dst[r0:r0 + STAGE_ROWS, c0:c0 + nc] = stage[i % STAGE_DEPTH, :, 0:nc].astype(BF16)
        if i + STAGE_DEPTH < len(pieces):
            copy(i + STAGE_DEPTH).start()


def _bias_tiles_kernel(brev_ref, out_ref):
    sub = MAX_DISTANCE
    r = lax.broadcasted_iota(jnp.int32, (sub, sub), 0)
    c = lax.broadcasted_iota(jnp.int32, (sub, sub), 1)
    y = pltpu.roll(jnp.broadcast_to(brev_ref[...], (sub, sub)), 0, 1, stride=1, stride_axis=0)
    by_diag = {0: jnp.where(c <= r, y, NEG).T, 1: jnp.where(c > r, y, 0.0).T}
    for t, off in enumerate((TQ, 0)):
        for a in range(TQ // sub):
            for b in range(TQ // sub):
                d = a - b + off // sub
                fill = NEG if d < 0 else 0.0
                out_ref[t, b * sub:(b + 1) * sub, a * sub:(a + 1) * sub] = by_diag.get(
                    d, jnp.full((sub, sub), fill, F32))


def _bias_tiles(brev):
    return pl.pallas_call(
        _bias_tiles_kernel,
        grid=(N_HEADS,),
        in_specs=[pl.BlockSpec((None, 1, MAX_DISTANCE), lambda h: (h, 0, 0))],
        out_specs=pl.BlockSpec((None, 2, TQ, TQ), lambda h: (h, 0, 0, 0)),
        out_shape=jax.ShapeDtypeStruct((N_HEADS, 2, TQ, TQ), F32),
        name="bias_tiles",
    )(brev)


def _store_heads(ref, z):
    for hd in range(ref.shape[0]):
        ref[hd] = z[:, hd * V_DIM:(hd + 1) * V_DIM]


def _inproj_kernel(x_ref, g_ref, w_hbm, seg_ref, qg_ref, kg_ref,
                   q_ref, k_ref, vt_ref, u_ref, w_ref, stage, sems, *, layer):
    @pl.when((pl.program_id(0) == 0) & (pl.program_id(1) == 0))
    def _():
        _stage_bf16([(w_hbm.at[layer], 0, w_ref)], stage, sems)

    h = _rms(x_ref[...], g_ref[...]).astype(BF16)

    def qk_norm(z, gain):
        z2 = (z * z).astype(BF16)
        ss = jnp.concatenate(
            [_dot(z2[:, t * MXU_TILE:(t + 1) * MXU_TILE], seg_ref[...])
             for t in range(QK_WIDTH // MXU_TILE)], axis=1)
        return z * lax.rsqrt(ss * (1.0 / HEAD_DIM) + EPS) * gain

    zq = _dot(h, w_ref[:, 0:QK_WIDTH])
    zk = _dot(h, w_ref[:, QK_WIDTH:2 * QK_WIDTH])
    _store_heads(q_ref, qk_norm(zq, qg_ref[...]).astype(BF16))
    v = _dot(h, w_ref[:, 2 * QK_WIDTH:2 * QK_WIDTH + ATTN_WIDTH])
    _store_heads(k_ref, qk_norm(zk, kg_ref[...]).astype(BF16))
    u_ref[...] = _dot(h, w_ref[:, 2 * QK_WIDTH + ATTN_WIDTH:]).astype(BF16)
    vt_ref[...] = v.T.astype(BF16)


def _inproj(x, g, w_in, layer, seg, qg, kg):
    B, S, D = x.shape
    n_cols = 2 * QK_WIDTH + ATTN_WIDTH + SSM_WIDTH
    tok = lambda width: pl.BlockSpec((None, TM_PROJ, width), lambda b, i: (b, i, 0))
    heads = pl.BlockSpec((None, N_HEADS, TM_PROJ, V_DIM), lambda b, i: (b, 0, i, 0))
    return pl.pallas_call(
        functools.partial(_inproj_kernel, layer=layer),
        grid=(B, S // TM_PROJ),
        in_specs=[tok(D), _per_layer(g, layer), _HBM, _resident(seg.shape),
                  _per_layer(qg, layer), _per_layer(kg, layer)],
        out_specs=[heads, heads,
                   pl.BlockSpec((None, ATTN_WIDTH, TM_PROJ), lambda b, i: (b, 0, i)),
                   tok(SSM_WIDTH)],
        out_shape=[jax.ShapeDtypeStruct((B, N_HEADS, S, V_DIM), BF16),
                   jax.ShapeDtypeStruct((B, N_HEADS, S, V_DIM), BF16),
                   jax.ShapeDtypeStruct((B, ATTN_WIDTH, S), BF16),
                   jax.ShapeDtypeStruct((B, S, SSM_WIDTH), BF16)],
        scratch_shapes=[pltpu.VMEM((D, n_cols), BF16)] + _STAGE_SCRATCH,
        compiler_params=pltpu.CompilerParams(
            dimension_semantics=("arbitrary", "arbitrary"), vmem_limit_bytes=VMEM_LIMIT),
        name="inproj",
    )(x, g, w_in, seg, qg, kg)


def _attn_kernel(lamv_ref, subln_ref, q_ref, k_ref, vt_ref, bias_ref, o_ref,
                 *, lam_init):
    lane = lax.broadcasted_iota(jnp.int32, (TQ, V_DIM), 1)
    ones = jnp.ones((ONES_ROWS, TQ), BF16)
    lv = lamv_ref[...]
    lam = (jnp.exp(jnp.sum(lv[0:1] * lv[1:2], axis=-1, keepdims=True))
           - jnp.exp(jnp.sum(lv[2:3] * lv[3:4], axis=-1, keepdims=True)) + lam_init)

    n_heads, n_keys = k_ref.shape[0], k_ref.shape[1]
    tasks = [(hd, qi, j, mp) for hd in range(n_heads) for qi in range(n_keys // TQ)
             for j in range(qi + 1) for mp in range(2)]
    qms, m, acc = {}, {}, {}

    def scores(hd, qi, j, mp):
        if (hd, qi) not in qms:
            q = q_ref[hd, qi * TQ:(qi + 1) * TQ, :]
            zero = jnp.zeros_like(q)
            qms[hd, qi] = (jnp.where(lane < HEAD_DIM, q, zero),
                           jnp.where(lane >= HEAD_DIM, q, zero))
        st = _dot_nt(k_ref[hd, j * TQ:(j + 1) * TQ, :], qms[hd, qi][mp])
        bias = {qi: 1, qi - 1: 0}.get(j)
        return st if bias is None else st + bias_ref[hd, bias]

    def softmax_pv(hd, qi, j, mp, st):
        v_rows = slice(hd * V_DIM, (hd + 1) * V_DIM)
        v1t = jnp.concatenate([vt_ref[v_rows, j * TQ:(j + 1) * TQ], ones], axis=0)
        m_cur = jnp.max(st, axis=0, keepdims=True)
        m_new = m_cur if j == 0 else jnp.maximum(m[hd, qi, mp], m_cur)
        p = jnp.exp2(st - m_new)
        pv = _dot(v1t, p.astype(BF16))
        acc[hd, qi, mp] = (pv if j == 0
                           else jnp.exp2(m[hd, qi, mp] - m_new) * acc[hd, qi, mp] + pv)
        m[hd, qi, mp] = m_new
        if j == qi and mp == 1:
            a0, a1 = acc.pop((hd, qi, 0)), acc.pop((hd, qi, 1))
            o_t = a0[:V_DIM] / a0[V_DIM:V_DIM + 1] - lam * (a1[:V_DIM] / a1[V_DIM:V_DIM + 1])
            o_n = (o_t * lax.rsqrt(jnp.mean(o_t * o_t, axis=0, keepdims=True) + EPS)).T
            o_ref[hd, qi * TQ:(qi + 1) * TQ, :] = (
                o_n * subln_ref[...] * (1.0 - lam_init)).astype(BF16)

    pending = {}
    for i in range(len(tasks) + SCORE_LOOKAHEAD):
        if i < len(tasks):
            pending[i] = scores(*tasks[i])
        if i >= SCORE_LOOKAHEAD:
            softmax_pv(*tasks[i - SCORE_LOOKAHEAD], pending.pop(i - SCORE_LOOKAHEAD))


def _attention(lamv, subln, q, k, vt, bias_tiles, lam_init, layer):
    B, _, S, _ = q.shape
    hps = HEADS_PER_STEP
    head = pl.BlockSpec((None, hps, S, V_DIM), lambda b, h: (b, h, 0, 0))
    head_t = pl.BlockSpec((None, hps * V_DIM, S), lambda b, h: (b, h, 0))
    return pl.pallas_call(
        functools.partial(_attn_kernel, lam_init=lam_init),
        grid=(B, N_HEADS // hps),
        in_specs=[_per_layer(lamv, layer), _per_layer(subln, layer), head, head, head_t,
                  pl.BlockSpec((hps, 2, TQ, TQ), lambda b, h: (h, 0, 0, 0))],
        out_specs=head,
        out_shape=jax.ShapeDtypeStruct((B, N_HEADS, S, V_DIM), BF16),
        compiler_params=pltpu.CompilerParams(
            dimension_semantics=("parallel", "parallel"), vmem_limit_bytes=VMEM_LIMIT),
        name="diff_attention",
    )(lamv, subln, q, k, vt, bias_tiles)


def _blockdiag_into(dst, stacked):
    (rows, n), cols = stacked.shape, dst.shape[1]
    m = rows // (cols // n)
    lane_in_block = lax.broadcasted_iota(jnp.int32, (n, cols), 1) % n
    tile_cols = (lane_in_block == lax.broadcasted_iota(jnp.int32, (n, cols), 0)).astype(BF16)
    tiled = _dot(stacked.astype(BF16), tile_cols)
    same_group = (lax.broadcasted_iota(jnp.int32, (rows, cols), 0) // m
                  == lax.broadcasted_iota(jnp.int32, (rows, cols), 1) // n)
    dst[...] = jnp.where(same_group, tiled, 0.0).astype(BF16)


def _ssm_kernel(u_ref, perm_ref, permt_ref, bb_ref, a_ref, cc_ref, dskip_ref, wglu_ref,
                out_ref, wb_ref, c_ref, bur, bui, xr_s, xi_s):
    @pl.when(pl.program_id(0) == 0)
    def _():
        xr_s[...] = jnp.zeros(xr_s.shape, F32)
        xi_s[...] = jnp.zeros(xi_s.shape, F32)
        for part in range(2):
            _blockdiag_into(wb_ref.at[part], bb_ref[part])
            _blockdiag_into(c_ref.at[part], cc_ref[part])


    def to_time_major(c):
        blocks = []
        for t0 in range(c * SCAN_STEPS, (c + 1) * SCAN_STEPS, PERM_STEPS):
            rows_bt = jnp.concatenate(
                [u_ref[b, t0:t0 + PERM_STEPS, :] for b in range(SUBLANES)], axis=0)
            blocks.append(_dot(perm_ref[...], rows_bt))
        return jnp.concatenate(blocks, axis=0)

    def b_project(c, u):
        ub = u.astype(BF16)
        for j in range(STATE_LANES // MXU_TILE):
            kt = (j * MXU_TILE // SSM_STATE * SSM_GROUP) // MXU_TILE
            rows = slice(kt * MXU_TILE, (kt + 1) * MXU_TILE)
            cols = slice(j * MXU_TILE, (j + 1) * MXU_TILE)
            bur[c, :, cols] = _dot(ub[:, rows], wb_ref[0, rows, cols])
            bui[c, :, cols] = _dot(ub[:, rows], wb_ref[1, rows, cols])

    def scan(c):
        for s in range(STATE_LANES // SCAN_STRIP):
            sl = slice(s * SCAN_STRIP, (s + 1) * SCAN_STRIP)
            a_r = a_ref[0, :, sl]
            a_i = a_ref[1, :, sl]
            xr = xr_s[:, sl]
            xi = xi_s[:, sl]
            for t in range(SCAN_STEPS):
                step = slice(t * SUBLANES, (t + 1) * SUBLANES)
                xr, xi = (a_r * xr - a_i * xi + bur[c, step, sl],
                          a_r * xi + a_i * xr + bui[c, step, sl])
                bur[c, step, sl] = xr
                bui[c, step, sl] = xi
            xr_s[:, sl] = xr
            xi_s[:, sl] = xi

    def c_project(c, u):
        halves = []
        k_per_half = STATE_LANES // 2
        n_per_half = SSM_WIDTH // 2
        for m in range(2):
            rows = slice(m * k_per_half, (m + 1) * k_per_half)
            cols = slice(m * n_per_half, (m + 1) * n_per_half)
            halves.append(_dot(bur[c, :, rows].astype(BF16), c_ref[0, rows, cols])
                          + _dot(bui[c, :, rows].astype(BF16), c_ref[1, rows, cols]))
        return jax.nn.gelu(jnp.concatenate(halves, axis=-1) + dskip_ref[...] * u)

    def to_batch_major(c, s):
        sb = s.astype(BF16)
        rows_per_block = PERM_STEPS * SUBLANES
        for i, t0 in enumerate(range(c * SCAN_STEPS, (c + 1) * SCAN_STEPS, PERM_STEPS)):
            rows_bt = _dot(permt_ref[...], sb[i * rows_per_block:(i + 1) * rows_per_block])
            for b in range(SUBLANES):
                out_ref[b, t0:t0 + PERM_STEPS, :] = rows_bt[
                    b * PERM_STEPS:(b + 1) * PERM_STEPS].astype(BF16)

    chunks = range(SCAN_CHUNKS)
    us = [to_time_major(c) for c in chunks]
    for c in chunks:
        b_project(c, us[c])
    for c in chunks:
        scan(c)
    ss = [c_project(c, us[c]) for c in chunks]
    ss = [s * jax.nn.sigmoid(_dot(s.astype(BF16), wglu_ref[...])) for s in ss]
    for c in chunks:
        to_batch_major(c, ss[c])


def _ssm(u, bb, a, cc, dskip, wglu, layer):
    rows = SCAN_STEPS * SUBLANES
    B, S, _ = u.shape
    r = jnp.arange(PERM_STEPS * B)
    perm = (r[None, :] == ((r % B) * PERM_STEPS + r // B)[:, None]).astype(BF16)
    blk = pl.BlockSpec((B, SCAN_CHUNKS * SCAN_STEPS, SSM_WIDTH), lambda c: (0, c, 0))
    return pl.pallas_call(
        _ssm_kernel,
        grid=(S // (SCAN_CHUNKS * SCAN_STEPS),),
        in_specs=[blk, _resident(perm.shape), _resident(perm.shape),
                  _per_layer(bb, layer), _per_layer(a, layer), _per_layer(cc, layer),
                  _per_layer(dskip, layer), _per_layer(wglu, layer)],
        out_specs=blk,
        out_shape=jax.ShapeDtypeStruct((B, S, SSM_WIDTH), BF16),
        scratch_shapes=[pltpu.VMEM((2, SSM_WIDTH, STATE_LANES), BF16),
                        pltpu.VMEM((2, STATE_LANES, SSM_WIDTH), BF16),
                        pltpu.VMEM((SCAN_CHUNKS, rows, STATE_LANES), F32),
                        pltpu.VMEM((SCAN_CHUNKS, rows, STATE_LANES), F32),
                        pltpu.VMEM((SUBLANES, STATE_LANES), F32),
                        pltpu.VMEM((SUBLANES, STATE_LANES), F32)],
        compiler_params=pltpu.CompilerParams(
            dimension_semantics=("arbitrary",), vmem_limit_bytes=VMEM_LIMIT),
        name="s5_branch",
    )(u, perm, perm.T, bb, a, cc, dskip, wglu)


def _merge_ffn_kernel(x_ref, o_ref, s_ref, g1_ref, g2_ref,
                      w_in_hbm, wa_hbm, wb_hbm, wo_hbm, w1_hbm, w3_hbm, w2_hbm, out_ref,
                      wg_ref, wa_ref, wb_ref, wo_ref, w1_ref, w3_ref, w2_ref,
                      acc_s, stage, sems, *, layer):
    @pl.when((pl.program_id(0) == 0) & (pl.program_id(1) == 0))
    def _():
        gate_col0 = w_in_hbm.shape[2] - wg_ref.shape[1]
        _stage_bf16([(w_in_hbm.at[layer], gate_col0, wg_ref), (wa_hbm.at[layer], 0, wa_ref),
                     (wb_hbm.at[layer], 0, wb_ref), (wo_hbm.at[layer], 0, wo_ref),
                     (w1_hbm.at[layer], 0, w1_ref), (w3_hbm.at[layer], 0, w3_ref),
                     (w2_hbm.at[layer], 0, w2_ref)], stage, sems)

    x = x_ref[...]
    h = _rms(x, g1_ref[...]).astype(BF16)
    o = jnp.concatenate([o_ref[hd] for hd in range(N_HEADS)], axis=1)
    mixed = jax.nn.sigmoid(_dot(h, wg_ref[:, 0:D_MODEL])) * _dot(o, wa_ref[...])
    mixed += jax.nn.sigmoid(_dot(h, wg_ref[:, D_MODEL:])) * _dot(s_ref[...], wb_ref[...])
    x1 = x + _dot(mixed.astype(BF16), wo_ref[...])

    h2 = _rms(x1, g2_ref[...]).astype(BF16)
    acc_s[...] = x1

    for c in range(D_FF // FF_CHUNK):
        cols = slice(c * FF_CHUNK, (c + 1) * FF_CHUNK)
        t = jax.nn.silu(_dot(h2, w1_ref[:, cols])) * _dot(h2, w3_ref[:, cols])
        acc_s[...] += _dot(t.astype(BF16), w2_ref[cols, :])
    out_ref[...] = acc_s[...]


def _merge_ffn(x, o, s2, g1, g2, w_in, w_a, w_b, w_o, w1, w3, w2, layer):
    B, S, D = x.shape
    tok = lambda width: pl.BlockSpec((None, TM_FFN, width), lambda b, i: (b, i, 0))
    bf16_weight = lambda shape: pltpu.VMEM(shape, BF16)
    return pl.pallas_call(
        functools.partial(_merge_ffn_kernel, layer=layer),
        grid=(B, S // TM_FFN),
        in_specs=[tok(D),
                  pl.BlockSpec((None, N_HEADS, TM_FFN, V_DIM), lambda b, i: (b, 0, i, 0)),
                  tok(SSM_WIDTH),
                  _per_layer(g1, layer), _per_layer(g2, layer)] + [_HBM] * 7,
        out_specs=tok(D),
        out_shape=jax.ShapeDtypeStruct((B, S, D), F32),
        scratch_shapes=[bf16_weight((D, 2 * D)), bf16_weight(w_a.shape[1:]),
                        bf16_weight(w_b.shape[1:]), bf16_weight(w_o.shape[1:]),
                        bf16_weight(w1.shape[1:]), bf16_weight(w3.shape[1:]),
                        bf16_weight(w2.shape[1:]),
                        pltpu.VMEM((TM_FFN, D), F32)] + _STAGE_SCRATCH,
        compiler_params=pltpu.CompilerParams(
            dimension_semantics=("arbitrary", "arbitrary"), vmem_limit_bytes=VMEM_LIMIT),
        name="merge_ffn",
    )(x, o, s2, g1, g2, w_in, w_a, w_b, w_o, w1, w3, w2)


def _t5_bucket(n):
    max_exact = N_BUCKETS // 2
    is_small = n < max_exact
    nf = jnp.maximum(n, 1).astype(F32)
    large = max_exact + (jnp.log(nf / max_exact) / math.log(MAX_DISTANCE / max_exact)
                         * (N_BUCKETS - max_exact)).astype(jnp.int32)
    large = jnp.minimum(large, N_BUCKETS - 1)
    return jnp.where(is_small, n, large)


def _ssm_params(lam_re, lam_im, b_re, b_im, c_re, c_im, log_step):
    step = jnp.exp(log_step.astype(F32))[..., None]
    lr = lam_re.astype(F32)
    li = lam_im.astype(F32)
    decay = jnp.exp(lr * step)
    ab_re = decay * jnp.cos(li * step)
    ab_im = decay * jnp.sin(li * step)
    nr = ab_re - 1.0
    ni = ab_im
    den = lr * lr + li * li
    f_re = (nr * lr + ni * li) / den
    f_im = (ni * lr - nr * li) / den
    br = b_re.astype(F32)
    bi = b_im.astype(F32)
    bb_re = f_re[..., None] * br - f_im[..., None] * bi
    bb_im = f_re[..., None] * bi + f_im[..., None] * br

    def stack_transposed(w):
        *lead, g, n, m = w.shape
        return jnp.swapaxes(w, -1, -2).reshape(*lead, g * m, n)

    depth = lr.shape[0]
    a = jnp.stack([ab_re, ab_im], axis=1).reshape(depth, 2, 1, STATE_LANES)
    return (stack_transposed(jnp.stack([bb_re, bb_im], axis=1)),
            jnp.broadcast_to(a, (depth, 2, SUBLANES, STATE_LANES)),
            stack_transposed(jnp.stack([c_re.astype(F32), -c_im.astype(F32)], axis=1)))


def kernel(x, rel_bias, norm_mix, w_in, q_gain, k_gain, lambda_q1, lambda_k1, lambda_q2, lambda_k2, subln, w_a, lam_re, lam_im, b_re, b_im, c_re, c_im, d_skip, log_step, w_glu, w_b, w_o, norm_ffn, w1, w3, w2):
    B, S, D = x.shape
    depth = w_in.shape[0]
    assert B == SUBLANES and D == D_MODEL

    table = rel_bias.astype(F32)
    dist = (-jnp.arange(MAX_DISTANCE)) % MAX_DISTANCE
    brev = (table[_t5_bucket(dist)] - table[N_BUCKETS - 1][None]).T * LOG2E
    bias_tiles = _bias_tiles(brev[:, None, :])

    seg = (jnp.arange(MXU_TILE)[:, None] // HEAD_DIM
           == jnp.arange(MXU_TILE)[None, :] // HEAD_DIM).astype(BF16)

    row = lambda p: p.astype(F32)[:, None, :]
    heads = QK_WIDTH // HEAD_DIM
    qg = row(jnp.tile(q_gain, (1, heads))) * (HEAD_DIM ** -0.5 * LOG2E)
    kg = row(jnp.tile(k_gain, (1, heads)))
    lamv = jnp.stack([lambda_q1, lambda_k1, lambda_q2, lambda_k2], axis=1).astype(F32)
    bb, a, cc = _ssm_params(lam_re, lam_im, b_re, b_im, c_re, c_im, log_step)
    g_mix, g_ffn, g_sub, dskip = row(norm_mix), row(norm_ffn), row(subln), row(d_skip)
    wglu = w_glu.astype(BF16)

    for l in range(depth):
        lam_init = 0.8 - 0.6 * math.exp(-0.3 * l)
        q, k, vt, u = _inproj(x, g_mix, w_in, l, seg, qg, kg)
        o = _attention(lamv, g_sub, q, k, vt, bias_tiles, lam_init, l)
        s2 = _ssm(u, bb, a, cc, dskip, wglu, l)
        x = _merge_ffn(x, o, s2, g_mix, g_ffn, w_in, w_a, w_b, w_o, w1, w3, w2, l)
    return x
```

```python
import functools
import math

import jax
import jax.numpy as jnp
from jax import lax
from jax.experimental import pallas as pl
from jax.experimental.pallas import tpu as pltpu

D_MODEL = 1024
N_HEADS = 4
HEAD_DIM = 64
V_DIM = 2 * HEAD_DIM
QK_WIDTH = N_HEADS * 2 * HEAD_DIM
ATTN_WIDTH = N_HEADS * V_DIM
SSM_WIDTH = D_MODEL // 2
SSM_GROUP = 16
SSM_GROUPS = SSM_WIDTH // SSM_GROUP
SSM_STATE = 64
STATE_LANES = SSM_GROUPS * SSM_STATE
D_FF = 2816
N_BUCKETS = 32
MAX_DISTANCE = 128
EPS = 1e-6
NEG = -1e30

MXU_TILE = 256
SUBLANES = 8

TM_PROJ = 1024
TM_FFN = 1024
TQ = 256
ONES_ROWS = 16
HEADS_PER_STEP = 4
SCORE_LOOKAHEAD = 8
LOG2E = math.log2(math.e)
SCAN_STEPS = 64
SCAN_CHUNKS = 2
PERM_STEPS = 16
SCAN_STRIP = 512
FF_CHUNK = 256
VMEM_LIMIT = 62 * 1024 * 1024

BF16 = jnp.bfloat16
F32 = jnp.float32


def _dot(a, b):
    return jnp.dot(a, b, preferred_element_type=F32)


def _dot_nt(a, b):
    return lax.dot_general(a, b, (((1,), (1,)), ((), ())), preferred_element_type=F32)


def _rms(x, g):
    return x * lax.rsqrt(jnp.mean(x * x, axis=-1, keepdims=True) + EPS) * g


def _resident(shape):
    zeros = (0,) * len(shape)
    return pl.BlockSpec(shape, lambda *_: zeros, pipeline_mode=pl.Buffered(1))


def _per_layer(arr, layer):
    zeros = (0,) * (arr.ndim - 1)
    return pl.BlockSpec((None,) + arr.shape[1:], lambda *_: (layer,) + zeros,
                        pipeline_mode=pl.Buffered(1))


STAGE_ROWS = 128
STAGE_COLS = 1024
STAGE_DEPTH = 8
_STAGE_SCRATCH = [pltpu.VMEM((STAGE_DEPTH, STAGE_ROWS, STAGE_COLS), F32),
                  pltpu.SemaphoreType.DMA((STAGE_DEPTH,))]
_HBM = pl.BlockSpec(memory_space=pl.ANY)


def _stage_bf16(jobs, stage, sems):
    pieces = []
    for src, col0, dst in jobs:
        n_rows, n_cols = dst.shape
        for r0 in range(0, n_rows, STAGE_ROWS):
            for c0 in range(0, n_cols, STAGE_COLS):
                pieces.append((src, col0, dst, r0, c0, min(STAGE_COLS, n_cols - c0)))

    def copy(i):
        src, col0, _, r0, c0, nc = pieces[i]
        slot = i % STAGE_DEPTH
        return pltpu.make_async_copy(
            src.at[pl.ds(r0, STAGE_ROWS), pl.ds(col0 + c0, nc)],
            stage.at[slot, :, pl.ds(0, nc)], sems.at[slot])

    for i in range(min(STAGE_DEPTH, len(pieces))):
        copy(i).start()
    for i, (_, _, dst, r0, c0, nc) in enumerate(pieces):
        copy(i).wait()
        dst[r0:r0 + STAGE_ROWS, c0:c0 + nc] = stage[i % STAGE_DEPTH, :, 0:nc].astype(BF16)
        if i + STAGE_DEPTH < len(pieces):
            copy(i + STAGE_DEPTH).start()


def _bias_tiles_kernel(brev_ref, out_ref):
    sub = MAX_DISTANCE
    r = lax.broadcasted_iota(jnp.int32, (sub, sub), 0)
    c = lax.broadcasted_iota(jnp.int32, (sub, sub), 1)
    y = pltpu.roll(jnp.broadcast_to(brev_ref[...], (sub, sub)), 0, 1, stride=1, stride_axis=0)
    by_diag = {0: jnp.where(c <= r, y, NEG).T, 1: jnp.where(c > r, y, 0.0).T}
    for t, off in enumerate((TQ, 0)):
        for a in range(TQ // sub):
            for b in range(TQ // sub):
                d = a - b + off // sub
                fill = NEG if d < 0 else 0.0
                out_ref[t, b * sub:(b + 1) * sub, a * sub:(a + 1) * sub] = by_diag.get(
                    d, jnp.full((sub, sub), fill, F32))


def _bias_tiles(brev):
    return pl.pallas_call(
        _bias_tiles_kernel,
        grid=(N_HEADS,),
        in_specs=[pl.BlockSpec((None, 1, MAX_DISTANCE), lambda h: (h, 0, 0))],
        out_specs=pl.BlockSpec((None, 2, TQ, TQ), lambda h: (h, 0, 0, 0)),
        out_shape=jax.ShapeDtypeStruct((N_HEADS, 2, TQ, TQ), F32),
        name="bias_tiles",
    )(brev)


def _store_heads(ref, z):
    for hd in range(ref.shape[0]):
        ref[hd] = z[:, hd * V_DIM:(hd + 1) * V_DIM]


def _inproj_kernel(x_ref, g_ref, w_hbm, seg_ref, qg_ref, kg_ref,
                   q_ref, k_ref, vt_ref, u_ref, w_ref, stage, sems, *, layer):
    @pl.when((pl.program_id(0) == 0) & (pl.program_id(1) == 0))
    def _():
        _stage_bf16([(w_hbm.at[layer], 0, w_ref)], stage, sems)

    h = _rms(x_ref[...], g_ref[...]).astype(BF16)

    def qk_norm(z, gain):
        z2 = (z * z).astype(BF16)
        ss = jnp.concatenate(
            [_dot(z2[:, t * MXU_TILE:(t + 1) * MXU_TILE], seg_ref[...])
             for t in range(QK_WIDTH // MXU_TILE)], axis=1)
        return z * lax.rsqrt(ss * (1.0 / HEAD_DIM) + EPS) * gain

    zq = _dot(h, w_ref[:, 0:QK_WIDTH])
    zk = _dot(h, w_ref[:, QK_WIDTH:2 * QK_WIDTH])
    _store_heads(q_ref, qk_norm(zq, qg_ref[...]).astype(BF16))
    v = _dot(h, w_ref[:, 2 * QK_WIDTH:2 * QK_WIDTH + ATTN_WIDTH])
    _store_heads(k_ref, qk_norm(zk, kg_ref[...]).astype(BF16))
    u_ref[...] = _dot(h, w_ref[:, 2 * QK_WIDTH + ATTN_WIDTH:]).astype(BF16)
    vt_ref[...] = v.T.astype(BF16)


def _inproj(x, g, w_in, layer, seg, qg, kg):
    B, S, D = x.shape
    n_cols = 2 * QK_WIDTH + ATTN_WIDTH + SSM_WIDTH
    tok = lambda width: pl.BlockSpec((None, TM_PROJ, width), lambda b, i: (b, i, 0))
    heads = pl.BlockSpec((None, N_HEADS, TM_PROJ, V_DIM), lambda b, i: (b, 0, i, 0))
    return pl.pallas_call(
        functools.partial(_inproj_kernel, layer=layer),
        grid=(B, S // TM_PROJ),
        in_specs=[tok(D), _per_layer(g, layer), _HBM, _resident(seg.shape),
                  _per_layer(qg, layer), _per_layer(kg, layer)],
        out_specs=[heads, heads,
                   pl.BlockSpec((None, ATTN_WIDTH, TM_PROJ), lambda b, i: (b, 0, i)),
                   tok(SSM_WIDTH)],
        out_shape=[jax.ShapeDtypeStruct((B, N_HEADS, S, V_DIM), BF16),
                   jax.ShapeDtypeStruct((B, N_HEADS, S, V_DIM), BF16),
                   jax.ShapeDtypeStruct((B, ATTN_WIDTH, S), BF16),
                   jax.ShapeDtypeStruct((B, S, SSM_WIDTH), BF16)],
        scratch_shapes=[pltpu.VMEM((D, n_cols), BF16)] + _STAGE_SCRATCH,
        compiler_params=pltpu.CompilerParams(
            dimension_semantics=("arbitrary", "arbitrary"), vmem_limit_bytes=VMEM_LIMIT),
        name="inproj",
    )(x, g, w_in, seg, qg, kg)


def _attn_kernel(lamv_ref, subln_ref, q_ref, k_ref, vt_ref, bias_ref, o_ref,
                 *, lam_init):
    lane = lax.broadcasted_iota(jnp.int32, (TQ, V_DIM), 1)
    ones = jnp.ones((ONES_ROWS, TQ), BF16)
    lv = lamv_ref[...]
    lam = (jnp.exp(jnp.sum(lv[0:1] * lv[1:2], axis=-1, keepdims=True))
           - jnp.exp(jnp.sum(lv[2:3] * lv[3:4], axis=-1, keepdims=True)) + lam_init)

    n_heads, n_keys = k_ref.shape[0], k_ref.shape[1]
    tasks = [(hd, qi, j, mp) for hd in range(n_heads) for qi in range(n_keys // TQ)
             for j in range(qi + 1) for mp in range(2)]
    qms, m, acc = {}, {}, {}

    def scores(hd, qi, j, mp):
        if (hd, qi) not in qms:
            q = q_ref[hd, qi * TQ:(qi + 1) * TQ, :]
            zero = jnp.zeros_like(q)
            qms[hd, qi] = (jnp.where(lane < HEAD_DIM, q, zero),
                           jnp.where(lane >= HEAD_DIM, q, zero))
        st = _dot_nt(k_ref[hd, j * TQ:(j + 1) * TQ, :], qms[hd, qi][mp])
        bias = {qi: 1, qi - 1: 0}.get(j)
        return st if bias is None else st + bias_ref[hd, bias]

    def softmax_pv(hd, qi, j, mp, st):
        v_rows = slice(hd * V_DIM, (hd + 1) * V_DIM)
        v1t = jnp.concatenate([vt_ref[v_rows, j * TQ:(j + 1) * TQ], ones], axis=0)
        m_cur = jnp.max(st, axis=0, keepdims=True)
        m_new = m_cur if j == 0 else jnp.maximum(m[hd, qi, mp], m_cur)
        p = jnp.exp2(st - m_new)
        pv = _dot(v1t, p.astype(BF16))
        acc[hd, qi, mp] = (pv if j == 0
                           else jnp.exp2(m[hd, qi, mp] - m_new) * acc[hd, qi, mp] + pv)
        m[hd, qi, mp] = m_new
        if j == qi and mp == 1:
            a0, a1 = acc.pop((hd, qi, 0)), acc.pop((hd, qi, 1))
            o_t = a0[:V_DIM] / a0[V_DIM:V_DIM + 1] - lam * (a1[:V_DIM] / a1[V_DIM:V_DIM + 1])
            o_n = (o_t * lax.rsqrt(jnp.mean(o_t * o_t, axis=0, keepdims=True) + EPS)).T
            o_ref[hd, qi * TQ:(qi + 1) * TQ, :] = (
                o_n * subln_ref[...] * (1.0 - lam_init)).astype(BF16)

    pending = {}
    for i in range(len(tasks) + SCORE_LOOKAHEAD):
        if i < len(tasks):
            pending[i] = scores(*tasks[i])
        if i >= SCORE_LOOKAHEAD:
            softmax_pv(*tasks[i - SCORE_LOOKAHEAD], pending.pop(i - SCORE_LOOKAHEAD))


def _attention(lamv, subln, q, k, vt, bias_tiles, lam_init, layer):
    B, _, S, _ = q.shape
    hps = HEADS_PER_STEP
    head = pl.BlockSpec((None, hps, S, V_DIM), lambda b, h: (b, h, 0, 0))
    head_t = pl.BlockSpec((None, hps * V_DIM, S), lambda b, h: (b, h, 0))
    return pl.pallas_call(
        functools.partial(_attn_kernel, lam_init=lam_init),
        grid=(B, N_HEADS // hps),
        in_specs=[_per_layer(lamv, layer), _per_layer(subln, layer), head, head, head_t,
                  pl.BlockSpec((hps, 2, TQ, TQ), lambda b, h: (h, 0, 0, 0))],
        out_specs=head,
        out_shape=jax.ShapeDtypeStruct((B, N_HEADS, S, V_DIM), BF16),
        compiler_params=pltpu.CompilerParams(
            dimension_semantics=("parallel", "parallel"), vmem_limit_bytes=VMEM_LIMIT),
        name="diff_attention",
    )(lamv, subln, q, k, vt, bias_tiles)


def _blockdiag_into(dst, stacked):
    (rows, n), cols = stacked.shape, dst.shape[1]
    m = rows // (cols // n)
    lane_in_block = lax.broadcasted_iota(jnp.int32, (n, cols), 1) % n
    tile_cols = (lane_in_block == lax.broadcasted_iota(jnp.int32, (n, cols), 0)).astype(BF16)
    tiled = _dot(stacked.astype(BF16), tile_cols)
    same_group = (lax.broadcasted_iota(jnp.int32, (rows, cols), 0) // m
                  == lax.broadcasted_iota(jnp.int32, (rows, cols), 1) // n)
    dst[...] = jnp.where(same_group, tiled, 0.0).astype(BF16)


def _ssm_kernel(u_ref, perm_ref, permt_ref, bb_ref, a_ref, cc_ref, dskip_ref, wglu_ref,
                out_ref, wb_ref, c_ref, bur, bui, xr_s, xi_s):
    @pl.when(pl.program_id(0) == 0)
    def _():
        xr_s[...] = jnp.zeros(xr_s.shape, F32)
        xi_s[...] = jnp.zeros(xi_s.shape, F32)
        for part in range(2):
            _blockdiag_into(wb_ref.at[part], bb_ref[part])
            _blockdiag_into(c_ref.at[part], cc_ref[part])


    def to_time_major(c):
        blocks = []
        for t0 in range(c * SCAN_STEPS, (c + 1) * SCAN_STEPS, PERM_STEPS):
            rows_bt = jnp.concatenate(
                [u_ref[b, t0:t0 + PERM_STEPS, :] for b in range(SUBLANES)], axis=0)
            blocks.append(_dot(perm_ref[...], rows_bt))
        return jnp.concatenate(blocks, axis=0)

    def b_project(c, u):
        ub = u.astype(BF16)
        for j in range(STATE_LANES // MXU_TILE):
            kt = (j * MXU_TILE // SSM_STATE * SSM_GROUP) // MXU_TILE
            rows = slice(kt * MXU_TILE, (kt + 1) * MXU_TILE)
            cols = slice(j * MXU_TILE, (j + 1) * MXU_TILE)
            bur[c, :, cols] = _dot(ub[:, rows], wb_ref[0, rows, cols])
            bui[c, :, cols] = _dot(ub[:, rows], wb_ref[1, rows, cols])

    def scan(c):
        for s in range(STATE_LANES // SCAN_STRIP):
            sl = slice(s * SCAN_STRIP, (s + 1) * SCAN_STRIP)
            a_r = a_ref[0, :, sl]
            a_i = a_ref[1, :, sl]
            xr = xr_s[:, sl]
            xi = xi_s[:, sl]
            for t in range(SCAN_STEPS):
                step = slice(t * SUBLANES, (t + 1) * SUBLANES)
                xr, xi = (a_r * xr - a_i * xi + bur[c, step, sl],
                          a_r * xi + a_i * xr + bui[c, step, sl])
                bur[c, step, sl] = xr
                bui[c, step, sl] = xi
            xr_s[:, sl] = xr
            xi_s[:, sl] = xi

    def c_project(c, u):
        halves = []
        k_per_half = STATE_LANES // 2
        n_per_half = SSM_WIDTH // 2
        for m in range(2):
            rows = slice(m * k_per_half, (m + 1) * k_per_half)
            cols = slice(m * n_per_half, (m + 1) * n_per_half)
            halves.append(_dot(bur[c, :, rows].astype(BF16), c_ref[0, rows, cols])
                          + _dot(bui[c, :, rows].astype(BF16), c_ref[1, rows, cols]))
        return jax.nn.gelu(jnp.concatenate(halves, axis=-1) + dskip_ref[...] * u)

    def to_batch_major(c, s):
        sb = s.astype(BF16)
        rows_per_block = PERM_STEPS * SUBLANES
        for i, t0 in enumerate(range(c * SCAN_STEPS, (c + 1) * SCAN_STEPS, PERM_STEPS)):
            rows_bt = _dot(permt_ref[...], sb[i * rows_per_block:(i + 1) * rows_per_block])
            for b in range(SUBLANES):
                out_ref[b, t0:t0 + PERM_STEPS, :] = rows_bt[
                    b * PERM_STEPS:(b + 1) * PERM_STEPS].astype(BF16)

    chunks = range(SCAN_CHUNKS)
    us = [to_time_major(c) for c in chunks]
    for c in chunks:
        b_project(c, us[c])
    for c in chunks:
        scan(c)
    ss = [c_project(c, us[c]) for c in chunks]
    ss = [s * jax.nn.sigmoid(_dot(s.astype(BF16), wglu_ref[...])) for s in ss]
    for c in chunks:
        to_batch_major(c, ss[c])


def _ssm(u, bb, a, cc, dskip, wglu, layer):
    rows = SCAN_STEPS * SUBLANES
    B, S, _ = u.shape
    r = jnp.arange(PERM_STEPS * B)
    perm = (r[None, :] == ((r % B) * PERM_STEPS + r // B)[:, None]).astype(BF16)
    blk = pl.BlockSpec((B, SCAN_CHUNKS * SCAN_STEPS, SSM_WIDTH), lambda c: (0, c, 0))
    return pl.pallas_call(
        _ssm_kernel,
        grid=(S // (SCAN_CHUNKS * SCAN_STEPS),),
        in_specs=[blk, _resident(perm.shape), _resident(perm.shape),
                  _per_layer(bb, layer), _per_layer(a, layer), _per_layer(cc, layer),
                  _per_layer(dskip, layer), _per_layer(wglu, layer)],
        out_specs=blk,
        out_shape=jax.ShapeDtypeStruct((B, S, SSM_WIDTH), BF16),
        scratch_shapes=[pltpu.VMEM((2, SSM_WIDTH, STATE_LANES), BF16),
                        pltpu.VMEM((2, STATE_LANES, SSM_WIDTH), BF16),
                        pltpu.VMEM((SCAN_CHUNKS, rows, STATE_LANES), F32),
                        pltpu.VMEM((SCAN_CHUNKS, rows, STATE_LANES), F32),
                        pltpu.VMEM((SUBLANES, STATE_LANES), F32),
                        pltpu.VMEM((SUBLANES, STATE_LANES), F32)],
        compiler_params=pltpu.CompilerParams(
            dimension_semantics=("arbitrary",), vmem_limit_bytes=VMEM_LIMIT),
        name="s5_branch",
    )(u, perm, perm.T, bb, a, cc, dskip, wglu)


def _merge_ffn_kernel(x_ref, o_ref, s_ref, g1_ref, g2_ref,
                      w_in_hbm, wa_hbm, wb_hbm, wo_hbm, w1_hbm, w3_hbm, w2_hbm, out_ref,
                      wg_ref, wa_ref, wb_ref, wo_ref, w1_ref, w3_ref, w2_ref,
                      acc_s, stage, sems, *, layer):
    @pl.when((pl.program_id(0) == 0) & (pl.program_id(1) == 0))
    def _():
        gate_col0 = w_in_hbm.shape[2] - wg_ref.shape[1]
        _stage_bf16([(w_in_hbm.at[layer], gate_col0, wg_ref), (wa_hbm.at[layer], 0, wa_ref),
                     (wb_hbm.at[layer], 0, wb_ref), (wo_hbm.at[layer], 0, wo_ref),
                     (w1_hbm.at[layer], 0, w1_ref), (w3_hbm.at[layer], 0, w3_ref),
                     (w2_hbm.at[layer], 0, w2_ref)], stage, sems)

    x = x_ref[...]
    h = _rms(x, g1_ref[...]).astype(BF16)
    o = jnp.concatenate([o_ref[hd] for hd in range(N_HEADS)], axis=1)
    mixed = jax.nn.sigmoid(_dot(h, wg_ref[:, 0:D_MODEL])) * _dot(o, wa_ref[...])
    mixed += jax.nn.sigmoid(_dot(h, wg_ref[:, D_MODEL:])) * _dot(s_ref[...], wb_ref[...])
    x1 = x + _dot(mixed.astype(BF16), wo_ref[...])

    h2 = _rms(x1, g2_ref[...]).astype(BF16)
    acc_s[...] = x1

    for c in range(D_FF // FF_CHUNK):
        cols = slice(c * FF_CHUNK, (c + 1) * FF_CHUNK)
        t = jax.nn.silu(_dot(h2, w1_ref[:, cols])) * _dot(h2, w3_ref[:, cols])
        acc_s[...] += _dot(t.astype(BF16), w2_ref[cols, :])
    out_ref[...] = acc_s[...]


def _merge_ffn(x, o, s2, g1, g2, w_in, w_a, w_b, w_o, w1, w3, w2, layer):
    B, S, D = x.shape
    tok = lambda width: pl.BlockSpec((None, TM_FFN, width), lambda b, i: (b, i, 0))
    bf16_weight = lambda shape: pltpu.VMEM(shape, BF16)
    return pl.pallas_call(
        functools.partial(_merge_ffn_kernel, layer=layer),
        grid=(B, S // TM_FFN),
        in_specs=[tok(D),
                  pl.BlockSpec((None, N_HEADS, TM_FFN, V_DIM), lambda b, i: (b, 0, i, 0)),
                  tok(SSM_WIDTH),
                  _per_layer(g1, layer), _per_layer(g2, layer)] + [_HBM] * 7,
        out_specs=tok(D),
        out_shape=jax.ShapeDtypeStruct((B, S, D), F32),
        scratch_shapes=[bf16_weight((D, 2 * D)), bf16_weight(w_a.shape[1:]),
                        bf16_weight(w_b.shape[1:]), bf16_weight(w_o.shape[1:]),
                        bf16_weight(w1.shape[1:]), bf16_weight(w3.shape[1:]),
                        bf16_weight(w2.shape[1:]),
                        pltpu.VMEM((TM_FFN, D), F32)] + _STAGE_SCRATCH,
        compiler_params=pltpu.CompilerParams(
            dimension_semantics=("arbitrary", "arbitrary"), vmem_limit_bytes=VMEM_LIMIT),
        name="merge_ffn",
    )(x, o, s2, g1, g2, w_in, w_a, w_b, w_o, w1, w3, w2)


def _t5_bucket(n):
    max_exact = N_BUCKETS // 2
    is_small = n < max_exact
    nf = jnp.maximum(n, 1).astype(F32)
    large = max_exact + (jnp.log(nf / max_exact) / math.log(MAX_DISTANCE / max_exact)
                         * (N_BUCKETS - max_exact)).astype(jnp.int32)
    large = jnp.minimum(large, N_BUCKETS - 1)
    return jnp.where(is_small, n, large)


def _ssm_params(lam_re, lam_im, b_re, b_im, c_re, c_im, log_step):
    step = jnp.exp(log_step.astype(F32))[..., None]
    lr = lam_re.astype(F32)
    li = lam_im.astype(F32)
    decay = jnp.exp(lr * step)
    ab_re = decay * jnp.cos(li * step)
    ab_im = decay * jnp.sin(li * step)
    nr = ab_re - 1.0
    ni = ab_im
    den = lr * lr + li * li
    f_re = (nr * lr + ni * li) / den
    f_im = (ni * lr - nr * li) / den
    br = b_re.astype(F32)
    bi = b_im.astype(F32)
    bb_re = f_re[..., None] * br - f_im[..., None] * bi
    bb_im = f_re[..., None] * bi + f_im[..., None] * br

    def stack_transposed(w):
        *lead, g, n, m = w.shape
        return jnp.swapaxes(w, -1, -2).reshape(*lead, g * m, n)

    depth = lr.shape[0]
    a = jnp.stack([ab_re, ab_im], axis=1).reshape(depth, 2, 1, STATE_LANES)
    return (stack_transposed(jnp.stack([bb_re, bb_im], axis=1)),
            jnp.broadcast_to(a, (depth, 2, SUBLANES, STATE_LANES)),
            stack_transposed(jnp.stack([c_re.astype(F32), -c_im.astype(F32)], axis=1)))


def kernel(x, rel_bias, norm_mix, w_in, q_gain, k_gain, lambda_q1, lambda_k1, lambda_q2, lambda_k2, subln, w_a, lam_re, lam_im, b_re, b_im, c_re, c_im, d_skip, log_step, w_glu, w_b, w_o, norm_ffn, w1, w3, w2):
    B, S, D = x.shape
    depth = w_in.shape[0]
    assert B == SUBLANES and D == D_MODEL

    table = rel_bias.astype(F32)
    dist = (-jnp.arange(MAX_DISTANCE)) % MAX_DISTANCE
    brev = (table[_t5_bucket(dist)] - table[N_BUCKETS - 1][None]).T * LOG2E
    bias_tiles = _bias_tiles(brev[:, None, :])

    seg = (jnp.arange(MXU_TILE)[:, None] // HEAD_DIM
           == jnp.arange(MXU_TILE)[None, :] // HEAD_DIM).astype(BF16)

    row = lambda p: p.astype(F32)[:, None, :]
    heads = QK_WIDTH // HEAD_DIM
    qg = row(jnp.tile(q_gain, (1, heads))) * (HEAD_DIM ** -0.5 * LOG2E)
    kg = row(jnp.tile(k_gain, (1, heads)))
    lamv = jnp.stack([lambda_q1, lambda_k1, lambda_q2, lambda_k2], axis=1).astype(F32)
    bb, a, cc = _ssm_params(lam_re, lam_im, b_re, b_im, c_re, c_im, log_step)
    g_mix, g_ffn, g_sub, dskip = row(norm_mix), row(norm_ffn), row(subln), row(d_skip)
    wglu = w_glu.astype(BF16)

    for l in range(depth):
        lam_init = 0.8 - 0.6 * math.exp(-0.3 * l)
        q, k, vt, u = _inproj(x, g_mix, w_in, l, seg, qg, kg)
        o = _attention(lamv, g_sub, q, k, vt, bias_tiles, lam_init, l)
        s2 = _ssm(u, bb, a, cc, dskip, wglu, l)
        x = _merge_ffn(x, o, s2, g_mix, g_ffn, w_in, w_a, w_b, w_o, w1, w3, w2, l)
    return x
```

```python
import functools
import math

import jax
import jax.numpy as jnp
from jax import lax
from jax.experimental import pallas as pl
from jax.experimental.pallas import tpu as pltpu

D_MODEL = 1024
N_HEADS = 4
HEAD_DIM = 64
V_DIM = 2 * HEAD_DIM
QK_WIDTH = N_HEADS * 2 * HEAD_DIM
ATTN_WIDTH = N_HEADS * V_DIM
SSM_WIDTH = D_MODEL // 2
SSM_GROUP = 16
SSM_GROUPS = SSM_WIDTH // SSM_GROUP
SSM_STATE = 64
STATE_LANES = SSM_GROUPS * SSM_STATE
D_FF = 2816
N_BUCKETS = 32
MAX_DISTANCE = 128
EPS = 1e-6
NEG = -1e30

MXU_TILE = 256
SUBLANES = 8

TM_PROJ = 1024
TM_FFN = 1024
TQ = 256
ONES_ROWS = 16
HEADS_PER_STEP = 2
SCORE_LOOKAHEAD = 8
LOG2E = math.log2(math.e)
SCAN_STEPS = 64
SCAN_CHUNKS = 4
PERM_STEPS = 16
SCAN_STRIP = 512
FF_CHUNK = 256
VMEM_LIMIT = 62 * 1024 * 1024

BF16 = jnp.bfloat16
F32 = jnp.float32


def _dot(a, b):
    return jnp.dot(a, b, preferred_element_type=F32)


def _dot_nt(a, b):
    return lax.dot_general(a, b, (((1,), (1,)), ((), ())), preferred_element_type=F32)


def _rms(x, g):
    return x * lax.rsqrt(jnp.mean(x * x, axis=-1, keepdims=True) + EPS) * g


def _resident(shape):
    zeros = (0,) * len(shape)
    return pl.BlockSpec(shape, lambda *_: zeros, pipeline_mode=pl.Buffered(1))


def _per_layer(arr, layer):
    zeros = (0,) * (arr.ndim - 1)
    return pl.BlockSpec((None,) + arr.shape[1:], lambda *_: (layer,) + zeros,
                        pipeline_mode=pl.Buffered(1))


STAGE_ROWS = 128
STAGE_COLS = 1024
STAGE_DEPTH = 8
_STAGE_SCRATCH = [pltpu.VMEM((STAGE_DEPTH, STAGE_ROWS, STAGE_COLS), F32),
                  pltpu.SemaphoreType.DMA((STAGE_DEPTH,))]
_HBM = pl.BlockSpec(memory_space=pl.ANY)


def _stage_bf16(jobs, stage, sems):
    pieces = []
    for src, col0, dst in jobs:
        n_rows, n_cols = dst.shape
        for r0 in range(0, n_rows, STAGE_ROWS):
            for c0 in range(0, n_cols, STAGE_COLS):
                pieces.append((src, col0, dst, r0, c0, min(STAGE_COLS, n_cols - c0)))

    def copy(i):
        src, col0, _, r0, c0, nc = pieces[i]
        slot = i % STAGE_DEPTH
        return pltpu.make_async_copy(
            src.at[pl.ds(r0, STAGE_ROWS), pl.ds(col0 + c0, nc)],
            stage.at[slot, :, pl.ds(0, nc)], sems.at[slot])

    for i in range(min(STAGE_DEPTH, len(pieces))):
        copy(i).start()
    for i, (_, _, dst, r0, c0, nc) in enumerate(pieces):
        copy(i).wait()
        dst[r0:r0 + STAGE_ROWS, c0:c0 + nc] = stage[i % STAGE_DEPTH, :, 0:nc].astype(BF16)
        if i + STAGE_DEPTH < len(pieces):
            copy(i + STAGE_DEPTH).start()


def _bias_tiles_kernel(brev_ref, out_ref):
    sub = MAX_DISTANCE
    r = lax.broadcasted_iota(jnp.int32, (sub, sub), 0)
    c = lax.broadcasted_iota(jnp.int32, (sub, sub), 1)
    y = pltpu.roll(jnp.broadcast_to(brev_ref[...], (sub, sub)), 0, 1, stride=1, stride_axis=0)
    by_diag = {0: jnp.where(c <= r, y, NEG).T, 1: jnp.where(c > r, y, 0.0).T}
    for t, off in enumerate((TQ, 0)):
        for a in range(TQ // sub):
            for b in range(TQ // sub):
                d = a - b + off // sub
                fill = NEG if d < 0 else 0.0
                out_ref[t, b * sub:(b + 1) * sub, a * sub:(a + 1) * sub] = by_diag.get(
                    d, jnp.full((sub, sub), fill, F32))


def _bias_tiles(brev):
    return pl.pallas_call(
        _bias_tiles_kernel,
        grid=(N_HEADS,),
        in_specs=[pl.BlockSpec((None, 1, MAX_DISTANCE), lambda h: (h, 0, 0))],
        out_specs=pl.BlockSpec((None, 2, TQ, TQ), lambda h: (h, 0, 0, 0)),
        out_shape=jax.ShapeDtypeStruct((N_HEADS, 2, TQ, TQ), F32),
        name="bias_tiles",
    )(brev)


def _store_heads(ref, z):
    for hd in range(ref.shape[0]):
        ref[hd] = z[:, hd * V_DIM:(hd + 1) * V_DIM]


def _inproj_kernel(x_ref, g_ref, w_hbm, seg_ref, qg_ref, kg_ref,
                   q_ref, k_ref, vt_ref, u_ref, w_ref, stage, sems, *, layer):
    @pl.when((pl.program_id(0) == 0) & (pl.program_id(1) == 0))
    def _():
        _stage_bf16([(w_hbm.at[layer], 0, w_ref)], stage, sems)

    h = _rms(x_ref[...], g_ref[...]).astype(BF16)

    def qk_norm(z, gain):
        z2 = (z * z).astype(BF16)
        ss = jnp.concatenate(
            [_dot(z2[:, t * MXU_TILE:(t + 1) * MXU_TILE], seg_ref[...])
             for t in range(QK_WIDTH // MXU_TILE)], axis=1)
        return z * lax.rsqrt(ss * (1.0 / HEAD_DIM) + EPS) * gain

    zq = _dot(h, w_ref[:, 0:QK_WIDTH])
    zk = _dot(h, w_ref[:, QK_WIDTH:2 * QK_WIDTH])
    _store_heads(q_ref, qk_norm(zq, qg_ref[...]).astype(BF16))
    v = _dot(h, w_ref[:, 2 * QK_WIDTH:2 * QK_WIDTH + ATTN_WIDTH])
    _store_heads(k_ref, qk_norm(zk, kg_ref[...]).astype(BF16))
    u_ref[...] = _dot(h, w_ref[:, 2 * QK_WIDTH + ATTN_WIDTH:]).astype(BF16)
    vt_ref[...] = v.T.astype(BF16)


def _inproj(x, g, w_in, layer, seg, qg, kg):
    B, S, D = x.shape
    n_cols = 2 * QK_WIDTH + ATTN_WIDTH + SSM_WIDTH
    tok = lambda width: pl.BlockSpec((None, TM_PROJ, width), lambda b, i: (b, i, 0))
    heads = pl.BlockSpec((None, N_HEADS, TM_PROJ, V_DIM), lambda b, i: (b, 0, i, 0))
    return pl.pallas_call(
        functools.partial(_inproj_kernel, layer=layer),
        grid=(B, S // TM_PROJ),
        in_specs=[tok(D), _per_layer(g, layer), _HBM, _resident(seg.shape),
                  _per_layer(qg, layer), _per_layer(kg, layer)],
        out_specs=[heads, heads,
                   pl.BlockSpec((None, ATTN_WIDTH, TM_PROJ), lambda b, i: (b, 0, i)),
                   tok(SSM_WIDTH)],
        out_shape=[jax.ShapeDtypeStruct((B, N_HEADS, S, V_DIM), BF16),
                   jax.ShapeDtypeStruct((B, N_HEADS, S, V_DIM), BF16),
                   jax.ShapeDtypeStruct((B, ATTN_WIDTH, S), BF16),
                   jax.ShapeDtypeStruct((B, S, SSM_WIDTH), BF16)],
        scratch_shapes=[pltpu.VMEM((D, n_cols), BF16)] + _STAGE_SCRATCH,
        compiler_params=pltpu.CompilerParams(
            dimension_semantics=("arbitrary", "arbitrary"), vmem_limit_bytes=VMEM_LIMIT),
        name="inproj",
    )(x, g, w_in, seg, qg, kg)


def _attn_kernel(lamv_ref, subln_ref, q_ref, k_ref, vt_ref, bias_ref, o_ref,
                 *, lam_init):
    lane = lax.broadcasted_iota(jnp.int32, (TQ, V_DIM), 1)
    ones = jnp.ones((ONES_ROWS, TQ), BF16)
    lv = lamv_ref[...]
    lam = (jnp.exp(jnp.sum(lv[0:1] * lv[1:2], axis=-1, keepdims=True))
           - jnp.exp(jnp.sum(lv[2:3] * lv[3:4], axis=-1, keepdims=True)) + lam_init)

    n_heads, n_keys = k_ref.shape[0], k_ref.shape[1]
    tasks = [(hd, qi, j, mp) for hd in range(n_heads) for qi in range(n_keys // TQ)
             for j in range(qi + 1) for mp in range(2)]
    qms, m, acc = {}, {}, {}

    def scores(hd, qi, j, mp):
        if (hd, qi) not in qms:
            q = q_ref[hd, qi * TQ:(qi + 1) * TQ, :]
            zero = jnp.zeros_like(q)
            qms[hd, qi] = (jnp.where(lane < HEAD_DIM, q, zero),
                           jnp.where(lane >= HEAD_DIM, q, zero))
        st = _dot_nt(k_ref[hd, j * TQ:(j + 1) * TQ, :], qms[hd, qi][mp])
        bias = {qi: 1, qi - 1: 0}.get(j)
        return st if bias is None else st + bias_ref[hd, bias]

    def softmax_pv(hd, qi, j, mp, st):
        v_rows = slice(hd * V_DIM, (hd + 1) * V_DIM)
        v1t = jnp.concatenate([vt_ref[v_rows, j * TQ:(j + 1) * TQ], ones], axis=0)
        m_cur = jnp.max(st, axis=0, keepdims=True)
        m_new = m_cur if j == 0 else jnp.maximum(m[hd, qi, mp], m_cur)
        p = jnp.exp2(st - m_new)
        pv = _dot(v1t, p.astype(BF16))
        acc[hd, qi, mp] = (pv if j == 0
                           else jnp.exp2(m[hd, qi, mp] - m_new) * acc[hd, qi, mp] + pv)
        m[hd, qi, mp] = m_new
        if j == qi and mp == 1:
            a0, a1 = acc.pop((hd, qi, 0)), acc.pop((hd, qi, 1))
            o_t = a0[:V_DIM] / a0[V_DIM:V_DIM + 1] - lam * (a1[:V_DIM] / a1[V_DIM:V_DIM + 1])
            o_n = (o_t * lax.rsqrt(jnp.mean(o_t * o_t, axis=0, keepdims=True) + EPS)).T
            o_ref[hd, qi * TQ:(qi + 1) * TQ, :] = (
                o_n * subln_ref[...] * (1.0 - lam_init)).astype(BF16)

    pending = {}
    for i in range(len(tasks) + SCORE_LOOKAHEAD):
        if i < len(tasks):
            pending[i] = scores(*tasks[i])
        if i >= SCORE_LOOKAHEAD:
            softmax_pv(*tasks[i - SCORE_LOOKAHEAD], pending.pop(i - SCORE_LOOKAHEAD))


def _attention(lamv, subln, q, k, vt, bias_tiles, lam_init, layer):
    B, _, S, _ = q.shape
    hps = HEADS_PER_STEP
    head = pl.BlockSpec((None, hps, S, V_DIM), lambda b, h: (b, h, 0, 0))
    head_t = pl.BlockSpec((None, hps * V_DIM, S), lambda b, h: (b, h, 0))
    return pl.pallas_call(
        functools.partial(_attn_kernel, lam_init=lam_init),
        grid=(B, N_HEADS // hps),
        in_specs=[_per_layer(lamv, layer), _per_layer(subln, layer), head, head, head_t,
                  pl.BlockSpec((hps, 2, TQ, TQ), lambda b, h: (h, 0, 0, 0))],
        out_specs=head,
        out_shape=jax.ShapeDtypeStruct((B, N_HEADS, S, V_DIM), BF16),
        compiler_params=pltpu.CompilerParams(
            dimension_semantics=("parallel", "parallel"), vmem_limit_bytes=VMEM_LIMIT),
        name="diff_attention",
    )(lamv, subln, q, k, vt, bias_tiles)


def _blockdiag_into(dst, stacked):
    (rows, n), cols = stacked.shape, dst.shape[1]
    m = rows // (cols // n)
    lane_in_block = lax.broadcasted_iota(jnp.int32, (n, cols), 1) % n
    tile_cols = (lane_in_block == lax.broadcasted_iota(jnp.int32, (n, cols), 0)).astype(BF16)
    tiled = _dot(stacked.astype(BF16), tile_cols)
    same_group = (lax.broadcasted_iota(jnp.int32, (rows, cols), 0) // m
                  == lax.broadcasted_iota(jnp.int32, (rows, cols), 1) // n)
    dst[...] = jnp.where(same_group, tiled, 0.0).astype(BF16)


def _ssm_kernel(u_ref, perm_ref, permt_ref, bb_ref, a_ref, cc_ref, dskip_ref, wglu_ref,
                out_ref, wb_ref, c_ref, bur, bui, xr_s, xi_s):
    @pl.when(pl.program_id(0) == 0)
    def _():
        xr_s[...] = jnp.zeros(xr_s.shape, F32)
        xi_s[...] = jnp.zeros(xi_s.shape, F32)
        for part in range(2):
            _blockdiag_into(wb_ref.at[part], bb_ref[part])
            _blockdiag_into(c_ref.at[part], cc_ref[part])


    def to_time_major(c):
        blocks = []
        for t0 in range(c * SCAN_STEPS, (c + 1) * SCAN_STEPS, PERM_STEPS):
            rows_bt = jnp.concatenate(
                [u_ref[b, t0:t0 + PERM_STEPS, :] for b in range(SUBLANES)], axis=0)
            blocks.append(_dot(perm_ref[...], rows_bt))
        return jnp.concatenate(blocks, axis=0)

    def b_project(c, u):
        ub = u.astype(BF16)
        for j in range(STATE_LANES // MXU_TILE):
            kt = (j * MXU_TILE // SSM_STATE * SSM_GROUP) // MXU_TILE
            rows = slice(kt * MXU_TILE, (kt + 1) * MXU_TILE)
            cols = slice(j * MXU_TILE, (j + 1) * MXU_TILE)
            bur[c, :, cols] = _dot(ub[:, rows], wb_ref[0, rows, cols])
            bui[c, :, cols] = _dot(ub[:, rows], wb_ref[1, rows, cols])

    def scan(c):
        for s in range(STATE_LANES // SCAN_STRIP):
            sl = slice(s * SCAN_STRIP, (s + 1) * SCAN_STRIP)
            a_r = a_ref[0, :, sl]
            a_i = a_ref[1, :, sl]
            xr = xr_s[:, sl]
            xi = xi_s[:, sl]
            for t in range(SCAN_STEPS):
                step = slice(t * SUBLANES, (t + 1) * SUBLANES)
                xr, xi = (a_r * xr - a_i * xi + bur[c, step, sl],
                          a_r * xi + a_i * xr + bui[c, step, sl])
                bur[c, step, sl] = xr
                bui[c, step, sl] = xi
            xr_s[:, sl] = xr
            xi_s[:, sl] = xi

    def c_project(c, u):
        halves = []
        k_per_half = STATE_LANES // 2
        n_per_half = SSM_WIDTH // 2
        for m in range(2):
            rows = slice(m * k_per_half, (m + 1) * k_per_half)
            cols = slice(m * n_per_half, (m + 1) * n_per_half)
            halves.append(_dot(bur[c, :, rows].astype(BF16), c_ref[0, rows, cols])
                          + _dot(bui[c, :, rows].astype(BF16), c_ref[1, rows, cols]))
        return jax.nn.gelu(jnp.concatenate(halves, axis=-1) + dskip_ref[...] * u)

    def to_batch_major(c, s):
        sb = s.astype(BF16)
        rows_per_block = PERM_STEPS * SUBLANES
        for i, t0 in enumerate(range(c * SCAN_STEPS, (c + 1) * SCAN_STEPS, PERM_STEPS)):
            rows_bt = _dot(permt_ref[...], sb[i * rows_per_block:(i + 1) * rows_per_block])
            for b in range(SUBLANES):
                out_ref[b, t0:t0 + PERM_STEPS, :] = rows_bt[
                    b * PERM_STEPS:(b + 1) * PERM_STEPS].astype(BF16)

    chunks = range(SCAN_CHUNKS)
    us = [to_time_major(c) for c in chunks]
    for c in chunks:
        b_project(c, us[c])
    for c in chunks:
        scan(c)
    ss = [c_project(c, us[c]) for c in chunks]
    ss = [s * jax.nn.sigmoid(_dot(s.astype(BF16), wglu_ref[...])) for s in ss]
    for c in chunks:
        to_batch_major(c, ss[c])


def _ssm(u, bb, a, cc, dskip, wglu, layer):
    rows = SCAN_STEPS * SUBLANES
    B, S, _ = u.shape
    r = jnp.arange(PERM_STEPS * B)
    perm = (r[None, :] == ((r % B) * PERM_STEPS + r // B)[:, None]).astype(BF16)
    blk = pl.BlockSpec((B, SCAN_CHUNKS * SCAN_STEPS, SSM_WIDTH), lambda c: (0, c, 0))
    return pl.pallas_call(
        _ssm_kernel,
        grid=(S // (SCAN_CHUNKS * SCAN_STEPS),),
        in_specs=[blk, _resident(perm.shape), _resident(perm.shape),
                  _per_layer(bb, layer), _per_layer(a, layer), _per_layer(cc, layer),
                  _per_layer(dskip, layer), _per_layer(wglu, layer)],
        out_specs=blk,
        out_shape=jax.ShapeDtypeStruct((B, S, SSM_WIDTH), BF16),
        scratch_shapes=[pltpu.VMEM((2, SSM_WIDTH, STATE_LANES), BF16),
                        pltpu.VMEM((2, STATE_LANES, SSM_WIDTH), BF16),
                        pltpu.VMEM((SCAN_CHUNKS, rows, STATE_LANES), F32),
                        pltpu.VMEM((SCAN_CHUNKS, rows, STATE_LANES), F32),
                        pltpu.VMEM((SUBLANES, STATE_LANES), F32),
                        pltpu.VMEM((SUBLANES, STATE_LANES), F32)],
        compiler_params=pltpu.CompilerParams(
            dimension_semantics=("arbitrary",), vmem_limit_bytes=VMEM_LIMIT),
        name="s5_branch",
    )(u, perm, perm.T, bb, a, cc, dskip, wglu)


def _merge_ffn_kernel(x_ref, o_ref, s_ref, g1_ref, g2_ref,
                      w_in_hbm, wa_hbm, wb_hbm, wo_hbm, w1_hbm, w3_hbm, w2_hbm, out_ref,
                      wg_ref, wa_ref, wb_ref, wo_ref, w1_ref, w3_ref, w2_ref,
                      acc_s, stage, sems, *, layer):
    @pl.when((pl.program_id(0) == 0) & (pl.program_id(1) == 0))
    def _():
        gate_col0 = w_in_hbm.shape[2] - wg_ref.shape[1]
        _stage_bf16([(w_in_hbm.at[layer], gate_col0, wg_ref), (wa_hbm.at[layer], 0, wa_ref),
                     (wb_hbm.at[layer], 0, wb_ref), (wo_hbm.at[layer], 0, wo_ref),
                     (w1_hbm.at[layer], 0, w1_ref), (w3_hbm.at[layer], 0, w3_ref),
                     (w2_hbm.at[layer], 0, w2_ref)], stage, sems)

    x = x_ref[...]
    h = _rms(x, g1_ref[...]).astype(BF16)
    o = jnp.concatenate([o_ref[hd] for hd in range(N_HEADS)], axis=1)
    mixed = jax.nn.sigmoid(_dot(h, wg_ref[:, 0:D_MODEL])) * _dot(o, wa_ref[...])
    mixed += jax.nn.sigmoid(_dot(h, wg_ref[:, D_MODEL:])) * _dot(s_ref[...], wb_ref[...])
    x1 = x + _dot(mixed.astype(BF16), wo_ref[...])

    h2 = _rms(x1, g2_ref[...]).astype(BF16)
    acc_s[...] = x1

    for c in range(D_FF // FF_CHUNK):
        cols = slice(c * FF_CHUNK, (c + 1) * FF_CHUNK)
        t = jax.nn.silu(_dot(h2, w1_ref[:, cols])) * _dot(h2, w3_ref[:, cols])
        acc_s[...] += _dot(t.astype(BF16), w2_ref[cols, :])
    out_ref[...] = acc_s[...]


def _merge_ffn(x, o, s2, g1, g2, w_in, w_a, w_b, w_o, w1, w3, w2, layer):
    B, S, D = x.shape
    tok = lambda width: pl.BlockSpec((None, TM_FFN, width), lambda b, i: (b, i, 0))
    bf16_weight = lambda shape: pltpu.VMEM(shape, BF16)
    return pl.pallas_call(
        functools.partial(_merge_ffn_kernel, layer=layer),
        grid=(B, S // TM_FFN),
        in_specs=[tok(D),
                  pl.BlockSpec((None, N_HEADS, TM_FFN, V_DIM), lambda b, i: (b, 0, i, 0)),
                  tok(SSM_WIDTH),
                  _per_layer(g1, layer), _per_layer(g2, layer)] + [_HBM] * 7,
        out_specs=tok(D),
        out_shape=jax.ShapeDtypeStruct((B, S, D), F32),
        scratch_shapes=[bf16_weight((D, 2 * D)), bf16_weight(w_a.shape[1:]),
                        bf16_weight(w_b.shape[1:]), bf16_weight(w_o.shape[1:]),
                        bf16_weight(w1.shape[1:]), bf16_weight(w3.shape[1:]),
                        bf16_weight(w2.shape[1:]),
                        pltpu.VMEM((TM_FFN, D), F32)] + _STAGE_SCRATCH,
        compiler_params=pltpu.CompilerParams(
            dimension_semantics=("arbitrary", "arbitrary"), vmem_limit_bytes=VMEM_LIMIT),
        name="merge_ffn",
    )(x, o, s2, g1, g2, w_in, w_a, w_b, w_o, w1, w3, w2)


def _t5_bucket(n):
    max_exact = N_BUCKETS // 2
    is_small = n < max_exact
    nf = jnp.maximum(n, 1).astype(F32)
    large = max_exact + (jnp.log(nf / max_exact) / math.log(MAX_DISTANCE / max_exact)
                         * (N_BUCKETS - max_exact)).astype(jnp.int32)
    large = jnp.minimum(large, N_BUCKETS - 1)
    return jnp.where(is_small, n, large)


def _ssm_params(lam_re, lam_im, b_re, b_im, c_re, c_im, log_step):
    step = jnp.exp(log_step.astype(F32))[..., None]
    lr = lam_re.astype(F32)
    li = lam_im.astype(F32)
    decay = jnp.exp(lr * step)
    ab_re = decay * jnp.cos(li * step)
    ab_im = decay * jnp.sin(li * step)
    nr = ab_re - 1.0
    ni = ab_im
    den = lr * lr + li * li
    f_re = (nr * lr + ni * li) / den
    f_im = (ni * lr - nr * li) / den
    br = b_re.astype(F32)
    bi = b_im.astype(F32)
    bb_re = f_re[..., None] * br - f_im[..., None] * bi
    bb_im = f_re[..., None] * bi + f_im[..., None] * br

    def stack_transposed(w):
        *lead, g, n, m = w.shape
        return jnp.swapaxes(w, -1, -2).reshape(*lead, g * m, n)

    depth = lr.shape[0]
    a = jnp.stack([ab_re, ab_im], axis=1).reshape(depth, 2, 1, STATE_LANES)
    return (stack_transposed(jnp.stack([bb_re, bb_im], axis=1)),
            jnp.broadcast_to(a, (depth, 2, SUBLANES, STATE_LANES)),
            stack_transposed(jnp.stack([c_re.astype(F32), -c_im.astype(F32)], axis=1)))


def kernel(x, rel_bias, norm_mix, w_in, q_gain, k_gain, lambda_q1, lambda_k1, lambda_q2, lambda_k2, subln, w_a, lam_re, lam_im, b_re, b_im, c_re, c_im, d_skip, log_step, w_glu, w_b, w_o, norm_ffn, w1, w3, w2):
    B, S, D = x.shape
    depth = w_in.shape[0]
    assert B == SUBLANES and D == D_MODEL

    table = rel_bias.astype(F32)
    dist = (-jnp.arange(MAX_DISTANCE)) % MAX_DISTANCE
    brev = (table[_t5_bucket(dist)] - table[N_BUCKETS - 1][None]).T * LOG2E
    bias_tiles = _bias_tiles(brev[:, None, :])

    seg = (jnp.arange(MXU_TILE)[:, None] // HEAD_DIM
           == jnp.arange(MXU_TILE)[None, :] // HEAD_DIM).astype(BF16)

    row = lambda p: p.astype(F32)[:, None, :]
    heads = QK_WIDTH // HEAD_DIM
    qg = row(jnp.tile(q_gain, (1, heads))) * (HEAD_DIM ** -0.5 * LOG2E)
    kg = row(jnp.tile(k_gain, (1, heads)))
    lamv = jnp.stack([lambda_q1, lambda_k1, lambda_q2, lambda_k2], axis=1).astype(F32)
    bb, a, cc = _ssm_params(lam_re, lam_im, b_re, b_im, c_re, c_im, log_step)
    g_mix, g_ffn, g_sub, dskip = row(norm_mix), row(norm_ffn), row(subln), row(d_skip)
    wglu = w_glu.astype(BF16)

    for l in range(depth):
        lam_init = 0.8 - 0.6 * math.exp(-0.3 * l)
        q, k, vt, u = _inproj(x, g_mix, w_in, l, seg, qg, kg)
        o = _attention(lamv, g_sub, q, k, vt, bias_tiles, lam_init, l)
        s2 = _ssm(u, bb, a, cc, dskip, wglu, l)
        x = _merge_ffn(x, o, s2, g_mix, g_ffn, w_in, w_a, w_b, w_o, w1, w3, w2, l)
    return x
```

```python
import functools
import math

import jax
import jax.numpy as jnp
from jax import lax
from jax.experimental import pallas as pl
from jax.experimental.pallas import tpu as pltpu

D_MODEL = 1024
N_HEADS = 4
HEAD_DIM = 64
V_DIM = 2 * HEAD_DIM
QK_WIDTH = N_HEADS * 2 * HEAD_DIM
ATTN_WIDTH = N_HEADS * V_DIM
SSM_WIDTH = D_MODEL // 2
SSM_GROUP = 16
SSM_GROUPS = SSM_WIDTH // SSM_GROUP
SSM_STATE = 64
STATE_LANES = SSM_GROUPS * SSM_STATE
D_FF = 2816
N_BUCKETS = 32
MAX_DISTANCE = 128
EPS = 1e-6
NEG = -1e30

MXU_TILE = 256
SUBLANES = 8

TM_PROJ = 1024
TM_FFN = 1024
TQ = 256
ONES_ROWS = 16
HEADS_PER_STEP = 2
SCORE_LOOKAHEAD = 8
LOG2E = math.log2(math.e)
SCAN_STEPS = 64
SCAN_CHUNKS = 2
PERM_STEPS = 16
SCAN_STRIP = 512
FF_CHUNK = 256
VMEM_LIMIT = 62 * 1024 * 1024

BF16 = jnp.bfloat16
F32 = jnp.float32


def _dot(a, b):
    return jnp.dot(a, b, preferred_element_type=F32)


def _dot_nt(a, b):
    return lax.dot_general(a, b, (((1,), (1,)), ((), ())), preferred_element_type=F32)


def _rms(x, g):
    return x * lax.rsqrt(jnp.mean(x * x, axis=-1, keepdims=True) + EPS) * g


def _resident(shape):
    zeros = (0,) * len(shape)
    return pl.BlockSpec(shape, lambda *_: zeros, pipeline_mode=pl.Buffered(1))


def _per_layer(arr, layer):
    zeros = (0,) * (arr.ndim - 1)
    return pl.BlockSpec((None,) + arr.shape[1:], lambda *_: (layer,) + zeros,
                        pipeline_mode=pl.Buffered(1))


STAGE_ROWS = 128
STAGE_COLS = 1024
STAGE_DEPTH = 8
_STAGE_SCRATCH = [pltpu.VMEM((STAGE_DEPTH, STAGE_ROWS, STAGE_COLS), F32),
                  pltpu.SemaphoreType.DMA((STAGE_DEPTH,))]
_HBM = pl.BlockSpec(memory_space=pl.ANY)


def _stage_bf16(jobs, stage, sems):
    pieces = []
    for src, col0, dst in jobs:
        n_rows, n_cols = dst.shape
        for r0 in range(0, n_rows, STAGE_ROWS):
            for c0 in range(0, n_cols, STAGE_COLS):
                pieces.append((src, col0, dst, r0, c0, min(STAGE_COLS, n_cols - c0)))

    def copy(i):
        src, col0, _, r0, c0, nc = pieces[i]
        slot = i % STAGE_DEPTH
        return pltpu.make_async_copy(
            src.at[pl.ds(r0, STAGE_ROWS), pl.ds(col0 + c0, nc)],
            stage.at[slot, :, pl.ds(0, nc)], sems.at[slot])

    for i in range(min(STAGE_DEPTH, len(pieces))):
        copy(i).start()
    for i, (_, _, dst, r0, c0, nc) in enumerate(pieces):
        copy(i).wait()
        dst[r0:r0 + STAGE_ROWS, c0:c0 + nc] = stage[i % STAGE_DEPTH, :, 0:nc].astype(BF16)
        if i + STAGE_DEPTH < len(pieces):
            copy(i + STAGE_DEPTH).start()


def _bias_tiles_kernel(brev_ref, out_ref):
    sub = MAX_DISTANCE
    r = lax.broadcasted_iota(jnp.int32, (sub, sub), 0)
    c = lax.broadcasted_iota(jnp.int32, (sub, sub), 1)
    y = pltpu.roll(jnp.broadcast_to(brev_ref[...], (sub, sub)), 0, 1, stride=1, stride_axis=0)
    by_diag = {0: jnp.where(c <= r, y, NEG).T, 1: jnp.where(c > r, y, 0.0).T}
    for t, off in enumerate((TQ, 0)):
        for a in range(TQ // sub):
            for b in range(TQ // sub):
                d = a - b + off // sub
                fill = NEG if d < 0 else 0.0
                out_ref[t, b * sub:(b + 1) * sub, a * sub:(a + 1) * sub] = by_diag.get(
                    d, jnp.full((sub, sub), fill, F32))


def _bias_tiles(brev):
    return pl.pallas_call(
        _bias_tiles_kernel,
        grid=(N_HEADS,),
        in_specs=[pl.BlockSpec((None, 1, MAX_DISTANCE), lambda h: (h, 0, 0))],
        out_specs=pl.BlockSpec((None, 2, TQ, TQ), lambda h: (h, 0, 0, 0)),
        out_shape=jax.ShapeDtypeStruct((N_HEADS, 2, TQ, TQ), F32),
        name="bias_tiles",
    )(brev)


def _store_heads(ref, z):
    for hd in range(ref.shape[0]):
        ref[hd] = z[:, hd * V_DIM:(hd + 1) * V_DIM]


def _inproj_kernel(x_ref, g_ref, w_hbm, q_ref, k_ref, vt_ref, u_ref, w_ref, stage, sems,
                   *, layer):
    @pl.when((pl.program_id(0) == 0) & (pl.program_id(1) == 0))
    def _():
        _stage_bf16([(w_hbm.at[layer], 0, w_ref)], stage, sems)

    h = _rms(x_ref[...], g_ref[...]).astype(BF16)

    _store_heads(q_ref, _dot(h, w_ref[:, 0:QK_WIDTH]).astype(BF16))
    _store_heads(k_ref, _dot(h, w_ref[:, QK_WIDTH:2 * QK_WIDTH]).astype(BF16))
    v = _dot(h, w_ref[:, 2 * QK_WIDTH:2 * QK_WIDTH + ATTN_WIDTH])
    u_ref[...] = _dot(h, w_ref[:, 2 * QK_WIDTH + ATTN_WIDTH:]).astype(BF16)
    vt_ref[...] = v.T.astype(BF16)


def _inproj(x, g, w_in, layer):
    B, S, D = x.shape
    n_cols = 2 * QK_WIDTH + ATTN_WIDTH + SSM_WIDTH
    tok = lambda width: pl.BlockSpec((None, TM_PROJ, width), lambda b, i: (b, i, 0))
    heads = pl.BlockSpec((None, N_HEADS, TM_PROJ, V_DIM), lambda b, i: (b, 0, i, 0))
    return pl.pallas_call(
        functools.partial(_inproj_kernel, layer=layer),
        grid=(B, S // TM_PROJ),
        in_specs=[tok(D), _per_layer(g, layer), _HBM],
        out_specs=[heads, heads,
                   pl.BlockSpec((None, ATTN_WIDTH, TM_PROJ), lambda b, i: (b, 0, i)),
                   tok(SSM_WIDTH)],
        out_shape=[jax.ShapeDtypeStruct((B, N_HEADS, S, V_DIM), BF16),
                   jax.ShapeDtypeStruct((B, N_HEADS, S, V_DIM), BF16),
                   jax.ShapeDtypeStruct((B, ATTN_WIDTH, S), BF16),
                   jax.ShapeDtypeStruct((B, S, SSM_WIDTH), BF16)],
        scratch_shapes=[pltpu.VMEM((D, n_cols), BF16)] + _STAGE_SCRATCH,
        compiler_params=pltpu.CompilerParams(
            dimension_semantics=("arbitrary", "arbitrary"), vmem_limit_bytes=VMEM_LIMIT),
        name="inproj",
    )(x, g, w_in)


def _attn_kernel(lamv_ref, subln_ref, qg_ref, kg_ref, q_ref, k_ref, vt_ref, bias_ref, o_ref,
                 *, lam_init):
    lane = lax.broadcasted_iota(jnp.int32, (TQ, V_DIM), 1)
    ones = jnp.ones((ONES_ROWS, TQ), BF16)
    lv = lamv_ref[...]
    lam = (jnp.exp(jnp.sum(lv[0:1] * lv[1:2], axis=-1, keepdims=True))
           - jnp.exp(jnp.sum(lv[2:3] * lv[3:4], axis=-1, keepdims=True)) + lam_init)

    n_heads, n_keys = k_ref.shape[0], k_ref.shape[1]
    tasks = [(hd, qi, j, mp) for hd in range(n_heads) for qi in range(n_keys // TQ)
             for j in range(qi + 1) for mp in range(2)]
    qms, kns, m, acc = {}, {}, {}, {}

    def qk_norm(z, gain):
        z = z.astype(F32)
        z2 = z * z
        ss = jnp.where(lane < HEAD_DIM,
                       jnp.sum(jnp.where(lane < HEAD_DIM, z2, 0.0), axis=-1, keepdims=True),
                       jnp.sum(jnp.where(lane >= HEAD_DIM, z2, 0.0), axis=-1, keepdims=True))
        return (z * lax.rsqrt(ss * (1.0 / HEAD_DIM) + EPS) * gain).astype(BF16)

    def scores(hd, qi, j, mp):
        if (hd, qi) not in qms:
            q = qk_norm(q_ref[hd, qi * TQ:(qi + 1) * TQ, :], qg_ref[...])
            zero = jnp.zeros_like(q)
            qms[hd, qi] = (jnp.where(lane < HEAD_DIM, q, zero),
                           jnp.where(lane >= HEAD_DIM, q, zero))
        if (hd, j) not in kns:
            kns[hd, j] = qk_norm(k_ref[hd, j * TQ:(j + 1) * TQ, :], kg_ref[...])
        st = _dot_nt(kns[hd, j], qms[hd, qi][mp])
        bias = {qi: 1, qi - 1: 0}.get(j)
        return st if bias is None else st + bias_ref[hd, bias]

    def softmax_pv(hd, qi, j, mp, st):
        v_rows = slice(hd * V_DIM, (hd + 1) * V_DIM)
        v1t = jnp.concatenate([vt_ref[v_rows, j * TQ:(j + 1) * TQ], ones], axis=0)
        m_cur = jnp.max(st, axis=0, keepdims=True)
        m_new = m_cur if j == 0 else jnp.maximum(m[hd, qi, mp], m_cur)
        p = jnp.exp2(st - m_new)
        pv = _dot(v1t, p.astype(BF16))
        acc[hd, qi, mp] = (pv if j == 0
                           else jnp.exp2(m[hd, qi, mp] - m_new) * acc[hd, qi, mp] + pv)
        m[hd, qi, mp] = m_new
        if j == qi and mp == 1:
            a0, a1 = acc.pop((hd, qi, 0)), acc.pop((hd, qi, 1))
            o_t = a0[:V_DIM] / a0[V_DIM:V_DIM + 1] - lam * (a1[:V_DIM] / a1[V_DIM:V_DIM + 1])
            o_n = (o_t * lax.rsqrt(jnp.mean(o_t * o_t, axis=0, keepdims=True) + EPS)).T
            o_ref[hd, qi * TQ:(qi + 1) * TQ, :] = (
                o_n * subln_ref[...] * (1.0 - lam_init)).astype(BF16)

    pending = {}
    for i in range(len(tasks) + SCORE_LOOKAHEAD):
        if i < len(tasks):
            pending[i] = scores(*tasks[i])
        if i >= SCORE_LOOKAHEAD:
            softmax_pv(*tasks[i - SCORE_LOOKAHEAD], pending.pop(i - SCORE_LOOKAHEAD))


def _attention(lamv, subln, qg, kg, q, k, vt, bias_tiles, lam_init, layer):
    B, _, S, _ = q.shape
    hps = HEADS_PER_STEP
    head = pl.BlockSpec((None, hps, S, V_DIM), lambda b, h: (b, h, 0, 0))
    head_t = pl.BlockSpec((None, hps * V_DIM, S), lambda b, h: (b, h, 0))
    return pl.pallas_call(
        functools.partial(_attn_kernel, lam_init=lam_init),
        grid=(B, N_HEADS // hps),
        in_specs=[_per_layer(lamv, layer), _per_layer(subln, layer),
                  _per_layer(qg, layer), _per_layer(kg, layer), head, head, head_t,
                  pl.BlockSpec((hps, 2, TQ, TQ), lambda b, h: (h, 0, 0, 0))],
        out_specs=head,
        out_shape=jax.ShapeDtypeStruct((B, N_HEADS, S, V_DIM), BF16),
        compiler_params=pltpu.CompilerParams(
            dimension_semantics=("parallel", "parallel"), vmem_limit_bytes=VMEM_LIMIT),
        name="diff_attention",
    )(lamv, subln, qg, kg, q, k, vt, bias_tiles)


def _blockdiag_into(dst, stacked):
    (rows, n), cols = stacked.shape, dst.shape[1]
    m = rows // (cols // n)
    lane_in_block = lax.broadcasted_iota(jnp.int32, (n, cols), 1) % n
    tile_cols = (lane_in_block == lax.broadcasted_iota(jnp.int32, (n, cols), 0)).astype(BF16)
    tiled = _dot(stacked.astype(BF16), tile_cols)
    same_group = (lax.broadcasted_iota(jnp.int32, (rows, cols), 0) // m
                  == lax.broadcasted_iota(jnp.int32, (rows, cols), 1) // n)
    dst[...] = jnp.where(same_group, tiled, 0.0).astype(BF16)


def _ssm_kernel(u_ref, perm_ref, permt_ref, bb_ref, a_ref, cc_ref, dskip_ref, wglu_ref,
                out_ref, wb_ref, c_ref, bur, bui, xr_s, xi_s):
    @pl.when(pl.program_id(0) == 0)
    def _():
        xr_s[...] = jnp.zeros(xr_s.shape, F32)
        xi_s[...] = jnp.zeros(xi_s.shape, F32)
        for part in range(2):
            _blockdiag_into(wb_ref.at[part], bb_ref[part])
            _blockdiag_into(c_ref.at[part], cc_ref[part])


    def to_time_major(c):
        blocks = []
        for t0 in range(c * SCAN_STEPS, (c + 1) * SCAN_STEPS, PERM_STEPS):
            rows_bt = jnp.concatenate(
                [u_ref[b, t0:t0 + PERM_STEPS, :] for b in range(SUBLANES)], axis=0)
            blocks.append(_dot(perm_ref[...], rows_bt))
        return jnp.concatenate(blocks, axis=0)

    def b_project(c, u):
        ub = u.astype(BF16)
        for j in range(STATE_LANES // MXU_TILE):
            kt = (j * MXU_TILE // SSM_STATE * SSM_GROUP) // MXU_TILE
            rows = slice(kt * MXU_TILE, (kt + 1) * MXU_TILE)
            cols = slice(j * MXU_TILE, (j + 1) * MXU_TILE)
            bur[c, :, cols] = _dot(ub[:, rows], wb_ref[0, rows, cols])
            bui[c, :, cols] = _dot(ub[:, rows], wb_ref[1, rows, cols])

    def scan(c):
        for s in range(STATE_LANES // SCAN_STRIP):
            sl = slice(s * SCAN_STRIP, (s + 1) * SCAN_STRIP)
            a_r = a_ref[0, :, sl]
            a_i = a_ref[1, :, sl]
            xr = xr_s[:, sl]
            xi = xi_s[:, sl]
            for t in range(SCAN_STEPS):
                step = slice(t * SUBLANES, (t + 1) * SUBLANES)
                xr, xi = (a_r * xr - a_i * xi + bur[c, step, sl],
                          a_r * xi + a_i * xr + bui[c, step, sl])
                bur[c, step, sl] = xr
                bui[c, step, sl] = xi
            xr_s[:, sl] = xr
            xi_s[:, sl] = xi

    def c_project(c, u):
        halves = []
        k_per_half = STATE_LANES // 2
        n_per_half = SSM_WIDTH // 2
        for m in range(2):
            rows = slice(m * k_per_half, (m + 1) * k_per_half)
            cols = slice(m * n_per_half, (m + 1) * n_per_half)
            halves.append(_dot(bur[c, :, rows].astype(BF16), c_ref[0, rows, cols])
                          + _dot(bui[c, :, rows].astype(BF16), c_ref[1, rows, cols]))
        return jax.nn.gelu(jnp.concatenate(halves, axis=-1) + dskip_ref[...] * u)

    def to_batch_major(c, s):
        sb = s.astype(BF16)
        rows_per_block = PERM_STEPS * SUBLANES
        for i, t0 in enumerate(range(c * SCAN_STEPS, (c + 1) * SCAN_STEPS, PERM_STEPS)):
            rows_bt = _dot(permt_ref[...], sb[i * rows_per_block:(i + 1) * rows_per_block])
            for b in range(SUBLANES):
                out_ref[b, t0:t0 + PERM_STEPS, :] = rows_bt[
                    b * PERM_STEPS:(b + 1) * PERM_STEPS].astype(BF16)

    chunks = range(SCAN_CHUNKS)
    us = [to_time_major(c) for c in chunks]
    for c in chunks:
        b_project(c, us[c])
    for c in chunks:
        scan(c)
    ss = [c_project(c, us[c]) for c in chunks]
    ss = [s * jax.nn.sigmoid(_dot(s.astype(BF16), wglu_ref[...])) for s in ss]
    for c in chunks:
        to_batch_major(c, ss[c])


def _ssm(u, bb, a, cc, dskip, wglu, layer):
    rows = SCAN_STEPS * SUBLANES
    B, S, _ = u.shape
    r = jnp.arange(PERM_STEPS * B)
    perm = (r[None, :] == ((r % B) * PERM_STEPS + r // B)[:, None]).astype(BF16)
    blk = pl.BlockSpec((B, SCAN_CHUNKS * SCAN_STEPS, SSM_WIDTH), lambda c: (0, c, 0))
    return pl.pallas_call(
        _ssm_kernel,
        grid=(S // (SCAN_CHUNKS * SCAN_STEPS),),
        in_specs=[blk, _resident(perm.shape), _resident(perm.shape),
                  _per_layer(bb, layer), _per_layer(a, layer), _per_layer(cc, layer),
                  _per_layer(dskip, layer), _per_layer(wglu, layer)],
        out_specs=blk,
        out_shape=jax.ShapeDtypeStruct((B, S, SSM_WIDTH), BF16),
        scratch_shapes=[pltpu.VMEM((2, SSM_WIDTH, STATE_LANES), BF16),
                        pltpu.VMEM((2, STATE_LANES, SSM_WIDTH), BF16),
                        pltpu.VMEM((SCAN_CHUNKS, rows, STATE_LANES), F32),
                        pltpu.VMEM((SCAN_CHUNKS, rows, STATE_LANES), F32),
                        pltpu.VMEM((SUBLANES, STATE_LANES), F32),
                        pltpu.VMEM((SUBLANES, STATE_LANES), F32)],
        compiler_params=pltpu.CompilerParams(
            dimension_semantics=("arbitrary",), vmem_limit_bytes=VMEM_LIMIT),
        name="s5_branch",
    )(u, perm, perm.T, bb, a, cc, dskip, wglu)


def _merge_ffn_kernel(x_ref, o_ref, s_ref, g1_ref, g2_ref,
                      w_in_hbm, wa_hbm, wb_hbm, wo_hbm, w1_hbm, w3_hbm, w2_hbm, out_ref,
                      wg_ref, wa_ref, wb_ref, wo_ref, w1_ref, w3_ref, w2_ref,
                      acc_s, stage, sems, *, layer):
    @pl.when((pl.program_id(0) == 0) & (pl.program_id(1) == 0))
    def _():
        gate_col0 = w_in_hbm.shape[2] - wg_ref.shape[1]
        _stage_bf16([(w_in_hbm.at[layer], gate_col0, wg_ref), (wa_hbm.at[layer], 0, wa_ref),
                     (wb_hbm.at[layer], 0, wb_ref), (wo_hbm.at[layer], 0, wo_ref),
                     (w1_hbm.at[layer], 0, w1_ref), (w3_hbm.at[layer], 0, w3_ref),
                     (w2_hbm.at[layer], 0, w2_ref)], stage, sems)

    x = x_ref[...]
    h = _rms(x, g1_ref[...]).astype(BF16)
    o = jnp.concatenate([o_ref[hd] for hd in range(N_HEADS)], axis=1)
    mixed = jax.nn.sigmoid(_dot(h, wg_ref[:, 0:D_MODEL])) * _dot(o, wa_ref[...])
    mixed += jax.nn.sigmoid(_dot(h, wg_ref[:, D_MODEL:])) * _dot(s_ref[...], wb_ref[...])
    x1 = x + _dot(mixed.astype(BF16), wo_ref[...])

    h2 = _rms(x1, g2_ref[...]).astype(BF16)
    acc_s[...] = x1

    for c in range(D_FF // FF_CHUNK):
        cols = slice(c * FF_CHUNK, (c + 1) * FF_CHUNK)
        t = jax.nn.silu(_dot(h2, w1_ref[:, cols])) * _dot(h2, w3_ref[:, cols])
        acc_s[...] += _dot(t.astype(BF16), w2_ref[cols, :])
    out_ref[...] = acc_s[...]


def _merge_ffn(x, o, s2, g1, g2, w_in, w_a, w_b, w_o, w1, w3, w2, layer):
    B, S, D = x.shape
    tok = lambda width: pl.BlockSpec((None, TM_FFN, width), lambda b, i: (b, i, 0))
    bf16_weight = lambda shape: pltpu.VMEM(shape, BF16)
    return pl.pallas_call(
        functools.partial(_merge_ffn_kernel, layer=layer),
        grid=(B, S // TM_FFN),
        in_specs=[tok(D),
                  pl.BlockSpec((None, N_HEADS, TM_FFN, V_DIM), lambda b, i: (b, 0, i, 0)),
                  tok(SSM_WIDTH),
                  _per_layer(g1, layer), _per_layer(g2, layer)] + [_HBM] * 7,
        out_specs=tok(D),
        out_shape=jax.ShapeDtypeStruct((B, S, D), F32),
        scratch_shapes=[bf16_weight((D, 2 * D)), bf16_weight(w_a.shape[1:]),
                        bf16_weight(w_b.shape[1:]), bf16_weight(w_o.shape[1:]),
                        bf16_weight(w1.shape[1:]), bf16_weight(w3.shape[1:]),
                        bf16_weight(w2.shape[1:]),
                        pltpu.VMEM((TM_FFN, D), F32)] + _STAGE_SCRATCH,
        compiler_params=pltpu.CompilerParams(
            dimension_semantics=("arbitrary", "arbitrary"), vmem_limit_bytes=VMEM_LIMIT),
        name="merge_ffn",
    )(x, o, s2, g1, g2, w_in, w_a, w_b, w_o, w1, w3, w2)


def _t5_bucket(n):
    max_exact = N_BUCKETS // 2
    is_small = n < max_exact
    nf = jnp.maximum(n, 1).astype(F32)
    large = max_exact + (jnp.log(nf / max_exact) / math.log(MAX_DISTANCE / max_exact)
                         * (N_BUCKETS - max_exact)).astype(jnp.int32)
    large = jnp.minimum(large, N_BUCKETS - 1)
    return jnp.where(is_small, n, large)


def _ssm_params(lam_re, lam_im, b_re, b_im, c_re, c_im, log_step):
    step = jnp.exp(log_step.astype(F32))[..., None]
    lr = lam_re.astype(F32)
    li = lam_im.astype(F32)
    decay = jnp.exp(lr * step)
    ab_re = decay * jnp.cos(li * step)
    ab_im = decay * jnp.sin(li * step)
    nr = ab_re - 1.0
    ni = ab_im
    den = lr * lr + li * li
    f_re = (nr * lr + ni * li) / den
    f_im = (ni * lr - nr * li) / den
    br = b_re.astype(F32)
    bi = b_im.astype(F32)
    bb_re = f_re[..., None] * br - f_im[..., None] * bi
    bb_im = f_re[..., None] * bi + f_im[..., None] * br

    def stack_transposed(w):
        *lead, g, n, m = w.shape
        return jnp.swapaxes(w, -1, -2).reshape(*lead, g * m, n)

    depth = lr.shape[0]
    a = jnp.stack([ab_re, ab_im], axis=1).reshape(depth, 2, 1, STATE_LANES)
    return (stack_transposed(jnp.stack([bb_re, bb_im], axis=1)),
            jnp.broadcast_to(a, (depth, 2, SUBLANES, STATE_LANES)),
            stack_transposed(jnp.stack([c_re.astype(F32), -c_im.astype(F32)], axis=1)))


def kernel(x, rel_bias, norm_mix, w_in, q_gain, k_gain, lambda_q1, lambda_k1, lambda_q2, lambda_k2, subln, w_a, lam_re, lam_im, b_re, b_im, c_re, c_im, d_skip, log_step, w_glu, w_b, w_o, norm_ffn, w1, w3, w2):
    B, S, D = x.shape
    depth = w_in.shape[0]
    assert B == SUBLANES and D == D_MODEL

    table = rel_bias.astype(F32)
    dist = (-jnp.arange(MAX_DISTANCE)) % MAX_DISTANCE
    brev = (table[_t5_bucket(dist)] - table[N_BUCKETS - 1][None]).T * LOG2E
    bias_tiles = _bias_tiles(brev[:, None, :])

    row = lambda p: p.astype(F32)[:, None, :]
    heads = V_DIM // HEAD_DIM
    qg = row(jnp.tile(q_gain, (1, heads))) * (HEAD_DIM ** -0.5 * LOG2E)
    kg = row(jnp.tile(k_gain, (1, heads)))
    lamv = jnp.stack([lambda_q1, lambda_k1, lambda_q2, lambda_k2], axis=1).astype(F32)
    bb, a, cc = _ssm_params(lam_re, lam_im, b_re, b_im, c_re, c_im, log_step)
    g_mix, g_ffn, g_sub, dskip = row(norm_mix), row(norm_ffn), row(subln), row(d_skip)
    wglu = w_glu.astype(BF16)

    for l in range(depth):
        lam_init = 0.8 - 0.6 * math.exp(-0.3 * l)
        q, k, vt, u = _inproj(x, g_mix, w_in, l)
        o = _attention(lamv, g_sub, qg, kg, q, k, vt, bias_tiles, lam_init, l)
        s2 = _ssm(u, bb, a, cc, dskip, wglu, l)
        x = _merge_ffn(x, o, s2, g_mix, g_ffn, w_in, w_a, w_b, w_o, w1, w3, w2, l)
    return x
```
